```python
import math
import jax, jax.numpy as jnp
from jax import lax
import numpy as np


D_MODEL = 1024
BATCH = 2
SEQ = 8192
DEPTH = 2
DEC_BATCH = 8
DEC_SEQ = 16
PAST_LEN = 1024

CHUNK = 64
PLE_DIM = 256
GMLP_CHUNK = 128
GMLP_GROUPS = 4
D_AV = D_MODEL // 2
GMLP_GROUP_W = D_AV // GMLP_GROUPS
FOX_HD = 64
D_B = D_MODEL // 2
FOX_HEADS = D_B // FOX_HD
Q_BLOCK = 128
FORGET_BIAS_INIT = 3.0
D_C = D_MODEL
SSM_HD = 64
SSM_HEADS = D_C // SSM_HD
SSM_N = 128
SSM_GROUPS = 2
CONV_W = 4
CONV_DIM = D_C + 2 * SSM_GROUPS * SSM_N
N_BRANCH = 3
N_IN = 2 * D_AV + 3 * D_B + FOX_HEADS + D_C + CONV_DIM + SSM_HEADS + N_BRANCH * D_MODEL
D_FF = 11 * D_MODEL // 4
N_EXPERTS = 8
TOP_K = 2
D_FF_EXPERT = D_FF // 2
N_DENSE = (DEPTH + 1) // 2
N_MOE = DEPTH // 2
ALPHA = (2.0 * DEPTH) ** 0.25
BETA = (8.0 * DEPTH) ** -0.25
LN_EPS = 1e-5
RMS_EPS = 1e-5
NEG_INF = -1e30

kernel_name = 'hybrid_gmlp_fox_ssd_stream_step'


def _layer_norm(x, g, b):
    xf = x.astype(jnp.float32)
    mu = jnp.mean(xf, axis=-1, keepdims=True)
    var = jnp.mean(jnp.square(xf - mu), axis=-1, keepdims=True)
    y = (xf - mu) * lax.rsqrt(var + LN_EPS) * g.astype(jnp.float32) + b.astype(jnp.float32)
    return y.astype(x.dtype)


def _swiglu(h, wg, wu, wd):
    return (jax.nn.silu(h @ wg) * (h @ wu)) @ wd


def _moe_swiglu(h, wr, br, wg, wu, wd):
    logits = (h @ wr).astype(jnp.float32) + br.astype(jnp.float32)
    top_v, top_i = lax.top_k(logits, TOP_K)
    top_w = jax.nn.softmax(top_v, axis=-1)
    combine = jnp.sum(jax.nn.one_hot(top_i, N_EXPERTS, dtype=jnp.float32) * top_w[..., None], axis=-2)
    out = jnp.zeros(h.shape, h.dtype)
    for e in range(N_EXPERTS):
        out = out + combine[..., e:e + 1].astype(h.dtype) * _swiglu(h, wg[e], wu[e], wd[e])
    return out


def _chunk_mlp(u, v, w_s, b_s):
    bsz, T, _ = v.shape
    L = min(GMLP_CHUNK, T)
    pos = np.arange(L)
    mask = (pos[None, :] // CHUNK) <= (pos[:, None] // CHUNK)
    w = jnp.where(mask[None], w_s[:, :L, :L], 0.0)
    vb = v.reshape(bsz, T // L, L, GMLP_GROUPS, GMLP_GROUP_W)
    mixed = jnp.einsum('gij,bnjgc->bnigc', w, vb)
    mixed = mixed + jnp.transpose(b_s[:, :L])[None, None, :, :, None]
    return u * mixed.reshape(bsz, T, D_AV)


def _fox_attend(q, k, v, F, q_off):
    bsz, Tq, H, hd = q.shape
    Tk = k.shape[1]
    blk = Q_BLOCK if Tq % Q_BLOCK == 0 else Tq
    nb = Tq // blk
    scale = hd ** -0.5
    Ft = jnp.transpose(F, (0, 2, 1))
    Fq_all = Ft[:, :, q_off:]
    kpos = jnp.arange(Tk)

    def one_block(bi):
        start = bi * blk
        qb = lax.dynamic_slice_in_dim(q, start, blk, axis=1)
        fq = lax.dynamic_slice_in_dim(Fq_all, start, blk, axis=2)
        qpos = q_off + start + jnp.arange(blk)
        s = jnp.einsum('bqhd,bkhd->bhqk', qb, k, preferred_element_type=jnp.float32) * scale
        s = s + fq[..., None] - Ft[:, :, None, :]
        s = jnp.where(kpos[None, :] <= qpos[:, None], s, NEG_INF)
        p = jax.nn.softmax(s, axis=-1).astype(v.dtype)
        return jnp.einsum('bhqk,bkhd->bqhd', p, v)

    out = lax.map(one_block, jnp.arange(nb))
    return jnp.moveaxis(out, 0, 1).reshape(bsz, Tq, H, hd)


def _causal_conv(xbc, conv0, w, b):
    T = xbc.shape[1]
    xpad = jnp.concatenate([conv0.astype(xbc.dtype), xbc], axis=1)
    y = b
    for j in range(CONV_W):
        y = y + xpad[:, j:j + T] * w[j]
    return y, xpad[:, T:]


def _ssd(x, dt, A, Bm, Cm, h0):
    bsz, T, H, P = x.shape
    N = Bm.shape[-1]
    L = min(CHUNK, T)
    nc = T // L
    xc = x.astype(jnp.float32).reshape(bsz, nc, L, H, P)
    Bc = Bm.astype(jnp.float32).reshape(bsz, nc, L, H, N)
    Cc = Cm.astype(jnp.float32).reshape(bsz, nc, L, H, N)
    dtc = dt.reshape(bsz, nc, L, H)
    a_cum = jnp.cumsum(dtc * A, axis=2)
    seg = a_cum[:, :, :, None, :] - a_cum[:, :, None, :, :]
    causal = np.tril(np.ones((L, L), dtype=bool))[None, None, :, :, None]
    decay = jnp.exp(jnp.where(causal, seg, -jnp.inf))
    cb = jnp.einsum('bclhn,bcshn->bclsh', Cc, Bc)
    y_diag = jnp.einsum('bclsh,bcsh,bcshp->bclhp', cb * decay, dtc, xc)
    decay_end = jnp.exp(a_cum[:, :, -1:, :] - a_cum)
    states = jnp.einsum('bclhn,bclh,bclhp->bchpn', Bc, decay_end * dtc, xc)
    chunk_decay = jnp.exp(a_cum[:, :, -1, :])

    def step(h, inp):
        st, cd = inp
        return cd[:, :, None, None] * h + st, h

    h_final, h_prev = lax.scan(step, h0.astype(jnp.float32),
                               (jnp.moveaxis(states, 1, 0), jnp.moveaxis(chunk_decay, 1, 0)))
    h_prev = jnp.moveaxis(h_prev, 0, 1)
    y_off = jnp.einsum('bclhn,bchpn,bclh->bclhp', Cc, h_prev, jnp.exp(a_cum))
    y = (y_diag + y_off).reshape(bsz, T, H, P).astype(x.dtype)
    return y, h_final


def _token_mixers(x, i, W, fox_cache, ssm0, conv0):
    bsz, T, _ = x.shape
    sizes = [D_AV, D_AV, D_B, D_B, D_B, FOX_HEADS, D_C, CONV_DIM, SSM_HEADS, N_BRANCH * D_MODEL]
    points = [int(s) for s in np.cumsum(sizes)[:-1]]
    proj = x @ W['w_in'][i]
    a_u, a_v, a_q, a_k, a_va, a_f, a_z, a_xbc, a_dt, a_gate = jnp.split(proj, points, axis=-1)

    u = jax.nn.gelu(a_u)
    v_n = _layer_norm(jax.nn.gelu(a_v), W['gmlp_ln_g'][i], W['gmlp_ln_b'][i])
    y_a = _chunk_mlp(u, v_n, W['gmlp_w_spatial'][i], W['gmlp_b_spatial'][i])

    q = a_q.reshape(bsz, T, FOX_HEADS, FOX_HD)
    k = a_k.reshape(bsz, T, FOX_HEADS, FOX_HD)
    v = a_va.reshape(bsz, T, FOX_HEADS, FOX_HD)
    logf = jax.nn.log_sigmoid(a_f.astype(jnp.float32) + W['fox_b_forget'][i].astype(jnp.float32))
    if fox_cache is None:
        k_all, v_all, logf_all, q_off = k, v, logf, 0
    else:
        k_c, v_c, logf_c = fox_cache
        k_all = jnp.concatenate([k_c.astype(k.dtype), k], axis=1)
        v_all = jnp.concatenate([v_c.astype(v.dtype), v], axis=1)
        logf_all = jnp.concatenate([logf_c.astype(jnp.float32), logf], axis=1)
        q_off = k_c.shape[1]
    F = jnp.cumsum(logf_all, axis=1)
    y_b = _fox_attend(q, k_all, v_all, F, q_off).reshape(bsz, T, D_B)

    xbc, conv_new = _causal_conv(a_xbc, conv0, W['ssm_conv_w'][i], W['ssm_conv_b'][i])
    xbc = jax.nn.silu(xbc)
    xs, Bm, Cm = jnp.split(xbc, [D_C, D_C + SSM_GROUPS * SSM_N], axis=-1)
    xh = xs.reshape(bsz, T, SSM_HEADS, SSM_HD)
    rep = SSM_HEADS // SSM_GROUPS
    Bh = jnp.repeat(Bm.reshape(bsz, T, SSM_GROUPS, SSM_N), rep, axis=2)
    Ch = jnp.repeat(Cm.reshape(bsz, T, SSM_GROUPS, SSM_N), rep, axis=2)
    dt = jax.nn.softplus(a_dt.astype(jnp.float32) + W['ssm_dt_bias'][i].astype(jnp.float32))
    A = -jnp.exp(W['ssm_a_log'][i].astype(jnp.float32))
    y_ssd, h_final = _ssd(xh, dt, A, Bh, Ch, ssm0)
    y_ssd = y_ssd + W['ssm_d'][i][None, None, :, None] * xh
    yg = (y_ssd.reshape(bsz, T, D_C) * jax.nn.silu(a_z)).astype(jnp.float32)
    yg = yg * lax.rsqrt(jnp.mean(jnp.square(yg), axis=-1, keepdims=True) + RMS_EPS)
    y_c = (yg * W['ssm_norm_g'][i].astype(jnp.float32)).astype(x.dtype)

    g = jax.nn.sigmoid(a_gate).reshape(bsz, T, N_BRANCH, D_MODEL)
    merged = (g[:, :, 0] * (y_a @ W['w_branch_a'][i])
              + g[:, :, 1] * (y_b @ W['w_branch_b'][i])
              + g[:, :, 2] * (y_c @ W['w_branch_c'][i]))
    mix = merged @ W['w_out'][i]
    return mix, (k, v, logf, h_final, conv_new, v_n)


def _trunk_layer(x, p, i, W, fox_cache, ssm0, conv0):
    mix, states = _token_mixers(x, i, W, fox_cache, ssm0, conv0)
    x = _layer_norm(ALPHA * x + mix, W['ln1_g'][i], W['ln1_b'][i])
    j = i // 2
    if i % 2 == 0:
        ffn = _swiglu(x, W['ffn_w_gate'][j], W['ffn_w_up'][j], W['ffn_w_down'][j])
    else:
        ffn = _moe_swiglu(x, W['moe_w_router'][j], W['moe_b_router'][j],
                          W['moe_w_gate'][j], W['moe_w_up'][j], W['moe_w_down'][j])
    ple = (p @ W['ple_w_proj'][i]) * jax.nn.sigmoid(x @ W['ple_w_gate'][i])
    x = _layer_norm(ALPHA * x + ffn + ple, W['ln2_g'][i], W['ln2_b'][i])
    return x, states


def setup_inputs(seed: int = 0) -> dict:
    key = jax.random.key(seed)
    ks = iter(jax.random.split(key, 64))
    f32 = jnp.float32

    def nrm(shape, scale=1.0):
        return scale * jax.random.normal(next(ks), shape, f32)

    d = D_MODEL
    inputs = {}
    inputs['x_prompt'] = nrm((BATCH, SEQ, d))
    inputs['x_sample'] = nrm((DEC_BATCH, DEC_SEQ, d))
    inputs['p_prompt'] = nrm((DEPTH, BATCH, SEQ, PLE_DIM))
    inputs['p_sample'] = nrm((DEPTH, DEC_BATCH, DEC_SEQ, PLE_DIM))
    inputs['cache_fox_k'] = nrm((DEPTH, DEC_BATCH, PAST_LEN, FOX_HEADS, FOX_HD))
    inputs['cache_fox_v'] = nrm((DEPTH, DEC_BATCH, PAST_LEN, FOX_HEADS, FOX_HD))
    inputs['cache_fox_logf'] = jax.nn.log_sigmoid(FORGET_BIAS_INIT + nrm((DEPTH, DEC_BATCH, PAST_LEN, FOX_HEADS)))
    inputs['state_ssm'] = nrm((DEPTH, DEC_BATCH, SSM_HEADS, SSM_HD, SSM_N), 0.1)
    inputs['state_conv'] = nrm((DEPTH, DEC_BATCH, CONV_W - 1, CONV_DIM))
    inputs['ln0_g'] = 1.0 + nrm((d,), 0.05)
    inputs['ln0_b'] = nrm((d,), 0.02)
    inputs['w_in'] = nrm((DEPTH, d, N_IN), d ** -0.5)
    inputs['fox_b_forget'] = FORGET_BIAS_INIT + nrm((DEPTH, FOX_HEADS), 0.1)
    inputs['gmlp_ln_g'] = 1.0 + nrm((DEPTH, D_AV), 0.05)
    inputs['gmlp_ln_b'] = nrm((DEPTH, D_AV), 0.02)
    inputs['gmlp_w_spatial'] = nrm((DEPTH, GMLP_GROUPS, GMLP_CHUNK, GMLP_CHUNK), GMLP_CHUNK ** -0.5)
    inputs['gmlp_b_spatial'] = 1.0 + nrm((DEPTH, GMLP_GROUPS, GMLP_CHUNK), 0.1)
    inputs['ssm_conv_w'] = nrm((DEPTH, CONV_W, CONV_DIM), CONV_W ** -0.5)
    inputs['ssm_conv_b'] = nrm((DEPTH, CONV_DIM), 0.02)
    dt0 = jnp.exp(jax.random.uniform(next(ks), (DEPTH, SSM_HEADS), f32, math.log(1e-3), math.log(1e-1)))
    inputs['ssm_dt_bias'] = dt0 + jnp.log(-jnp.expm1(-dt0))
    inputs['ssm_a_log'] = jnp.log(jax.random.uniform(next(ks), (DEPTH, SSM_HEADS), f32, 1.0, 16.0))
    inputs['ssm_d'] = 1.0 + nrm((DEPTH, SSM_HEADS), 0.1)
    inputs['ssm_norm_g'] = 1.0 + nrm((DEPTH, D_C), 0.05)
    inputs['w_branch_a'] = nrm((DEPTH, D_AV, d), D_AV ** -0.5)
    inputs['w_branch_b'] = nrm((DEPTH, D_B, d), D_B ** -0.5)
    inputs['w_branch_c'] = nrm((DEPTH, D_C, d), D_C ** -0.5)
    inputs['w_out'] = nrm((DEPTH, d, d), BETA * d ** -0.5)
    inputs['ln1_g'] = 1.0 + nrm((DEPTH, d), 0.05)
    inputs['ln1_b'] = nrm((DEPTH, d), 0.02)
    inputs['ln2_g'] = 1.0 + nrm((DEPTH, d), 0.05)
    inputs['ln2_b'] = nrm((DEPTH, d), 0.02)
    inputs['ffn_w_gate'] = nrm((N_DENSE, d, D_FF), d ** -0.5)
    inputs['ffn_w_up'] = nrm((N_DENSE, d, D_FF), d ** -0.5)
    inputs['ffn_w_down'] = nrm((N_DENSE, D_FF, d), BETA * D_FF ** -0.5)
    inputs['moe_w_router'] = nrm((N_MOE, d, N_EXPERTS), d ** -0.5)
    inputs['moe_b_router'] = nrm((N_MOE, N_EXPERTS), 0.01)
    inputs['moe_w_gate'] = nrm((N_MOE, N_EXPERTS, d, D_FF_EXPERT), d ** -0.5)
    inputs['moe_w_up'] = nrm((N_MOE, N_EXPERTS, d, D_FF_EXPERT), d ** -0.5)
    inputs['moe_w_down'] = nrm((N_MOE, N_EXPERTS, D_FF_EXPERT, d), BETA * D_FF_EXPERT ** -0.5)
    inputs['ple_w_proj'] = nrm((DEPTH, PLE_DIM, d), PLE_DIM ** -0.5)
    inputs['ple_w_gate'] = nrm((DEPTH, d, d), d ** -0.5)
    return inputs


def reference(x_prompt, x_sample, p_prompt, p_sample, cache_fox_k, cache_fox_v, cache_fox_logf,
              state_ssm, state_conv, ln0_g, ln0_b, w_in, fox_b_forget, gmlp_ln_g, gmlp_ln_b,
              gmlp_w_spatial, gmlp_b_spatial, ssm_conv_w, ssm_conv_b, ssm_dt_bias, ssm_a_log, ssm_d,
              ssm_norm_g, w_branch_a, w_branch_b, w_branch_c, w_out, ln1_g, ln1_b, ln2_g, ln2_b,
              ffn_w_gate, ffn_w_up, ffn_w_down, moe_w_router, moe_b_router, moe_w_gate, moe_w_up,
              moe_w_down, ple_w_proj, ple_w_gate):
    W = dict(w_in=w_in, fox_b_forget=fox_b_forget, gmlp_ln_g=gmlp_ln_g, gmlp_ln_b=gmlp_ln_b,
             gmlp_w_spatial=gmlp_w_spatial, gmlp_b_spatial=gmlp_b_spatial, ssm_conv_w=ssm_conv_w,
             ssm_conv_b=ssm_conv_b, ssm_dt_bias=ssm_dt_bias, ssm_a_log=ssm_a_log, ssm_d=ssm_d,
             ssm_norm_g=ssm_norm_g, w_branch_a=w_branch_a, w_branch_b=w_branch_b,
             w_branch_c=w_branch_c, w_out=w_out, ln1_g=ln1_g, ln1_b=ln1_b, ln2_g=ln2_g, ln2_b=ln2_b,
             ffn_w_gate=ffn_w_gate, ffn_w_up=ffn_w_up, ffn_w_down=ffn_w_down,
             moe_w_router=moe_w_router, moe_b_router=moe_b_router, moe_w_gate=moe_w_gate,
             moe_w_up=moe_w_up, moe_w_down=moe_w_down, ple_w_proj=ple_w_proj, ple_w_gate=ple_w_gate)

    xp = _layer_norm(x_prompt, ln0_g, ln0_b)
    xs = _layer_norm(x_sample, ln0_g, ln0_b)
    bp = x_prompt.shape[0]
    pk, pv, pf, ph, pc = [], [], [], [], []
    sk, sv, sf, sh, sc, sg = [], [], [], [], [], []
    for i in range(DEPTH):
        ssm0 = jnp.zeros((bp, SSM_HEADS, SSM_HD, SSM_N), jnp.float32)
        conv0 = jnp.zeros((bp, CONV_W - 1, CONV_DIM), xp.dtype)
        xp, (k_p, v_p, f_p, h_p, c_p, _) = _trunk_layer(xp, p_prompt[i], i, W, None, ssm0, conv0)
        xs, (k_s, v_s, f_s, h_s, c_s, g_s) = _trunk_layer(
            xs, p_sample[i], i, W, (cache_fox_k[i], cache_fox_v[i], cache_fox_logf[i]),
            state_ssm[i], state_conv[i])
        pk.append(k_p); pv.append(v_p); pf.append(f_p); ph.append(h_p); pc.append(c_p)
        sk.append(k_s); sv.append(v_s); sf.append(f_s); sh.append(h_s); sc.append(c_s); sg.append(g_s)

    return (xp, xs,
            jnp.stack(pk), jnp.stack(pv), jnp.stack(pf), jnp.stack(ph), jnp.stack(pc),
            jnp.stack(sk), jnp.stack(sv), jnp.stack(sf), jnp.stack(sh), jnp.stack(sc), jnp.stack(sg))
```

```python
import functools

import numpy as np
import jax
import jax.numpy as jnp
from jax import lax
from jax.experimental import pallas as pl
from jax.experimental.pallas import tpu as pltpu

F32 = jnp.float32
BF16 = jnp.bfloat16

D_MODEL = 1024
DEPTH = 2
CHUNK = 64
PLE_DIM = 256
GMLP_CHUNK = 128
GMLP_GROUPS = 4
D_AV = D_MODEL // 2
GMLP_GROUP_W = D_AV // GMLP_GROUPS
FOX_HD = 64
D_B = D_MODEL // 2
FOX_HEADS = D_B // FOX_HD
D_C = D_MODEL
SSM_HD = 64
SSM_HEADS = D_C // SSM_HD
SSM_N = 128
SSM_GROUPS = 2
CONV_W = 4
CONV_DIM = D_C + 2 * SSM_GROUPS * SSM_N
N_BRANCH = 3
D_FF = 11 * D_MODEL // 4
N_EXPERTS = 8
D_FF_EXPERT = D_FF // 2
ALPHA = (2.0 * DEPTH) ** 0.25
LN_EPS = 1e-5
RMS_EPS = 1e-5
NEG_INF = -1e30

LANES = 128
VMEM_LIMIT = 56 * 2**20
N_MAIN = CONV_DIM + 2 * D_AV + 3 * D_B + D_C + N_BRANCH * D_MODEL
OFF_XBC, OFF_U, OFF_V, OFF_Q, OFF_K, OFF_VA, OFF_Z, OFF_GATE = 0, 1536, 2048, 2560, 3072, 3584, 4096, 5120
SSD_L = 128
CONV_PAD = 8


def _cparams(sem):
    return pltpu.CompilerParams(dimension_semantics=sem, vmem_limit_bytes=VMEM_LIMIT)


def _sigmoid(x):
    return 1.0 / (1.0 + jnp.exp(-x))


def _softplus(x):
    return jnp.maximum(x, 0.0) + jnp.log1p(jnp.exp(-jnp.abs(x)))


def _gelu(x):
    c = np.float32(np.sqrt(2.0 / np.pi))
    return x * (0.5 * (1.0 + jnp.tanh(c * (x + 0.044715 * (x * x * x)))))


def _ln_rows(x, g, b):
    mu = jnp.mean(x, axis=-1, keepdims=True)
    xc = x - mu
    var = jnp.mean(xc * xc, axis=-1, keepdims=True)
    return xc * lax.rsqrt(var + LN_EPS) * g + b


def _split3(x):
    hi = x.astype(BF16)
    r1 = x - hi.astype(F32)
    mid = r1.astype(BF16)
    lo = (r1 - mid.astype(F32)).astype(BF16)
    return hi, mid, lo


def _dot(a, b):
    return jnp.dot(a, b, preferred_element_type=F32)


def _dot_nt(a, b):
    return lax.dot_general(a, b, (((1,), (1,)), ((), ())), preferred_element_type=F32)


def _dot_tn(a, b):
    return lax.dot_general(a, b, (((0,), (0,)), ((), ())), preferred_element_type=F32)


def _ln_kernel(x_ref, g_ref, b_ref, o_ref, ob_ref):
    y = _ln_rows(x_ref[...], g_ref[...], b_ref[...])
    o_ref[...] = y
    ob_ref[...] = y.astype(BF16)


def _layer_norm_call(x, g, b, tm):
    n, d = x.shape
    row = pl.BlockSpec((tm, d), lambda i: (i, 0))
    vec = pl.BlockSpec((1, d), lambda i: (0, 0))
    return pl.pallas_call(
        _ln_kernel, grid=(n // tm,), in_specs=[row, vec, vec], out_specs=[row, row],
        out_shape=[jax.ShapeDtypeStruct((n, d), F32), jax.ShapeDtypeStruct((n, d), BF16)],
        compiler_params=_cparams(("parallel",)), name="ln_in")(x, g.reshape(1, d), b.reshape(1, d))


def _mm_kernel(x_ref, w_ref, o_ref):
    o_ref[...] = _dot(x_ref[...], w_ref[...])


def _matmul_call(x, w, tm, tn, name):
    m, k = x.shape
    n = w.shape[1]
    return pl.pallas_call(
        _mm_kernel, grid=(m // tm, n // tn),
        in_specs=[pl.BlockSpec((tm, k), lambda i, j: (i, 0)), pl.BlockSpec((k, tn), lambda i, j: (0, j))],
        out_specs=pl.BlockSpec((tm, tn), lambda i, j: (i, j)),
        out_shape=jax.ShapeDtypeStruct((m, n), F32),
        compiler_params=_cparams(("parallel", "parallel")), name=name)(x, w)


def _gmlp_kernel(u_ref, v_ref, g_ref, b_ref, w_ref, bias_ref, ya_ref, vn_ref, *, rows):
    u = _gelu(u_ref[...])
    vn = _ln_rows(_gelu(v_ref[...]), g_ref[...], b_ref[...])
    vn_ref[...] = vn
    vb = vn.astype(BF16)
    for c in range(rows // GMLP_CHUNK):
        r0 = c * GMLP_CHUNK
        for g in range(GMLP_GROUPS):
            c0 = g * GMLP_GROUP_W
            mixed = _dot(w_ref[g], vb[r0:r0 + GMLP_CHUNK, c0:c0 + GMLP_GROUP_W]) + bias_ref[:, c0:c0 + GMLP_GROUP_W]
            ya_ref[r0:r0 + GMLP_CHUNK, c0:c0 + GMLP_GROUP_W] = (
                u[r0:r0 + GMLP_CHUNK, c0:c0 + GMLP_GROUP_W] * mixed).astype(BF16)


def _gmlp_call(proj, ln_g, ln_b, w_eff, bias_eff, rows):
    n = proj.shape[0]
    ublk, vblk = OFF_U // D_AV, OFF_V // D_AV
    vec = pl.BlockSpec((1, D_AV), lambda i: (0, 0))
    return pl.pallas_call(
        functools.partial(_gmlp_kernel, rows=rows), grid=(n // rows,),
        in_specs=[pl.BlockSpec((rows, D_AV), lambda i: (i, ublk)), pl.BlockSpec((rows, D_AV), lambda i: (i, vblk)),
                  vec, vec,
                  pl.BlockSpec((GMLP_GROUPS, GMLP_CHUNK, GMLP_CHUNK), lambda i: (0, 0, 0)),
                  pl.BlockSpec((GMLP_CHUNK, D_AV), lambda i: (0, 0))],
        out_specs=[pl.BlockSpec((rows, D_AV), lambda i: (i, 0)), pl.BlockSpec((rows, D_AV), lambda i: (i, 0))],
        out_shape=[jax.ShapeDtypeStruct((n, D_AV), BF16), jax.ShapeDtypeStruct((n, D_AV), F32)],
        compiler_params=_cparams(("parallel",)), name="gmlp")(
            proj, proj, ln_g.reshape(1, D_AV), ln_b.reshape(1, D_AV), w_eff, bias_eff)


def _gmlp_weights(w_s, b_s, t):
    l = min(GMLP_CHUNK, t)
    pos = np.arange(l)
    mask = (pos[None, :] // CHUNK) <= (pos[:, None] // CHUNK)
    w = jnp.where(mask[None], w_s[:, :l, :l], 0.0)
    bias = jnp.transpose(b_s[:, :l])
    reps = GMLP_CHUNK // l
    if reps > 1:
        eye = jnp.eye(reps, dtype=w.dtype)
        w = jnp.einsum("ab,gij->gaibj", eye, w).reshape(GMLP_GROUPS, GMLP_CHUNK, GMLP_CHUNK)
        bias = jnp.tile(bias, (reps, 1))
    return w.astype(BF16), jnp.repeat(bias, GMLP_GROUP_W, axis=1)


def _logsig_kernel(a_ref, b_ref, o_ref):
    o_ref[...] = -_softplus(-(a_ref[...] + b_ref[...]))


def _logsig_call(a_t, bias):
    bsz, h, t = a_t.shape
    blk = pl.BlockSpec((None, h, t), lambda b: (b, 0, 0))
    return pl.pallas_call(
        _logsig_kernel, grid=(bsz,), in_specs=[blk, pl.BlockSpec((h, 1), lambda b: (0, 0))], out_specs=blk,
        out_shape=jax.ShapeDtypeStruct((bsz, h, t), F32),
        compiler_params=_cparams(("parallel",)), name="fox_logf")(a_t, bias.reshape(h, 1))


def _cumsum_kernel(x_ref, tri_ref, o_ref, carry_ref):
    @pl.when(pl.program_id(1) == 0)
    def _():
        carry_ref[...] = jnp.zeros_like(carry_ref)

    tri = tri_ref[...]
    hi, mid, lo = _split3(x_ref[...])
    out = (_dot(hi, tri) + _dot(mid, tri)) + _dot(lo, tri) + carry_ref[:, 0:1]
    o_ref[...] = out
    tc = out.shape[1]
    carry_ref[...] = jnp.broadcast_to(out[:, tc - 1:tc], carry_ref.shape)


def _cumsum_call(x_t, tc):
    bsz, h, t = x_t.shape
    tri = jnp.triu(jnp.ones((tc, tc), F32)).astype(BF16)
    blk = pl.BlockSpec((None, h, tc), lambda b, j: (b, 0, j))
    return pl.pallas_call(
        _cumsum_kernel, grid=(bsz, t // tc),
        in_specs=[blk, pl.BlockSpec((tc, tc), lambda b, j: (0, 0))], out_specs=blk,
        out_shape=jax.ShapeDtypeStruct((bsz, h, t), F32),
        scratch_shapes=[pltpu.VMEM((h, LANES), F32)],
        compiler_params=_cparams(("parallel", "arbitrary")), name="fox_cumsum")(x_t, tri)


def _fox_kernel(q_ref, k_ref, v_ref, f_ref, o_ref, m_sc, l_sc, acc_sc, *, tq, tk, q_off):
    i = pl.program_id(2)
    q = (q_ref[...] * (FOX_HD ** -0.5)).astype(BF16)
    m_sc[...] = jnp.full_like(m_sc, -jnp.inf)
    l_sc[...] = jnp.zeros_like(l_sc)
    acc_sc[...] = jnp.zeros_like(acc_sc)
    first_q = q_off + i * tq

    def step(j, masked):
        k0 = j * tk if isinstance(j, int) else pl.multiple_of(j * tk, tk)
        s = _dot_nt(q, k_ref[pl.ds(k0, tk), :]) - f_ref[:, pl.ds(k0, tk)]
        if masked:
            kpos = k0 + lax.broadcasted_iota(jnp.int32, (tq, tk), 1)
            qpos = first_q + lax.broadcasted_iota(jnp.int32, (tq, tk), 0)
            s = jnp.where(kpos <= qpos, s, NEG_INF)
        m_prev = m_sc[...]
        m_new = jnp.maximum(m_prev, jnp.max(s, axis=-1, keepdims=True))
        alpha = jnp.exp(m_prev - m_new)
        p = jnp.exp(s - m_new)
        l_sc[...] = alpha * l_sc[...] + jnp.sum(p, axis=-1, keepdims=True)
        acc_sc[...] = alpha * acc_sc[...] + _dot(p.astype(BF16), v_ref[pl.ds(k0, tk), :])
        m_sc[...] = m_new

    n_full = (first_q + 1) // tk
    n_last = (first_q + tq - 1) // tk

    def full_body(j, c):
        step(j, False)
        return c

    def diag_body(j, c):
        step(j, True)
        return c

    if k_ref.shape[0] == tk:
        step(0, True)
    else:
        lax.fori_loop(0, n_full, full_body, 0)
        lax.fori_loop(n_full, n_last + 1, diag_body, 0)
    o_ref[...] = (acc_sc[...] / l_sc[...]).astype(o_ref.dtype)


def _fox_call(q, k, v, f_t, tq, tk, q_off):
    bsz, h, t_q, hd = q.shape
    t_k = k.shape[2]
    kv = pl.BlockSpec((None, None, t_k, hd), lambda b, hh, i: (b, hh, 0, 0))
    qo = pl.BlockSpec((None, None, tq, hd), lambda b, hh, i: (b, hh, i, 0))
    return pl.pallas_call(
        functools.partial(_fox_kernel, tq=tq, tk=tk, q_off=q_off), grid=(bsz, h, t_q // tq),
        in_specs=[qo, kv, kv, pl.BlockSpec((None, None, 1, t_k), lambda b, hh, i: (b, hh, 0, 0))],
        out_specs=qo, out_shape=jax.ShapeDtypeStruct((bsz, h, t_q, hd), BF16),
        scratch_shapes=[pltpu.VMEM((tq, 1), F32), pltpu.VMEM((tq, 1), F32), pltpu.VMEM((tq, hd), F32)],
        compiler_params=_cparams(("parallel", "parallel", "arbitrary")), name="fox_attn")(q, k, v, f_t)


def _ssd_kernel(xbc_ref, z_ref, dtp_ref, dtt_ref, h0_ref, c0_ref, cw_ref, cb_ref, dtb_row_ref, dtb_col_ref,
                alog_row_ref, alog_col_ref, dvec_ref, ng_ref, tril_ref, triu_ref,
                yc_ref, hfin_ref, cnew_ref, h_sc, pad_sc, y_sc, *, L):
    c = pl.program_id(1)

    @pl.when(c == 0)
    def _():
        h_sc[...] = h0_ref[...]
        pad_sc[CONV_PAD - (CONV_W - 1):CONV_PAD, :] = c0_ref[...]

    xin = xbc_ref[...]
    pad_sc[CONV_PAD:CONV_PAD + L, :] = xin
    y = cb_ref[...]
    for j in range(CONV_W - 1):
        r0 = CONV_PAD - (CONV_W - 1) + j
        y = y + pad_sc[r0:r0 + L, :] * cw_ref[j:j + 1, :]
    y = y + xin * cw_ref[CONV_W - 1:CONV_W, :]
    tail = pad_sc[CONV_PAD + L - (CONV_W - 1):CONV_PAD + L, :]
    pad_sc[CONV_PAD - (CONV_W - 1):CONV_PAD, :] = tail
    cnew_ref[...] = tail
    act = y * _sigmoid(y)

    dt_c = _softplus(dtp_ref[...] + dtb_row_ref[...])
    a_c = dt_c * (-jnp.exp(alog_row_ref[...]))
    tril = tril_ref[...]
    hi, mid, lo = _split3(a_c)
    acum_c = (_dot(tril, hi) + _dot(tril, mid)) + _dot(tril, lo)
    dt_r = _softplus(dtt_ref[...] + dtb_col_ref[...])
    a_r = dt_r * (-jnp.exp(alog_col_ref[...]))
    triu = triu_ref[...]
    hi, mid, lo = _split3(a_r)
    acum_r = (_dot(hi, triu) + _dot(mid, triu)) + _dot(lo, triu)

    row = lax.broadcasted_iota(jnp.int32, (L, L), 0)
    col = lax.broadcasted_iota(jnp.int32, (L, L), 1)
    causal = col <= row
    rep = SSM_HEADS // SSM_GROUPS
    for g in range(SSM_GROUPS):
        b0 = D_C + g * SSM_N
        c0 = D_C + SSM_GROUPS * SSM_N + g * SSM_N
        b_g = act[:, b0:b0 + SSM_N]
        c_g = act[:, c0:c0 + SSM_N]
        b_gb = b_g.astype(BF16)
        cb = _dot_nt(c_g.astype(BF16), b_gb)
        for hh in range(rep):
            h = g * rep + hh
            acol = acum_c[:, h:h + 1]
            arow = acum_r[h:h + 1, :]
            alast = acum_c[L - 1:L, h:h + 1]
            decay = jnp.exp(jnp.where(causal, acol - arow, -jnp.inf))
            m = (cb * decay) * dt_r[h:h + 1, :]
            x_h = act[:, h * SSM_HD:(h + 1) * SSM_HD]
            y_diag = _dot(m.astype(BF16), x_h.astype(BF16))
            h_prev = h_sc[h]
            y_off = _dot_nt((c_g * jnp.exp(acol)).astype(BF16), h_prev.astype(BF16))
            wcol = jnp.exp(alast - acol) * dt_c[:, h:h + 1]
            s_h = _dot_tn((x_h * wcol).astype(BF16), b_gb)
            h_sc[h] = jnp.exp(alast) * h_prev + s_h
            y_sc[:, h * SSM_HD:(h + 1) * SSM_HD] = (y_diag + y_off) + dvec_ref[:, h * SSM_HD:(h + 1) * SSM_HD] * x_h

    @pl.when(c == pl.num_programs(1) - 1)
    def _():
        hfin_ref[...] = h_sc[...]

    z = z_ref[...]
    yg = y_sc[...] * (z * _sigmoid(z))
    ms = jnp.mean(yg * yg, axis=-1, keepdims=True)
    yc_ref[...] = ((yg * lax.rsqrt(ms + RMS_EPS)) * ng_ref[...]).astype(BF16)


def _ssd_call(proj, small, dt_t, h0, conv0, lw, bsz, t):
    L = min(SSD_L, t)
    nc = t // L
    zblk = OFF_Z // D_C

    def const(shape):
        return pl.BlockSpec(shape, lambda b, c: (0,) * len(shape))

    in_specs = [
        pl.BlockSpec((L, CONV_DIM), lambda b, c: (b * nc + c, OFF_XBC // CONV_DIM)),
        pl.BlockSpec((L, D_C), lambda b, c: (b * nc + c, zblk)),
        pl.BlockSpec((L, LANES), lambda b, c: (b * nc + c, 0)),
        pl.BlockSpec((None, SSM_HEADS, L), lambda b, c: (b, 0, c)),
        pl.BlockSpec((None, SSM_HEADS, SSM_HD, SSM_N), lambda b, c: (b, 0, 0, 0)),
        pl.BlockSpec((None, CONV_W - 1, CONV_DIM), lambda b, c: (b, 0, 0)),
        const((CONV_W, CONV_DIM)), const((1, CONV_DIM)), const((1, LANES)), const((SSM_HEADS, 1)),
        const((1, LANES)), const((SSM_HEADS, 1)), const((1, D_C)), const((1, D_C)), const((L, L)), const((L, L)),
    ]
    out_specs = [
        pl.BlockSpec((L, D_C), lambda b, c: (b * nc + c, 0)),
        pl.BlockSpec((None, SSM_HEADS, SSM_HD, SSM_N), lambda b, c: (b, 0, 0, 0)),
        pl.BlockSpec((None, CONV_W - 1, CONV_DIM), lambda b, c: (b, 0, 0)),
    ]
    out_shape = [jax.ShapeDtypeStruct((bsz * t, D_C), BF16),
                 jax.ShapeDtypeStruct((bsz, SSM_HEADS, SSM_HD, SSM_N), F32),
                 jax.ShapeDtypeStruct((bsz, CONV_W - 1, CONV_DIM), F32)]
    ones = jnp.ones((L, L), F32)
    return pl.pallas_call(
        functools.partial(_ssd_kernel, L=L), grid=(bsz, nc), in_specs=in_specs, out_specs=out_specs,
        out_shape=out_shape,
        scratch_shapes=[pltpu.VMEM((SSM_HEADS, SSM_HD, SSM_N), F32), pltpu.VMEM((CONV_PAD + L, CONV_DIM), F32),
                        pltpu.VMEM((L, D_C), F32)],
        compiler_params=_cparams(("parallel", "arbitrary")), name="ssd")(
            proj, proj, small, dt_t, h0, conv0, lw["conv_w"], lw["conv_b"], lw["dtb_row"], lw["dtb_col"],
            lw["alog_row"], lw["alog_col"], lw["dvec"], lw["norm_g"],
            jnp.tril(ones).astype(BF16), jnp.triu(ones).astype(BF16))


def _merge_kernel(ya_ref, yb_ref, yc_ref, g0_ref, g1_ref, g2_ref, x_ref, wa_ref, wb_ref, wc_ref, wo_ref,
                  lg_ref, lb_ref, o_ref, ob_ref):
    merged = _sigmoid(g0_ref[...]) * _dot(ya_ref[...], wa_ref[...])
    merged = merged + _sigmoid(g1_ref[...]) * _dot(yb_ref[...], wb_ref[...])
    merged = merged + _sigmoid(g2_ref[...]) * _dot(yc_ref[...], wc_ref[...])
    mix = _dot(merged.astype(BF16), wo_ref[...])
    y = _ln_rows(ALPHA * x_ref[...] + mix, lg_ref[...], lb_ref[...])
    o_ref[...] = y
    ob_ref[...] = y.astype(BF16)


def _merge_call(ya, yb, yc, proj, x, lw, tm):
    n = x.shape[0]
    gblk = OFF_GATE // D_MODEL

    def rows(w):
        return pl.BlockSpec((tm, w), lambda i: (i, 0))

    def const(shape):
        return pl.BlockSpec(shape, lambda i: (0, 0))

    gates = [pl.BlockSpec((tm, D_MODEL), lambda i, j=j: (i, gblk + j)) for j in range(N_BRANCH)]
    return pl.pallas_call(
        _merge_kernel, grid=(n // tm,),
        in_specs=[rows(D_AV), rows(D_B), rows(D_C)] + gates + [rows(D_MODEL),
                  const((D_AV, D_MODEL)), const((D_B, D_MODEL)), const((D_C, D_MODEL)), const((D_MODEL, D_MODEL)),
                  const((1, D_MODEL)), const((1, D_MODEL))],
        out_specs=[rows(D_MODEL), rows(D_MODEL)],
        out_shape=[jax.ShapeDtypeStruct((n, D_MODEL), F32), jax.ShapeDtypeStruct((n, D_MODEL), BF16)],
        compiler_params=_cparams(("parallel",)), name="merge")(
            ya, yb, yc, proj, proj, proj, x, lw["w_a"], lw["w_b"], lw["w_c"], lw["w_o"], lw["ln1_g"], lw["ln1_b"])


def _router_kernel(h_ref, whi_ref, wlo_ref, br_ref, o_ref):
    h = h_ref[...]
    hi = h.astype(BF16)
    lo = (h - hi.astype(F32)).astype(BF16)
    whi = whi_ref[...]
    logits = (_dot(hi, whi) + _dot(lo, whi)) + _dot(hi, wlo_ref[...]) + br_ref[...]
    lane = lax.broadcasted_iota(jnp.int32, logits.shape, 1)
    lg = jnp.where(lane < N_EXPERTS, logits, -jnp.inf)
    m1 = jnp.max(lg, axis=-1, keepdims=True)
    i1 = jnp.min(jnp.where(lg == m1, lane, LANES), axis=-1, keepdims=True)
    lg2 = jnp.where(lane == i1, -jnp.inf, lg)
    m2 = jnp.max(lg2, axis=-1, keepdims=True)
    i2 = jnp.min(jnp.where(lg2 == m2, lane, LANES), axis=-1, keepdims=True)
    e = jnp.exp(m2 - m1)
    den = 1.0 + e
    o_ref[...] = jnp.where(lane == i1, 1.0 / den, 0.0) + jnp.where(lane == i2, e / den, 0.0)


def _router_call(h, lw, tm):
    n = h.shape[0]
    return pl.pallas_call(
        _router_kernel, grid=(n // tm,),
        in_specs=[pl.BlockSpec((tm, D_MODEL), lambda i: (i, 0)), pl.BlockSpec((D_MODEL, LANES), lambda i: (0, 0)),
                  pl.BlockSpec((D_MODEL, LANES), lambda i: (0, 0)), pl.BlockSpec((1, LANES), lambda i: (0, 0))],
        out_specs=pl.BlockSpec((tm, LANES), lambda i: (i, 0)),
        out_shape=jax.ShapeDtypeStruct((n, LANES), F32),
        compiler_params=_cparams(("parallel",)), name="router")(h, lw["wr_hi"], lw["wr_lo"], lw["br"])


def _ffn_kernel(hb_ref, h_ref, p_ref, comb_ref, wg_ref, wu_ref, wd_ref, wp_ref, wpg_ref, lg_ref, lb_ref,
                o_ref, ob_ref, acc_sc, *, weighted):
    j = pl.program_id(1)
    hb = hb_ref[...]

    @pl.when(j == 0)
    def _():
        ple = _dot(p_ref[...].astype(BF16), wp_ref[...]) * _sigmoid(_dot(hb, wpg_ref[...]))
        acc_sc[...] = ALPHA * h_ref[...] + ple

    gate = _dot(hb, wg_ref[...])
    up = _dot(hb, wu_ref[...])
    out = _dot(((gate * _sigmoid(gate)) * up).astype(BF16), wd_ref[...])
    if weighted:
        comb = comb_ref[...]
        lane = lax.broadcasted_iota(jnp.int32, comb.shape, 1)
        out = jnp.sum(jnp.where(lane == j, comb, 0.0), axis=-1, keepdims=True) * out
    acc_sc[...] += out

    @pl.when(j == pl.num_programs(1) - 1)
    def _():
        y = _ln_rows(acc_sc[...], lg_ref[...], lb_ref[...])
        o_ref[...] = y
        ob_ref[...] = y.astype(BF16)


def _ffn_call(hb, h, p, comb, lw, tm):
    n = h.shape[0]
    weighted = comb is not None
    if weighted:
        n_e = N_EXPERTS
        wg_spec = pl.BlockSpec((None, D_MODEL, D_FF_EXPERT), lambda i, j: (j, 0, 0))
        wd_spec = pl.BlockSpec((None, D_FF_EXPERT, D_MODEL), lambda i, j: (j, 0, 0))
    else:
        n_e = D_FF // D_FF_EXPERT
        comb = jnp.zeros((n, LANES), F32)
        wg_spec = pl.BlockSpec((D_MODEL, D_FF_EXPERT), lambda i, j: (0, j))
        wd_spec = pl.BlockSpec((D_FF_EXPERT, D_MODEL), lambda i, j: (j, 0))

    def rows(w):
        return pl.BlockSpec((tm, w), lambda i, j: (i, 0))

    def const(shape):
        return pl.BlockSpec(shape, lambda i, j: (0, 0))

    return pl.pallas_call(
        functools.partial(_ffn_kernel, weighted=weighted), grid=(n // tm, n_e),
        in_specs=[rows(D_MODEL), rows(D_MODEL), rows(PLE_DIM), rows(LANES), wg_spec, wg_spec, wd_spec,
                  const((PLE_DIM, D_MODEL)), const((D_MODEL, D_MODEL)), const((1, D_MODEL)), const((1, D_MODEL))],
        out_specs=[rows(D_MODEL), rows(D_MODEL)],
        out_shape=[jax.ShapeDtypeStruct((n, D_MODEL), F32), jax.ShapeDtypeStruct((n, D_MODEL), BF16)],
        scratch_shapes=[pltpu.VMEM((tm, D_MODEL), F32)],
        compiler_params=_cparams(("parallel", "arbitrary")), name="ffn")(
            hb, h, p, comb, lw["w_gate"], lw["w_up"], lw["w_down"], lw["w_ple"], lw["w_pleg"], lw["ln2_g"], lw["ln2_b"])


def _layer_weights(i, W):
    w_in = W["w_in"][i]
    sizes = [D_AV, D_AV, D_B, D_B, D_B, FOX_HEADS, D_C, CONV_DIM, SSM_HEADS, N_BRANCH * D_MODEL]
    o = [0] + [int(s) for s in np.cumsum(sizes)]
    seg = [w_in[:, o[k]:o[k + 1]] for k in range(len(sizes))]
    w_u, w_v, w_q, w_k, w_va, w_f, w_z, w_xbc, w_dt, w_gate = seg
    w_main = jnp.concatenate([w_xbc, w_u, w_v, w_q, w_k, w_va, w_z, w_gate], axis=1).astype(BF16)
    pad = jnp.zeros((D_MODEL, LANES - SSM_HEADS - FOX_HEADS), F32)
    w_small = jnp.concatenate([w_dt, w_f, pad], axis=1).astype(BF16)

    def lane_row(v):
        return jnp.pad(v.astype(F32), (0, LANES - v.shape[0])).reshape(1, LANES)

    lw = dict(
        w_main=w_main, w_small=w_small, fox_b=W["fox_b_forget"][i],
        gmlp_g=W["gmlp_ln_g"][i], gmlp_b=W["gmlp_ln_b"][i], gmlp_w=W["gmlp_w_spatial"][i], gmlp_bs=W["gmlp_b_spatial"][i],
        conv_w=W["ssm_conv_w"][i], conv_b=W["ssm_conv_b"][i].reshape(1, CONV_DIM),
        dtb_row=lane_row(W["ssm_dt_bias"][i]), dtb_col=W["ssm_dt_bias"][i].reshape(SSM_HEADS, 1),
        alog_row=lane_row(W["ssm_a_log"][i]), alog_col=W["ssm_a_log"][i].reshape(SSM_HEADS, 1),
        dvec=jnp.repeat(W["ssm_d"][i], SSM_HD).reshape(1, D_C), norm_g=W["ssm_norm_g"][i].reshape(1, D_C),
        w_a=W["w_branch_a"][i].astype(BF16), w_b=W["w_branch_b"][i].astype(BF16), w_c=W["w_branch_c"][i].astype(BF16),
        w_o=W["w_out"][i].astype(BF16), ln1_g=W["ln1_g"][i].reshape(1, D_MODEL), ln1_b=W["ln1_b"][i].reshape(1, D_MODEL),
        ln2_g=W["ln2_g"][i].reshape(1, D_MODEL), ln2_b=W["ln2_b"][i].reshape(1, D_MODEL),
        w_ple=W["ple_w_proj"][i].astype(BF16), w_pleg=W["ple_w_gate"][i].astype(BF16),
    )
    j = i // 2
    if i % 2 == 0:
        lw.update(w_gate=W["ffn_w_gate"][j].astype(BF16), w_up=W["ffn_w_up"][j].astype(BF16),
                  w_down=W["ffn_w_down"][j].astype(BF16))
    else:
        wr = jnp.pad(W["moe_w_router"][j], ((0, 0), (0, LANES - N_EXPERTS)))
        wr_hi = wr.astype(BF16)
        lw.update(w_gate=W["moe_w_gate"][j].astype(BF16), w_up=W["moe_w_up"][j].astype(BF16),
                  w_down=W["moe_w_down"][j].astype(BF16), wr_hi=wr_hi,
                  wr_lo=(wr - wr_hi.astype(F32)).astype(BF16), br=lane_row(W["moe_b_router"][j]))
    return lw


def _trunk_layer(x, xb, p, i, lw, bsz, t, fox_cache, ssm0, conv0, tm):
    n = bsz * t
    proj = _matmul_call(xb, lw["w_main"], tm, 1024, "in_proj")
    small = _matmul_call(xb, lw["w_small"], tm, LANES, "in_proj_small")

    w_eff, bias_eff = _gmlp_weights(lw["gmlp_w"], lw["gmlp_bs"], t)
    y_a, v_n = _gmlp_call(proj, lw["gmlp_g"], lw["gmlp_b"], w_eff, bias_eff, min(n, 512))

    k_new = proj[:, OFF_K:OFF_K + D_B].reshape(bsz, t, FOX_HEADS, FOX_HD)
    v_new = proj[:, OFF_VA:OFF_VA + D_B].reshape(bsz, t, FOX_HEADS, FOX_HD)
    q_h = jnp.transpose(proj[:, OFF_Q:OFF_Q + D_B].reshape(bsz, t, FOX_HEADS, FOX_HD), (0, 2, 1, 3))
    a_f = jnp.transpose(small[:, SSM_HEADS:SSM_HEADS + FOX_HEADS].reshape(bsz, t, FOX_HEADS), (0, 2, 1))
    logf_t = _logsig_call(a_f, lw["fox_b"])
    if fox_cache is None:
        k_all, v_all, logf_all, q_off = k_new, v_new, logf_t, 0
    else:
        k_c, v_c, logf_c = fox_cache
        k_all = jnp.concatenate([k_c, k_new], axis=1)
        v_all = jnp.concatenate([v_c, v_new], axis=1)
        logf_all = jnp.concatenate([jnp.transpose(logf_c, (0, 2, 1)), logf_t], axis=2)
        q_off = k_c.shape[1]
    t_k = k_all.shape[1]
    tc = 512 if t_k % 512 == 0 else t_k
    f_t = _cumsum_call(logf_all, tc).reshape(bsz, FOX_HEADS, 1, t_k)
    k_h = jnp.transpose(k_all, (0, 2, 1, 3)).astype(BF16)
    v_h = jnp.transpose(v_all, (0, 2, 1, 3)).astype(BF16)
    tq, tk = (512, 512) if t % 512 == 0 else (t, t_k)
    y_b = _fox_call(q_h, k_h, v_h, f_t, tq, tk, q_off)
    y_b = jnp.transpose(y_b, (0, 2, 1, 3)).reshape(n, D_B)

    dt_t = jnp.transpose(small[:, :SSM_HEADS].reshape(bsz, t, SSM_HEADS), (0, 2, 1))
    y_c, h_fin, conv_new = _ssd_call(proj, small, dt_t, ssm0, conv0, lw, bsz, t)

    x1, x1b = _merge_call(y_a, y_b, y_c, proj, x, lw, tm)
    comb = _router_call(x1, lw, tm) if i % 2 == 1 else None
    x2, x2b = _ffn_call(x1b, x1, p, comb, lw, tm)
    logf = jnp.transpose(logf_t, (0, 2, 1))
    return x2, x2b, (k_new, v_new, logf, h_fin, conv_new, v_n.reshape(bsz, t, D_AV))


def kernel(x_prompt, x_sample, p_prompt, p_sample, cache_fox_k, cache_fox_v, cache_fox_logf, state_ssm, state_conv, ln0_g, ln0_b, w_in, fox_b_forget, gmlp_ln_g, gmlp_ln_b, gmlp_w_spatial, gmlp_b_spatial, ssm_conv_w, ssm_conv_b, ssm_dt_bias, ssm_a_log, ssm_d, ssm_norm_g, w_branch_a, w_branch_b, w_branch_c, w_out, ln1_g, ln1_b, ln2_g, ln2_b, ffn_w_gate, ffn_w_up, ffn_w_down, moe_w_router, moe_b_router, moe_w_gate, moe_w_up, moe_w_down, ple_w_proj, ple_w_gate):
    W = dict(w_in=w_in, fox_b_forget=fox_b_forget, gmlp_ln_g=gmlp_ln_g, gmlp_ln_b=gmlp_ln_b,
             gmlp_w_spatial=gmlp_w_spatial, gmlp_b_spatial=gmlp_b_spatial, ssm_conv_w=ssm_conv_w,
             ssm_conv_b=ssm_conv_b, ssm_dt_bias=ssm_dt_bias, ssm_a_log=ssm_a_log, ssm_d=ssm_d,
             ssm_norm_g=ssm_norm_g, w_branch_a=w_branch_a, w_branch_b=w_branch_b,
             w_branch_c=w_branch_c, w_out=w_out, ln1_g=ln1_g, ln1_b=ln1_b, ln2_g=ln2_g, ln2_b=ln2_b,
             ffn_w_gate=ffn_w_gate, ffn_w_up=ffn_w_up, ffn_w_down=ffn_w_down,
             moe_w_router=moe_w_router, moe_b_router=moe_b_router, moe_w_gate=moe_w_gate, moe_w_up=moe_w_up,
             moe_w_down=moe_w_down, ple_w_proj=ple_w_proj, ple_w_gate=ple_w_gate)
    bp, tp, _ = x_prompt.shape
    bs, ts, _ = x_sample.shape
    tm_p = 512 if (bp * tp) % 512 == 0 else bp * tp
    tm_s = bs * ts
    xp, xpb = _layer_norm_call(x_prompt.reshape(bp * tp, D_MODEL), ln0_g, ln0_b, tm_p)
    xs, xsb = _layer_norm_call(x_sample.reshape(bs * ts, D_MODEL), ln0_g, ln0_b, tm_s)
    outs_p, outs_s = [], []
    for i in range(DEPTH):
        lw = _layer_weights(i, W)
        ssm0 = jnp.zeros((bp, SSM_HEADS, SSM_HD, SSM_N), F32)
        conv0 = jnp.zeros((bp, CONV_W - 1, CONV_DIM), F32)
        xp, xpb, st_p = _trunk_layer(xp, xpb, p_prompt[i].reshape(bp * tp, PLE_DIM), i, lw, bp, tp,
                                     None, ssm0, conv0, tm_p)
        xs, xsb, st_s = _trunk_layer(xs, xsb, p_sample[i].reshape(bs * ts, PLE_DIM), i, lw, bs, ts,
                                     (cache_fox_k[i], cache_fox_v[i], cache_fox_logf[i]),
                                     state_ssm[i], state_conv[i], tm_s)
        outs_p.append(st_p)
        outs_s.append(st_s)

    def stack(outs, k):
        return jnp.stack([o[k] for o in outs])

    return (xp.reshape(bp, tp, D_MODEL), xs.reshape(bs, ts, D_MODEL),
            stack(outs_p, 0), stack(outs_p, 1), stack(outs_p, 2), stack(outs_p, 3), stack(outs_p, 4),
            stack(outs_s, 0), stack(outs_s, 1), stack(outs_s, 2), stack(outs_s, 3), stack(outs_s, 4), stack(outs_s, 5))
```

```python
import functools
import math

import numpy as np
import jax
import jax.numpy as jnp
from jax import lax
from jax.experimental import pallas as pl
from jax.experimental.pallas import tpu as pltpu

F32 = jnp.float32
BF16 = jnp.bfloat16

D_MODEL = 1024
DEPTH = 2
CHUNK = 64
PLE_DIM = 256
GMLP_CHUNK = 128
GMLP_GROUPS = 4
D_AV = D_MODEL // 2
GMLP_GROUP_W = D_AV // GMLP_GROUPS
FOX_HD = 64
D_B = D_MODEL // 2
FOX_HEADS = D_B // FOX_HD
D_C = D_MODEL
SSM_HD = 64
SSM_HEADS = D_C // SSM_HD
SSM_N = 128
SSM_GROUPS = 2
CONV_W = 4
CONV_DIM = D_C + 2 * SSM_GROUPS * SSM_N
N_BRANCH = 3
D_FF = 11 * D_MODEL // 4
N_EXPERTS = 8
D_FF_EXPERT = D_FF // 2
ALPHA = (2.0 * DEPTH) ** 0.25
LN_EPS = 1e-5
RMS_EPS = 1e-5
NEG_INF = -1e30
LOG2E = math.log2(math.e)

LANES = 128
VMEM_LIMIT = 56 * 2**20
N_MAIN = CONV_DIM + 2 * D_AV + 3 * D_B + D_C + N_BRANCH * D_MODEL
OFF_XBC, OFF_U, OFF_V, OFF_Q, OFF_K, OFF_VA, OFF_Z, OFF_GATE = 0, 1536, 2048, 2560, 3072, 3584, 4096, 5120
SSD_L = 128
CONV_PAD = 8
FOX_NBIAS = 3
FOX_DK = 128
FOX_DV = 80
FOX_QSUB = 256


def _cparams(sem):
    return pltpu.CompilerParams(dimension_semantics=sem, vmem_limit_bytes=VMEM_LIMIT)


def _sigmoid(x):
    return 1.0 / (1.0 + jnp.exp(-x))


def _softplus(x):
    return jnp.maximum(x, 0.0) + jnp.log1p(jnp.exp(-jnp.abs(x)))


def _gelu(x):
    c = np.float32(np.sqrt(2.0 / np.pi))
    return x * (0.5 * (1.0 + jnp.tanh(c * (x + 0.044715 * (x * x * x)))))


def _ln_rows(x, g, b):
    mu = jnp.mean(x, axis=-1, keepdims=True)
    xc = x - mu
    var = jnp.mean(xc * xc, axis=-1, keepdims=True)
    return xc * lax.rsqrt(var + LN_EPS) * g + b


def _bf16_part(x):
    u = lax.bitcast_convert_type(x, jnp.uint32) & jnp.uint32(0xFFFF0000)
    return lax.bitcast_convert_type(u, F32)


def _split2(x):
    hi = _bf16_part(x)
    return hi.astype(BF16), (x - hi).astype(BF16)


def _split3(x):
    hi = _bf16_part(x)
    r1 = x - hi
    mid = _bf16_part(r1)
    return hi.astype(BF16), mid.astype(BF16), (r1 - mid).astype(BF16)


def _dot(a, b):
    return jnp.dot(a, b, preferred_element_type=F32)


def _dot_nt(a, b):
    return lax.dot_general(a, b, (((1,), (1,)), ((), ())), preferred_element_type=F32)


def _dot_tn(a, b):
    return lax.dot_general(a, b, (((0,), (0,)), ((), ())), preferred_element_type=F32)


_DOTS = {"nn": _dot, "nt": _dot_nt, "tn": _dot_tn}


def _mm(a, b, precise, dims="nn"):
    dot = _DOTS[dims]
    if not precise:
        a = a[0] if isinstance(a, tuple) else a.astype(BF16)
        b = b[0] if isinstance(b, tuple) else b.astype(BF16)
        return dot(a, b)
    ah, al = a if isinstance(a, tuple) else _split2(a)
    bh, bl = b if isinstance(b, tuple) else _split2(b)
    return (dot(ah, bh) + dot(al, bh)) + dot(ah, bl)


def _take_w(it, precise):
    hi = next(it)
    return hi, (next(it) if precise else None)


def _ld(w, idx=...):
    return w[0][idx], (None if w[1] is None else w[1][idx])


def _wargs(lw, name, precise):
    return [lw[name], lw[name + "_lo"]] if precise else [lw[name]]


def _act_dtype(precise):
    return F32 if precise else BF16


def _ln_kernel(x_ref, g_ref, b_ref, o_ref, ob_ref):
    y = _ln_rows(x_ref[...], g_ref[...], b_ref[...])
    o_ref[...] = y
    ob_ref[...] = y.astype(BF16)


def _layer_norm_call(x, g, b, tm):
    n, d = x.shape
    row = pl.BlockSpec((tm, d), lambda i: (i, 0))
    vec = pl.BlockSpec((1, d), lambda i: (0, 0))
    return pl.pallas_call(
        _ln_kernel, grid=(n // tm,), in_specs=[row, vec, vec], out_specs=[row, row],
        out_shape=[jax.ShapeDtypeStruct((n, d), F32), jax.ShapeDtypeStruct((n, d), BF16)],
        compiler_params=_cparams(("parallel",)), name="ln_in")(x, g.reshape(1, d), b.reshape(1, d))


def _mm_kernel(*refs, precise):
    it = iter(refs)
    x_ref = next(it)
    w = _take_w(it, precise)
    o_ref = next(it)
    o_ref[...] = _mm(x_ref[...], _ld(w), precise)


def _matmul_call(x, lw, wname, tm, tn, precise, name):
    m, k = x.shape
    ws = _wargs(lw, wname, precise)
    n = ws[0].shape[1]
    wspec = pl.BlockSpec((k, tn), lambda i, j: (0, j))
    return pl.pallas_call(
        functools.partial(_mm_kernel, precise=precise), grid=(m // tm, n // tn),
        in_specs=[pl.BlockSpec((tm, k), lambda i, j: (i, 0))] + [wspec] * len(ws),
        out_specs=pl.BlockSpec((tm, tn), lambda i, j: (i, j)),
        out_shape=jax.ShapeDtypeStruct((m, n), F32),
        compiler_params=_cparams(("parallel", "parallel")), name=name)(x, *ws)


def _gmlp_kernel(*refs, rows, precise):
    it = iter(refs)
    u_ref, v_ref, g_ref, b_ref = next(it), next(it), next(it), next(it)
    w = _take_w(it, precise)
    bias_ref, ya_ref, vn_ref = next(it), next(it), next(it)
    u = _gelu(u_ref[...])
    vn = _ln_rows(_gelu(v_ref[...]), g_ref[...], b_ref[...])
    vn_ref[...] = vn
    vb = vn if precise else vn.astype(BF16)
    for c in range(rows // GMLP_CHUNK):
        r0 = c * GMLP_CHUNK
        for g in range(GMLP_GROUPS):
            c0 = g * GMLP_GROUP_W
            mixed = _mm(_ld(w, g), vb[r0:r0 + GMLP_CHUNK, c0:c0 + GMLP_GROUP_W], precise)
            mixed = mixed + bias_ref[:, c0:c0 + GMLP_GROUP_W]
            ya_ref[r0:r0 + GMLP_CHUNK, c0:c0 + GMLP_GROUP_W] = (
                u[r0:r0 + GMLP_CHUNK, c0:c0 + GMLP_GROUP_W] * mixed).astype(ya_ref.dtype)


def _gmlp_call(proj, ln_g, ln_b, w_eff, bias_eff, rows, precise):
    n = proj.shape[0]
    ublk, vblk = OFF_U // D_AV, OFF_V // D_AV
    vec = pl.BlockSpec((1, D_AV), lambda i: (0, 0))
    ws = list(_split2(w_eff)) if precise else [w_eff.astype(BF16)]
    wspec = pl.BlockSpec((GMLP_GROUPS, GMLP_CHUNK, GMLP_CHUNK), lambda i: (0, 0, 0))
    return pl.pallas_call(
        functools.partial(_gmlp_kernel, rows=rows, precise=precise), grid=(n // rows,),
        in_specs=[pl.BlockSpec((rows, D_AV), lambda i: (i, ublk)), pl.BlockSpec((rows, D_AV), lambda i: (i, vblk)),
                  vec, vec] + [wspec] * len(ws) + [pl.BlockSpec((GMLP_CHUNK, D_AV), lambda i: (0, 0))],
        out_specs=[pl.BlockSpec((rows, D_AV), lambda i: (i, 0)), pl.BlockSpec((rows, D_AV), lambda i: (i, 0))],
        out_shape=[jax.ShapeDtypeStruct((n, D_AV), _act_dtype(precise)), jax.ShapeDtypeStruct((n, D_AV), F32)],
        compiler_params=_cparams(("parallel",)), name="gmlp")(
            proj, proj, ln_g.reshape(1, D_AV), ln_b.reshape(1, D_AV), *ws, bias_eff)


def _gmlp_weights(w_s, b_s, t):
    l = min(GMLP_CHUNK, t)
    pos = np.arange(l)
    mask = (pos[None, :] // CHUNK) <= (pos[:, None] // CHUNK)
    w = jnp.where(mask[None], w_s[:, :l, :l], 0.0)
    bias = jnp.transpose(b_s[:, :l])
    reps = GMLP_CHUNK // l
    if reps > 1:
        eye = jnp.eye(reps, dtype=w.dtype)
        w = jnp.einsum("ab,gij->gaibj", eye, w).reshape(GMLP_GROUPS, GMLP_CHUNK, GMLP_CHUNK)
        bias = jnp.tile(bias, (reps, 1))
    return w, jnp.repeat(bias, GMLP_GROUP_W, axis=1)


def _logsig_kernel(a_ref, b_ref, o_ref):
    o_ref[...] = -_softplus(-(a_ref[...] + b_ref[...]))


def _logsig_call(a_t, bias):
    bsz, h, t = a_t.shape
    blk = pl.BlockSpec((None, h, t), lambda b: (b, 0, 0))
    return pl.pallas_call(
        _logsig_kernel, grid=(bsz,), in_specs=[blk, pl.BlockSpec((h, 1), lambda b: (0, 0))], out_specs=blk,
        out_shape=jax.ShapeDtypeStruct((bsz, h, t), F32),
        compiler_params=_cparams(("parallel",)), name="fox_logf")(a_t, bias.reshape(h, 1))


def _fbias_kernel(x_ref, tri_ref, o_ref, carry_ref):
    @pl.when(pl.program_id(1) == 0)
    def _():
        carry_ref[...] = jnp.zeros_like(carry_ref)

    tri = tri_ref[...]
    hi, mid, lo = _split3(x_ref[...])
    f = (_dot(hi, tri) + _dot(mid, tri)) + _dot(lo, tri) + carry_ref[:, 0:1]
    tc = f.shape[1]
    carry_ref[...] = jnp.broadcast_to(f[:, tc - 1:tc], carry_ref.shape)
    g = f * (-LOG2E)
    g1 = _bf16_part(g)
    r = g - g1
    g2 = _bf16_part(r)
    o_ref[0] = g1
    o_ref[1] = g2
    o_ref[2] = r - g2


def _fbias_call(x_t, tc):
    bsz, h, t = x_t.shape
    tri = jnp.triu(jnp.ones((tc, tc), F32)).astype(BF16)
    return pl.pallas_call(
        _fbias_kernel, grid=(bsz, t // tc),
        in_specs=[pl.BlockSpec((None, h, tc), lambda b, j: (b, 0, j)), pl.BlockSpec((tc, tc), lambda b, j: (0, 0))],
        out_specs=pl.BlockSpec((None, FOX_NBIAS, h, tc), lambda b, j: (b, 0, 0, j)),
        out_shape=jax.ShapeDtypeStruct((bsz, FOX_NBIAS, h, t), F32),
        scratch_shapes=[pltpu.VMEM((h, LANES), F32)],
        compiler_params=_cparams(("parallel", "arbitrary")), name="fox_bias")(x_t, tri)


def _fox_kernel(qt_ref, ka_ref, vt_ref, o_ref, qa_sc, m_sc, acc_sc, s_sc, mb_sc, *, tq, tk, qs, q_off, precise):
    i = pl.program_id(2)
    rows = lax.broadcasted_iota(jnp.int32, (FOX_DK - FOX_HD, tq), 0)
    ones_rows = jnp.where(rows < FOX_NBIAS, 1.0, 0.0)
    qa = jnp.concatenate([qt_ref[...] * (FOX_HD ** -0.5 * LOG2E), ones_rows], axis=0)
    qa_sc[...] = qa.astype(qa_sc.dtype)
    m_sc[...] = jnp.full_like(m_sc, -jnp.inf)
    acc_sc[...] = jnp.zeros_like(acc_sc)
    first_q = q_off + i * tq
    subs = [slice(c * qs, (c + 1) * qs) for c in range(tq // qs)]

    def scores(k0):
        ka = ka_ref[pl.ds(k0, tk), :]
        for sl in subs:
            s = _mm(ka, qa_sc[:, sl], precise)
            s_sc[:, sl] = s
            mb_sc[:, sl] = jnp.max(s, axis=0, keepdims=True)

    def consume(k0, masked):
        vt = vt_ref[:, pl.ds(k0, tk)]
        for c, sl in enumerate(subs):
            s = s_sc[:, sl]
            if masked:
                kpos = k0 + lax.broadcasted_iota(jnp.int32, (tk, qs), 0)
                qpos = first_q + c * qs + lax.broadcasted_iota(jnp.int32, (tk, qs), 1)
                s = jnp.where(kpos <= qpos, s, NEG_INF)
                mb = jnp.max(s, axis=0, keepdims=True)
            else:
                mb = mb_sc[:, sl]
            m_prev = m_sc[:, sl]
            m_new = jnp.maximum(m_prev, mb)
            alpha = jnp.exp2(m_prev - m_new)
            p = jnp.exp2(s - m_new)
            acc_sc[:, sl] = alpha * acc_sc[:, sl] + _mm(vt, p, precise)
            m_sc[:, sl] = m_new

    scores(0)
    if ka_ref.shape[0] == tk:
        consume(0, True)
    else:
        assert tq == tk and q_off == 0

        def body(j, c):
            k0 = pl.multiple_of(j * tk, tk)
            consume(k0, False)
            scores(pl.multiple_of(k0 + tk, tk))
            return c

        lax.fori_loop(0, i, body, 0)
        consume(pl.multiple_of(i * tk, tk), True)
    acc = acc_sc[...]
    o_ref[...] = (acc[:FOX_HD] / acc[FOX_HD:FOX_HD + 1]).astype(o_ref.dtype)


def _fox_call(q_t, k_a, v_t, tq, tk, q_off, precise):
    bsz, h, hd, t_q = q_t.shape
    t_k = k_a.shape[2]
    qs = min(FOX_QSUB, tq)
    act = _act_dtype(precise)
    qo = pl.BlockSpec((None, None, hd, tq), lambda b, hh, i: (b, hh, 0, i))
    return pl.pallas_call(
        functools.partial(_fox_kernel, tq=tq, tk=tk, qs=qs, q_off=q_off, precise=precise), grid=(bsz, h, t_q // tq),
        in_specs=[qo, pl.BlockSpec((None, None, t_k, FOX_DK), lambda b, hh, i: (b, hh, 0, 0)),
                  pl.BlockSpec((None, None, FOX_DV, t_k), lambda b, hh, i: (b, hh, 0, 0))],
        out_specs=qo, out_shape=jax.ShapeDtypeStruct((bsz, h, hd, t_q), act),
        scratch_shapes=[pltpu.VMEM((FOX_DK, tq), act), pltpu.VMEM((1, tq), F32), pltpu.VMEM((FOX_DV, tq), F32),
                        pltpu.VMEM((tk, tq), F32), pltpu.VMEM((1, tq), F32)],
        compiler_params=_cparams(("parallel", "parallel", "arbitrary")), name="fox_attn")(q_t, k_a, v_t)


def _ssd_kernel(xbc_ref, z_ref, dtp_ref, dtt_ref, h0_ref, c0_ref, cw_ref, cb_ref, dtb_row_ref, dtb_col_ref,
                alog_row_ref, alog_col_ref, dvec_ref, ng_ref, tril_ref, triu_ref,
                yc_ref, hfin_ref, cnew_ref, h_sc, pad_sc, y_sc, *, L, precise):
    c = pl.program_id(1)

    @pl.when(c == 0)
    def _():
        h_sc[...] = h0_ref[...]
        pad_sc[CONV_PAD - (CONV_W - 1):CONV_PAD, :] = c0_ref[...]

    xin = xbc_ref[...]
    pad_sc[CONV_PAD:CONV_PAD + L, :] = xin
    y = cb_ref[...]
    for j in range(CONV_W - 1):
        r0 = CONV_PAD - (CONV_W - 1) + j
        y = y + pad_sc[r0:r0 + L, :] * cw_ref[j:j + 1, :]
    y = y + xin * cw_ref[CONV_W - 1:CONV_W, :]
    tail = pad_sc[CONV_PAD + L - (CONV_W - 1):CONV_PAD + L, :]
    pad_sc[CONV_PAD - (CONV_W - 1):CONV_PAD, :] = tail
    cnew_ref[...] = tail
    act = y * _sigmoid(y)

    dt_c = _softplus(dtp_ref[...] + dtb_row_ref[...])
    a_c = dt_c * (-jnp.exp(alog_row_ref[...]))
    tril = tril_ref[...]
    hi, mid, lo = _split3(a_c)
    acum_c = (_dot(tril, hi) + _dot(tril, mid)) + _dot(tril, lo)
    dt_r = _softplus(dtt_ref[...] + dtb_col_ref[...])
    a_r = dt_r * (-jnp.exp(alog_col_ref[...]))
    triu = triu_ref[...]
    hi, mid, lo = _split3(a_r)
    acum_r = (_dot(hi, triu) + _dot(mid, triu)) + _dot(lo, triu)

    row = lax.broadcasted_iota(jnp.int32, (L, L), 0)
    col = lax.broadcasted_iota(jnp.int32, (L, L), 1)
    causal = col <= row
    rep = SSM_HEADS // SSM_GROUPS
    for g in range(SSM_GROUPS):
        b0 = D_C + g * SSM_N
        c0 = D_C + SSM_GROUPS * SSM_N + g * SSM_N
        b_g = act[:, b0:b0 + SSM_N]
        c_g = act[:, c0:c0 + SSM_N]
        b_gs = _split2(b_g) if precise else (b_g.astype(BF16), None)
        cb = _mm(c_g, b_gs, precise, "nt")
        for hh in range(rep):
            h = g * rep + hh
            acol = acum_c[:, h:h + 1]
            arow = acum_r[h:h + 1, :]
            alast = acum_c[L - 1:L, h:h + 1]
            decay = jnp.exp(jnp.where(causal, acol - arow, -jnp.inf))
            m = (cb * decay) * dt_r[h:h + 1, :]
            x_h = act[:, h * SSM_HD:(h + 1) * SSM_HD]
            y_diag = _mm(m, x_h, precise)
            h_prev = h_sc[h]
            y_off = _mm(c_g * jnp.exp(acol), h_prev, precise, "nt")
            wcol = jnp.exp(alast - acol) * dt_c[:, h:h + 1]
            s_h = _mm(x_h * wcol, b_gs, precise, "tn")
            h_sc[h] = jnp.exp(alast) * h_prev + s_h
            y_sc[:, h * SSM_HD:(h + 1) * SSM_HD] = (y_diag + y_off) + dvec_ref[:, h * SSM_HD:(h + 1) * SSM_HD] * x_h

    @pl.when(c == pl.num_programs(1) - 1)
    def _():
        hfin_ref[...] = h_sc[...]

    z = z_ref[...]
    yg = y_sc[...] * (z * _sigmoid(z))
    ms = jnp.mean(yg * yg, axis=-1, keepdims=True)
    yc_ref[...] = ((yg * lax.rsqrt(ms + RMS_EPS)) * ng_ref[...]).astype(yc_ref.dtype)


def _ssd_call(proj, small, dt_t, h0, conv0, lw, bsz, t, precise):
    L = min(SSD_L, t)
    nc = t // L
    zblk = OFF_Z // D_C

    def const(shape):
        return pl.BlockSpec(shape, lambda b, c: (0,) * len(shape))

    in_specs = [
        pl.BlockSpec((L, CONV_DIM), lambda b, c: (b * nc + c, OFF_XBC // CONV_DIM)),
        pl.BlockSpec((L, D_C), lambda b, c: (b * nc + c, zblk)),
        pl.BlockSpec((L, LANES), lambda b, c: (b * nc + c, 0)),
        pl.BlockSpec((None, SSM_HEADS, L), lambda b, c: (b, 0, c)),
        pl.BlockSpec((None, SSM_HEADS, SSM_HD, SSM_N), lambda b, c: (b, 0, 0, 0)),
        pl.BlockSpec((None, CONV_W - 1, CONV_DIM), lambda b, c: (b, 0, 0)),
        const((CONV_W, CONV_DIM)), const((1, CONV_DIM)), const((1, LANES)), const((SSM_HEADS, 1)),
        const((1, LANES)), const((SSM_HEADS, 1)), const((1, D_C)), const((1, D_C)), const((L, L)), const((L, L)),
    ]
    out_specs = [
        pl.BlockSpec((L, D_C), lambda b, c: (b * nc + c, 0)),
        pl.BlockSpec((None, SSM_HEADS, SSM_HD, SSM_N), lambda b, c: (b, 0, 0, 0)),
        pl.BlockSpec((None, CONV_W - 1, CONV_DIM), lambda b, c: (b, 0, 0)),
    ]
    out_shape = [jax.ShapeDtypeStruct((bsz * t, D_C), _act_dtype(precise)),
                 jax.ShapeDtypeStruct((bsz, SSM_HEADS, SSM_HD, SSM_N), F32),
                 jax.ShapeDtypeStruct((bsz, CONV_W - 1, CONV_DIM), F32)]
    ones = jnp.ones((L, L), F32)
    return pl.pallas_call(
        functools.partial(_ssd_kernel, L=L, precise=precise), grid=(bsz, nc), in_specs=in_specs,
        out_specs=out_specs, out_shape=out_shape,
        scratch_shapes=[pltpu.VMEM((SSM_HEADS, SSM_HD, SSM_N), F32), pltpu.VMEM((CONV_PAD + L, CONV_DIM), F32),
                        pltpu.VMEM((L, D_C), F32)],
        compiler_params=_cparams(("parallel", "arbitrary")), name="ssd")(
            proj, proj, small, dt_t, h0, conv0, lw["conv_w"], lw["conv_b"], lw["dtb_row"], lw["dtb_col"],
            lw["alog_row"], lw["alog_col"], lw["dvec"], lw["norm_g"],
            jnp.tril(ones).astype(BF16), jnp.triu(ones).astype(BF16))


def _merge_kernel(*refs, precise):
    it = iter(refs)
    ya_ref, ybt_ref, yc_ref, g0_ref, g1_ref, g2_ref, x_ref = [next(it) for _ in range(7)]
    wa, wb, wc, wo = [_take_w(it, precise) for _ in range(4)]
    lg_ref, lb_ref, o_ref, ob_ref = next(it), next(it), next(it), next(it)
    merged = _sigmoid(g0_ref[...]) * _mm(ya_ref[...], _ld(wa), precise)
    merged = merged + _sigmoid(g1_ref[...]) * _mm(ybt_ref[...], _ld(wb), precise, "tn")
    merged = merged + _sigmoid(g2_ref[...]) * _mm(yc_ref[...], _ld(wc), precise)
    mix = _mm(merged, _ld(wo), precise)
    y = _ln_rows(ALPHA * x_ref[...] + mix, lg_ref[...], lb_ref[...])
    o_ref[...] = y
    ob_ref[...] = y.astype(BF16)


def _merge_call(ya, yb_t, yc, proj, x, lw, tm, bsz, t, precise):
    n = x.shape[0]
    gblk = OFF_GATE // D_MODEL
    per_b = t // tm

    def rows(w):
        return pl.BlockSpec((tm, w), lambda i: (i, 0))

    def const(shape):
        return pl.BlockSpec(shape, lambda i: (0, 0))

    gates = [pl.BlockSpec((tm, D_MODEL), lambda i, j=j: (i, gblk + j)) for j in range(N_BRANCH)]
    wspecs, wargs = [], []
    for name, k in (("w_a", D_AV), ("w_b", D_B), ("w_c", D_C), ("w_o", D_MODEL)):
        ws = _wargs(lw, name, precise)
        wargs += ws
        wspecs += [const((k, D_MODEL))] * len(ws)
    return pl.pallas_call(
        functools.partial(_merge_kernel, precise=precise), grid=(n // tm,),
        in_specs=[rows(D_AV), pl.BlockSpec((None, D_B, tm), lambda i: (i // per_b, 0, i % per_b)), rows(D_C)]
        + gates + [rows(D_MODEL)] + wspecs + [const((1, D_MODEL)), const((1, D_MODEL))],
        out_specs=[rows(D_MODEL), rows(D_MODEL)],
        out_shape=[jax.ShapeDtypeStruct((n, D_MODEL), F32), jax.ShapeDtypeStruct((n, D_MODEL), BF16)],
        compiler_params=_cparams(("parallel",)), name="merge")(
            ya, yb_t, yc, proj, proj, proj, x, *wargs, lw["ln1_g"], lw["ln1_b"])


def _router_kernel(h_ref, whi_ref, wlo_ref, br_ref, o_ref):
    logits = _mm(h_ref[...], (whi_ref[...], wlo_ref[...]), True) + br_ref[...]
    lane = lax.broadcasted_iota(jnp.int32, logits.shape, 1)
    lg = jnp.where(lane < N_EXPERTS, logits, -jnp.inf)
    m1 = jnp.max(lg, axis=-1, keepdims=True)
    i1 = jnp.min(jnp.where(lg == m1, lane, LANES), axis=-1, keepdims=True)
    lg2 = jnp.where(lane == i1, -jnp.inf, lg)
    m2 = jnp.max(lg2, axis=-1, keepdims=True)
    i2 = jnp.min(jnp.where(lg2 == m2, lane, LANES), axis=-1, keepdims=True)
    e = jnp.exp(m2 - m1)
    den = 1.0 + e
    o_ref[...] = jnp.where(lane == i1, 1.0 / den, 0.0) + jnp.where(lane == i2, e / den, 0.0)


def _router_call(h, lw, tm):
    n = h.shape[0]
    return pl.pallas_call(
        _router_kernel, grid=(n // tm,),
        in_specs=[pl.BlockSpec((tm, D_MODEL), lambda i: (i, 0)), pl.BlockSpec((D_MODEL, LANES), lambda i: (0, 0)),
                  pl.BlockSpec((D_MODEL, LANES), lambda i: (0, 0)), pl.BlockSpec((1, LANES), lambda i: (0, 0))],
        out_specs=pl.BlockSpec((tm, LANES), lambda i: (i, 0)),
        out_shape=jax.ShapeDtypeStruct((n, LANES), F32),
        compiler_params=_cparams(("parallel",)), name="router")(h, lw["wr"], lw["wr_lo"], lw["br"])


def _ffn_kernel(*refs, weighted, precise):
    it = iter(refs)
    hb_ref, h_ref, p_ref, comb_ref = next(it), next(it), next(it), next(it)
    wg, wu, wd, wp, wpg = [_take_w(it, precise) for _ in range(5)]
    lg_ref, lb_ref, o_ref, ob_ref, acc_sc = next(it), next(it), next(it), next(it), next(it)
    j = pl.program_id(1)
    hb = h_ref[...] if precise else hb_ref[...]

    @pl.when(j == 0)
    def _():
        ple = _mm(p_ref[...], _ld(wp), precise) * _sigmoid(_mm(hb, _ld(wpg), precise))
        acc_sc[...] = ALPHA * h_ref[...] + ple

    gate = _mm(hb, _ld(wg), precise)
    up = _mm(hb, _ld(wu), precise)
    out = _mm((gate * _sigmoid(gate)) * up, _ld(wd), precise)
    if weighted:
        comb = comb_ref[...]
        lane = lax.broadcasted_iota(jnp.int32, comb.shape, 1)
        out = jnp.sum(jnp.where(lane == j, comb, 0.0), axis=-1, keepdims=True) * out
    acc_sc[...] += out

    @pl.when(j == pl.num_programs(1) - 1)
    def _():
        y = _ln_rows(acc_sc[...], lg_ref[...], lb_ref[...])
        o_ref[...] = y
        ob_ref[...] = y.astype(BF16)


def _ffn_call(hb, h, p, comb, lw, tm, precise):
    n = h.shape[0]
    weighted = comb is not None
    if weighted:
        n_e = N_EXPERTS
        wg_spec = pl.BlockSpec((None, D_MODEL, D_FF_EXPERT), lambda i, j: (j, 0, 0))
        wd_spec = pl.BlockSpec((None, D_FF_EXPERT, D_MODEL), lambda i, j: (j, 0, 0))
    else:
        n_e = D_FF // D_FF_EXPERT
        comb = jnp.zeros((n, LANES), F32)
        wg_spec = pl.BlockSpec((D_MODEL, D_FF_EXPERT), lambda i, j: (0, j))
        wd_spec = pl.BlockSpec((D_FF_EXPERT, D_MODEL), lambda i, j: (j, 0))

    def rows(w):
        return pl.BlockSpec((tm, w), lambda i, j: (i, 0))

    def const(shape):
        return pl.BlockSpec(shape, lambda i, j: (0, 0))

    wspecs, wargs = [], []
    for name, spec in (("w_gate", wg_spec), ("w_up", wg_spec), ("w_down", wd_spec),
                       ("w_ple", const((PLE_DIM, D_MODEL))), ("w_pleg", const((D_MODEL, D_MODEL)))):
        ws = _wargs(lw, name, precise)
        wargs += ws
        wspecs += [spec] * len(ws)
    return pl.pallas_call(
        functools.partial(_ffn_kernel, weighted=weighted, precise=precise), grid=(n // tm, n_e),
        in_specs=[rows(D_MODEL), rows(D_MODEL), rows(PLE_DIM), rows(LANES)] + wspecs
        + [const((1, D_MODEL)), const((1, D_MODEL))],
        out_specs=[rows(D_MODEL), rows(D_MODEL)],
        out_shape=[jax.ShapeDtypeStruct((n, D_MODEL), F32), jax.ShapeDtypeStruct((n, D_MODEL), BF16)],
        scratch_shapes=[pltpu.VMEM((tm, D_MODEL), F32)],
        compiler_params=_cparams(("parallel", "arbitrary")), name="ffn")(
            hb, h, p, comb, *wargs, lw["ln2_g"], lw["ln2_b"])


def _layer_weights(i, W):
    w_in = W["w_in"][i]
    sizes = [D_AV, D_AV, D_B, D_B, D_B, FOX_HEADS, D_C, CONV_DIM, SSM_HEADS, N_BRANCH * D_MODEL]
    o = [0] + [int(s) for s in np.cumsum(sizes)]
    seg = [w_in[:, o[k]:o[k + 1]] for k in range(len(sizes))]
    w_u, w_v, w_q, w_k, w_va, w_f, w_z, w_xbc, w_dt, w_gate = seg
    pad = jnp.zeros((D_MODEL, LANES - SSM_HEADS - FOX_HEADS), F32)

    def lane_row(v):
        return jnp.pad(v.astype(F32), (0, LANES - v.shape[0])).reshape(1, LANES)

    lw = dict(
        fox_b=W["fox_b_forget"][i],
        gmlp_g=W["gmlp_ln_g"][i], gmlp_b=W["gmlp_ln_b"][i], gmlp_w=W["gmlp_w_spatial"][i], gmlp_bs=W["gmlp_b_spatial"][i],
        conv_w=W["ssm_conv_w"][i], conv_b=W["ssm_conv_b"][i].reshape(1, CONV_DIM),
        dtb_row=lane_row(W["ssm_dt_bias"][i]), dtb_col=W["ssm_dt_bias"][i].reshape(SSM_HEADS, 1),
        alog_row=lane_row(W["ssm_a_log"][i]), alog_col=W["ssm_a_log"][i].reshape(SSM_HEADS, 1),
        dvec=jnp.repeat(W["ssm_d"][i], SSM_HD).reshape(1, D_C), norm_g=W["ssm_norm_g"][i].reshape(1, D_C),
        ln1_g=W["ln1_g"][i].reshape(1, D_MODEL), ln1_b=W["ln1_b"][i].reshape(1, D_MODEL),
        ln2_g=W["ln2_g"][i].reshape(1, D_MODEL), ln2_b=W["ln2_b"][i].reshape(1, D_MODEL),
    )
    j = i // 2
    mats = dict(
        w_main=jnp.concatenate([w_xbc, w_u, w_v, w_q, w_k, w_va, w_z, w_gate], axis=1),
        w_small=jnp.concatenate([w_dt, w_f, pad], axis=1),
        w_a=W["w_branch_a"][i], w_b=W["w_branch_b"][i], w_c=W["w_branch_c"][i], w_o=W["w_out"][i],
        w_ple=W["ple_w_proj"][i], w_pleg=W["ple_w_gate"][i],
    )
    if i % 2 == 0:
        mats.update(w_gate=W["ffn_w_gate"][j], w_up=W["ffn_w_up"][j], w_down=W["ffn_w_down"][j])
    else:
        mats.update(w_gate=W["moe_w_gate"][j], w_up=W["moe_w_up"][j], w_down=W["moe_w_down"][j],
                    wr=jnp.pad(W["moe_w_router"][j], ((0, 0), (0, LANES - N_EXPERTS))))
        lw["br"] = lane_row(W["moe_b_router"][j])
    for name, w in mats.items():
        c = w * np.float32(2.0**16 + 1.0)
        hi_f = c - (c - w)
        lw[name] = hi_f.astype(BF16)
        lw[name + "_lo"] = (w - hi_f).astype(BF16)
    return lw


def _trunk_layer(x, xb, p, i, lw, bsz, t, fox_cache, ssm0, conv0, tm, precise):
    n = bsz * t
    act = _act_dtype(precise)
    xin = x if precise else xb
    proj = _matmul_call(xin, lw, "w_main", tm, 1024, precise, "in_proj")
    small = _matmul_call(xin, lw, "w_small", tm, LANES, precise, "in_proj_small")

    w_eff, bias_eff = _gmlp_weights(lw["gmlp_w"], lw["gmlp_bs"], t)
    y_a, v_n = _gmlp_call(proj, lw["gmlp_g"], lw["gmlp_b"], w_eff, bias_eff, min(n, 512), precise)

    k_new = proj[:, OFF_K:OFF_K + D_B].reshape(bsz, t, FOX_HEADS, FOX_HD)
    v_new = proj[:, OFF_VA:OFF_VA + D_B].reshape(bsz, t, FOX_HEADS, FOX_HD)
    q_t = jnp.transpose(proj[:, OFF_Q:OFF_Q + D_B].reshape(bsz, t, FOX_HEADS, FOX_HD), (0, 2, 3, 1))
    a_f = jnp.transpose(small[:, SSM_HEADS:SSM_HEADS + FOX_HEADS].reshape(bsz, t, FOX_HEADS), (0, 2, 1))
    logf_t = _logsig_call(a_f, lw["fox_b"])
    if fox_cache is None:
        k_all, v_all, logf_all, q_off = k_new, v_new, logf_t, 0
    else:
        k_c, v_c, logf_c = fox_cache
        k_all = jnp.concatenate([k_c, k_new], axis=1)
        v_all = jnp.concatenate([v_c, v_new], axis=1)
        logf_all = jnp.concatenate([jnp.transpose(logf_c, (0, 2, 1)), logf_t], axis=2)
        q_off = k_c.shape[1]
    t_k = k_all.shape[1]
    fb = _fbias_call(logf_all, 512 if t_k % 512 == 0 else t_k)
    k_a = jnp.concatenate([jnp.transpose(k_all, (0, 2, 1, 3)), jnp.transpose(fb, (0, 2, 3, 1)),
                           jnp.zeros((bsz, FOX_HEADS, t_k, FOX_DK - FOX_HD - FOX_NBIAS), F32)], axis=-1).astype(act)
    v_t = jnp.concatenate([jnp.transpose(v_all, (0, 2, 3, 1)), jnp.ones((bsz, FOX_HEADS, 1, t_k), F32),
                           jnp.zeros((bsz, FOX_HEADS, FOX_DV - FOX_HD - 1, t_k), F32)], axis=2).astype(act)
    tq, tk = (512, 512) if t % 512 == 0 else (t, t_k)
    y_bt = _fox_call(q_t, k_a, v_t, tq, tk, q_off, precise).reshape(bsz, D_B, t)

    dt_t = jnp.transpose(small[:, :SSM_HEADS].reshape(bsz, t, SSM_HEADS), (0, 2, 1))
    y_c, h_fin, conv_new = _ssd_call(proj, small, dt_t, ssm0, conv0, lw, bsz, t, precise)

    tmm = min(tm, t)
    x1, x1b = _merge_call(y_a, y_bt, y_c, proj, x, lw, tmm, bsz, t, precise)
    comb = _router_call(x1, lw, tm) if i % 2 == 1 else None
    x2, x2b = _ffn_call(x1b, x1, p, comb, lw, tm, precise)
    logf = jnp.transpose(logf_t, (0, 2, 1))
    return x2, x2b, (k_new, v_new, logf, h_fin, conv_new, v_n.reshape(bsz, t, D_AV))


def kernel(x_prompt, x_sample, p_prompt, p_sample, cache_fox_k, cache_fox_v, cache_fox_logf, state_ssm, state_conv, ln0_g, ln0_b, w_in, fox_b_forget, gmlp_ln_g, gmlp_ln_b, gmlp_w_spatial, gmlp_b_spatial, ssm_conv_w, ssm_conv_b, ssm_dt_bias, ssm_a_log, ssm_d, ssm_norm_g, w_branch_a, w_branch_b, w_branch_c, w_out, ln1_g, ln1_b, ln2_g, ln2_b, ffn_w_gate, ffn_w_up, ffn_w_down, moe_w_router, moe_b_router, moe_w_gate, moe_w_up, moe_w_down, ple_w_proj, ple_w_gate):
    W = dict(w_in=w_in, fox_b_forget=fox_b_forget, gmlp_ln_g=gmlp_ln_g, gmlp_ln_b=gmlp_ln_b,
             gmlp_w_spatial=gmlp_w_spatial, gmlp_b_spatial=gmlp_b_spatial, ssm_conv_w=ssm_conv_w,
             ssm_conv_b=ssm_conv_b, ssm_dt_bias=ssm_dt_bias, ssm_a_log=ssm_a_log, ssm_d=ssm_d,
             ssm_norm_g=ssm_norm_g, w_branch_a=w_branch_a, w_branch_b=w_branch_b,
             w_branch_c=w_branch_c, w_out=w_out, ln1_g=ln1_g, ln1_b=ln1_b, ln2_g=ln2_g, ln2_b=ln2_b,
             ffn_w_gate=ffn_w_gate, ffn_w_up=ffn_w_up, ffn_w_down=ffn_w_down,
             moe_w_router=moe_w_router, moe_b_router=moe_b_router, moe_w_gate=moe_w_gate, moe_w_up=moe_w_up,
             moe_w_down=moe_w_down, ple_w_proj=ple_w_proj, ple_w_gate=ple_w_gate)
    bp, tp, _ = x_prompt.shape
    bs, ts, _ = x_sample.shape
    tm_p = 512 if (bp * tp) % 512 == 0 else bp * tp
    tm_s = bs * ts
    xp, xpb = _layer_norm_call(x_prompt.reshape(bp * tp, D_MODEL), ln0_g, ln0_b, tm_p)
    xs, xsb = _layer_norm_call(x_sample.reshape(bs * ts, D_MODEL), ln0_g, ln0_b, tm_s)
    outs_p, outs_s = [], []
    for i in range(DEPTH):
        lw = _layer_weights(i, W)
        ssm0 = jnp.zeros((bp, SSM_HEADS, SSM_HD, SSM_N), F32)
        conv0 = jnp.zeros((bp, CONV_W - 1, CONV_DIM), F32)
        xp, xpb, st_p = _trunk_layer(xp, xpb, p_prompt[i].reshape(bp * tp, PLE_DIM), i, lw, bp, tp,
                                     None, ssm0, conv0, tm_p, False)
        xs, xsb, st_s = _trunk_layer(xs, xsb, p_sample[i].reshape(bs * ts, PLE_DIM), i, lw, bs, ts,
                                     (cache_fox_k[i], cache_fox_v[i], cache_fox_logf[i]),
                                     state_ssm[i], state_conv[i], tm_s, True)
        outs_p.append(st_p)
        outs_s.append(st_s)

    def stack(outs, k):
        return jnp.stack([o[k] for o in outs])

    return (xp.reshape(bp, tp, D_MODEL), xs.reshape(bs, ts, D_MODEL),
            stack(outs_p, 0), stack(outs_p, 1), stack(outs_p, 2), stack(outs_p, 3), stack(outs_p, 4),
            stack(outs_s, 0), stack(outs_s, 1), stack(outs_s, 2), stack(outs_s, 3), stack(outs_s, 4), stack(outs_s, 5))
```

```python
import functools
import math

import numpy as np
import jax
import jax.numpy as jnp
from jax import lax
from jax.experimental import pallas as pl
from jax.experimental.pallas import tpu as pltpu

F32 = jnp.float32
BF16 = jnp.bfloat16

D_MODEL = 1024
DEPTH = 2
CHUNK = 64
PLE_DIM = 256
GMLP_CHUNK = 128
GMLP_GROUPS = 4
D_AV = D_MODEL // 2
GMLP_GROUP_W = D_AV // GMLP_GROUPS
FOX_HD = 64
D_B = D_MODEL // 2
FOX_HEADS = D_B // FOX_HD
D_C = D_MODEL
SSM_HD = 64
SSM_HEADS = D_C // SSM_HD
SSM_N = 128
SSM_GROUPS = 2
CONV_W = 4
CONV_DIM = D_C + 2 * SSM_GROUPS * SSM_N
N_BRANCH = 3
D_FF = 11 * D_MODEL // 4
N_EXPERTS = 8
D_FF_EXPERT = D_FF // 2
ALPHA = (2.0 * DEPTH) ** 0.25
LN_EPS = 1e-5
RMS_EPS = 1e-5
NEG_INF = -1e30
LOG2E = math.log2(math.e)

LANES = 128
VMEM_LIMIT = 56 * 2**20
N_MAIN = N_BRANCH * D_MODEL + CONV_DIM + D_C + 2 * D_AV + 2 * D_B
OFF_GATE, OFF_XBC, OFF_Z, OFF_U, OFF_V, OFF_K, OFF_VA = 0, 3072, 4608, 5632, 6144, 6656, 7168
MAIN_TN = 768
DT_LANE0 = FOX_HEADS
SSD_L = 128
CONV_PAD = 8
FOX_NBIAS = 3
FOX_DK = 128
FOX_DV = 80
FOX_QSUB = 256


def _cparams(sem):
    return pltpu.CompilerParams(dimension_semantics=sem, vmem_limit_bytes=VMEM_LIMIT)


def _sigmoid(x):
    return 1.0 / (1.0 + jnp.exp(-x))


def _softplus(x):
    return jnp.maximum(x, 0.0) + jnp.log1p(jnp.exp(-jnp.abs(x)))


def _gelu(x):
    c = np.float32(np.sqrt(2.0 / np.pi))
    return x * (0.5 * (1.0 + jnp.tanh(c * (x + 0.044715 * (x * x * x)))))


def _ln_rows(x, g, b):
    mu = jnp.mean(x, axis=-1, keepdims=True)
    xc = x - mu
    var = jnp.mean(xc * xc, axis=-1, keepdims=True)
    return xc * lax.rsqrt(var + LN_EPS) * g + b


def _bf16_part(x):
    u = lax.bitcast_convert_type(x, jnp.uint32) & jnp.uint32(0xFFFF0000)
    return lax.bitcast_convert_type(u, F32)


def _split2(x):
    hi = _bf16_part(x)
    return hi.astype(BF16), (x - hi).astype(BF16)


def _split3(x):
    hi = _bf16_part(x)
    r1 = x - hi
    mid = _bf16_part(r1)
    return hi.astype(BF16), mid.astype(BF16), (r1 - mid).astype(BF16)


def _dot(a, b):
    return jnp.dot(a, b, preferred_element_type=F32)


def _dot_nt(a, b):
    return lax.dot_general(a, b, (((1,), (1,)), ((), ())), preferred_element_type=F32)


def _dot_tn(a, b):
    return lax.dot_general(a, b, (((0,), (0,)), ((), ())), preferred_element_type=F32)


_DOTS = {"nn": _dot, "nt": _dot_nt, "tn": _dot_tn}


def _mm(a, b, precise, dims="nn"):
    dot = _DOTS[dims]
    if not precise:
        a = a[0] if isinstance(a, tuple) else a.astype(BF16)
        b = b[0] if isinstance(b, tuple) else b.astype(BF16)
        return dot(a, b)
    ah, al = a if isinstance(a, tuple) else _split2(a)
    bh, bl = b if isinstance(b, tuple) else _split2(b)
    return (dot(ah, bh) + dot(al, bh)) + dot(ah, bl)


def _take_w(it, precise):
    hi = next(it)
    return hi, (next(it) if precise else None)


def _ld(w, idx=...):
    return w[0][idx], (None if w[1] is None else w[1][idx])


def _wargs(lw, name, precise):
    return [lw[name], lw[name + "_lo"]] if precise else [lw[name]]


def _act_dtype(precise):
    return F32 if precise else BF16


def _ln_kernel(x_ref, g_ref, b_ref, o_ref, ob_ref):
    y = _ln_rows(x_ref[...], g_ref[...], b_ref[...])
    o_ref[...] = y
    ob_ref[...] = y.astype(BF16)


def _layer_norm_call(x, g, b, tm):
    n, d = x.shape
    row = pl.BlockSpec((tm, d), lambda i: (i, 0))
    vec = pl.BlockSpec((1, d), lambda i: (0, 0))
    return pl.pallas_call(
        _ln_kernel, grid=(n // tm,), in_specs=[row, vec, vec], out_specs=[row, row],
        out_shape=[jax.ShapeDtypeStruct((n, d), F32), jax.ShapeDtypeStruct((n, d), BF16)],
        compiler_params=_cparams(("parallel",)), name="ln_in")(x, g.reshape(1, d), b.reshape(1, d))


def _mm_kernel(*refs, precise):
    it = iter(refs)
    x_ref = next(it)
    w = _take_w(it, precise)
    o_ref = next(it)
    o_ref[...] = _mm(x_ref[...], _ld(w), precise)


def _matmul_call(x, lw, wname, tm, tn, precise, name):
    m, k = x.shape
    ws = _wargs(lw, wname, precise)
    n = ws[0].shape[1]
    wspec = pl.BlockSpec((k, tn), lambda i, j: (0, j))
    return pl.pallas_call(
        functools.partial(_mm_kernel, precise=precise), grid=(m // tm, n // tn),
        in_specs=[pl.BlockSpec((tm, k), lambda i, j: (i, 0))] + [wspec] * len(ws),
        out_specs=pl.BlockSpec((tm, tn), lambda i, j: (i, j)),
        out_shape=jax.ShapeDtypeStruct((m, n), F32),
        compiler_params=_cparams(("parallel", "parallel")), name=name)(x, *ws)


def _gmlp_kernel(*refs, rows, precise):
    it = iter(refs)
    u_ref, v_ref, g_ref, b_ref = next(it), next(it), next(it), next(it)
    w = _take_w(it, precise)
    bias_ref, ya_ref, vn_ref = next(it), next(it), next(it)
    u = _gelu(u_ref[...])
    vn = _ln_rows(_gelu(v_ref[...]), g_ref[...], b_ref[...])
    vn_ref[...] = vn
    vb = vn if precise else vn.astype(BF16)
    for c in range(rows // GMLP_CHUNK):
        r0 = c * GMLP_CHUNK
        for g in range(GMLP_GROUPS):
            c0 = g * GMLP_GROUP_W
            mixed = _mm(_ld(w, g), vb[r0:r0 + GMLP_CHUNK, c0:c0 + GMLP_GROUP_W], precise)
            mixed = mixed + bias_ref[:, c0:c0 + GMLP_GROUP_W]
            ya_ref[r0:r0 + GMLP_CHUNK, c0:c0 + GMLP_GROUP_W] = (
                u[r0:r0 + GMLP_CHUNK, c0:c0 + GMLP_GROUP_W] * mixed).astype(ya_ref.dtype)


def _gmlp_call(proj, ln_g, ln_b, w_eff, bias_eff, rows, precise):
    n = proj.shape[0]
    ublk, vblk = OFF_U // D_AV, OFF_V // D_AV
    vec = pl.BlockSpec((1, D_AV), lambda i: (0, 0))
    ws = list(_split2(w_eff)) if precise else [w_eff.astype(BF16)]
    wspec = pl.BlockSpec((GMLP_GROUPS, GMLP_CHUNK, GMLP_CHUNK), lambda i: (0, 0, 0))
    return pl.pallas_call(
        functools.partial(_gmlp_kernel, rows=rows, precise=precise), grid=(n // rows,),
        in_specs=[pl.BlockSpec((rows, D_AV), lambda i: (i, ublk)), pl.BlockSpec((rows, D_AV), lambda i: (i, vblk)),
                  vec, vec] + [wspec] * len(ws) + [pl.BlockSpec((GMLP_CHUNK, D_AV), lambda i: (0, 0))],
        out_specs=[pl.BlockSpec((rows, D_AV), lambda i: (i, 0)), pl.BlockSpec((rows, D_AV), lambda i: (i, 0))],
        out_shape=[jax.ShapeDtypeStruct((n, D_AV), _act_dtype(precise)), jax.ShapeDtypeStruct((n, D_AV), F32)],
        compiler_params=_cparams(("parallel",)), name="gmlp")(
            proj, proj, ln_g.reshape(1, D_AV), ln_b.reshape(1, D_AV), *ws, bias_eff)


def _gmlp_weights(w_s, b_s, t):
    l = min(GMLP_CHUNK, t)
    pos = np.arange(l)
    mask = (pos[None, :] // CHUNK) <= (pos[:, None] // CHUNK)
    w = jnp.where(mask[None], w_s[:, :l, :l], 0.0)
    bias = jnp.transpose(b_s[:, :l])
    reps = GMLP_CHUNK // l
    if reps > 1:
        eye = jnp.eye(reps, dtype=w.dtype)
        w = jnp.einsum("ab,gij->gaibj", eye, w).reshape(GMLP_GROUPS, GMLP_CHUNK, GMLP_CHUNK)
        bias = jnp.tile(bias, (reps, 1))
    return w, jnp.repeat(bias, GMLP_GROUP_W, axis=1)


def _logsig_kernel(a_ref, b_ref, o_ref):
    o_ref[...] = (-_softplus(-(a_ref[...] + b_ref[...])))[:, :FOX_HEADS]


def _logsig_call(small, bias_row, tm):
    n = small.shape[0]
    return pl.pallas_call(
        _logsig_kernel, grid=(n // tm,),
        in_specs=[pl.BlockSpec((tm, LANES), lambda i: (i, 0)), pl.BlockSpec((1, LANES), lambda i: (0, 0))],
        out_specs=pl.BlockSpec((tm, FOX_HEADS), lambda i: (i, 0)),
        out_shape=jax.ShapeDtypeStruct((n, FOX_HEADS), F32),
        compiler_params=_cparams(("parallel",)), name="fox_logf")(small, bias_row)


def _qvt_kernel(*refs, precise):
    it = iter(refs)
    x_ref = next(it)
    wq, wv = _take_w(it, precise), _take_w(it, precise)
    qt_ref, vt_ref = next(it), next(it)
    x = x_ref[...]
    qt_ref[...] = (_mm(_ld(wq), x, precise, "nt") * (FOX_HD ** -0.5 * LOG2E)).astype(qt_ref.dtype)
    vt_ref[...] = _mm(_ld(wv), x, precise, "nt").astype(vt_ref.dtype)


def _qvt_call(x, lw, tm, bsz, t, precise):
    n = x.shape[0]
    per_b = t // tm
    act = _act_dtype(precise)
    ws = _wargs(lw, "w_qt", precise) + _wargs(lw, "w_vt", precise)
    out = pl.BlockSpec((None, D_B, tm), lambda i: (i // per_b, 0, i % per_b))
    return pl.pallas_call(
        functools.partial(_qvt_kernel, precise=precise), grid=(n // tm,),
        in_specs=[pl.BlockSpec((tm, D_MODEL), lambda i: (i, 0))]
        + [pl.BlockSpec((D_B, D_MODEL), lambda i: (0, 0))] * len(ws),
        out_specs=[out, out],
        out_shape=[jax.ShapeDtypeStruct((bsz, D_B, t), act), jax.ShapeDtypeStruct((bsz, D_B, t), act)],
        compiler_params=_cparams(("parallel",)), name="fox_qv_t")(x, *ws)


def _place(x, sel, precise):
    if not precise:
        return _dot(x.astype(BF16), sel)
    hi, mid, lo = _split3(x)
    return (_dot(hi, sel) + _dot(mid, sel)) + _dot(lo, sel)


def _keys_kernel(k_ref, lf_ref, tril_ref, selk_ref, selg_ref, o_ref, carry_ref, *, precise):
    @pl.when(pl.program_id(1) == 0)
    def _():
        carry_ref[...] = jnp.zeros_like(carry_ref)

    tril = tril_ref[...]
    hi, mid, lo = _split3(lf_ref[...])
    f = (_dot(tril, hi) + _dot(tril, mid)) + _dot(tril, lo) + carry_ref[0:1, :]
    tc = f.shape[0]
    carry_ref[...] = jnp.broadcast_to(f[tc - 1:tc, :], carry_ref.shape)
    g = f * (-LOG2E)
    g1 = _bf16_part(g)
    r = g - g1
    g2 = _bf16_part(r)
    ka = _place(k_ref[...], selk_ref[...], precise) + _place(g1, selg_ref[0], precise)
    ka = ka + _place(g2, selg_ref[1], precise) + _place(r - g2, selg_ref[2], precise)
    o_ref[...] = ka.astype(o_ref.dtype)


def _keys_call(k_all, logf_all, tc, precise):
    bsz, t_k, _ = k_all.shape
    sel_k = np.zeros((D_B, FOX_HEADS * FOX_DK), np.float32)
    sel_g = np.zeros((FOX_NBIAS, FOX_HEADS, FOX_HEADS * FOX_DK), np.float32)
    for h in range(FOX_HEADS):
        sel_k[h * FOX_HD + np.arange(FOX_HD), h * FOX_DK + np.arange(FOX_HD)] = 1.0
        for j in range(FOX_NBIAS):
            sel_g[j, h, h * FOX_DK + FOX_HD + j] = 1.0
    tril = jnp.tril(jnp.ones((tc, tc), F32)).astype(BF16)
    return pl.pallas_call(
        functools.partial(_keys_kernel, precise=precise), grid=(bsz, t_k // tc),
        in_specs=[pl.BlockSpec((None, tc, D_B), lambda b, j: (b, j, 0)),
                  pl.BlockSpec((None, tc, FOX_HEADS), lambda b, j: (b, j, 0)),
                  pl.BlockSpec((tc, tc), lambda b, j: (0, 0)),
                  pl.BlockSpec(sel_k.shape, lambda b, j: (0, 0)), pl.BlockSpec(sel_g.shape, lambda b, j: (0, 0, 0))],
        out_specs=pl.BlockSpec((None, tc, FOX_HEADS * FOX_DK), lambda b, j: (b, j, 0)),
        out_shape=jax.ShapeDtypeStruct((bsz, t_k, FOX_HEADS * FOX_DK), _act_dtype(precise)),
        scratch_shapes=[pltpu.VMEM((8, FOX_HEADS), F32)],
        compiler_params=_cparams(("parallel", "arbitrary")), name="fox_keys")(
            k_all, logf_all, tril, jnp.asarray(sel_k, BF16), jnp.asarray(sel_g, BF16))


def _fox_kernel(qt_ref, ka_ref, vt_ref, o_ref, qa_sc, m_sc, acc_sc, s_sc, mb_sc, *, tq, tk, qs, q_off, precise):
    i = pl.program_id(2)
    act = qa_sc.dtype
    rows = lax.broadcasted_iota(jnp.int32, (FOX_DK - FOX_HD, tq), 0)
    ones_rows = jnp.where(rows < FOX_NBIAS, 1.0, 0.0).astype(act)
    qa_sc[...] = jnp.concatenate([qt_ref[...], ones_rows], axis=0)
    vrows = lax.broadcasted_iota(jnp.int32, (FOX_DV - FOX_HD, tk), 0)
    v_extra = jnp.where(vrows < 1, 1.0, 0.0).astype(act)
    m_sc[...] = jnp.full_like(m_sc, -jnp.inf)
    acc_sc[...] = jnp.zeros_like(acc_sc)
    first_q = q_off + i * tq
    subs = [slice(c * qs, (c + 1) * qs) for c in range(tq // qs)]

    def scores(k0):
        ka = ka_ref[pl.ds(k0, tk), :]
        for sl in subs:
            s = _mm(ka, qa_sc[:, sl], precise)
            s_sc[:, sl] = s
            mb_sc[:, sl] = jnp.max(s, axis=0, keepdims=True)

    def consume(k0, masked):
        vt = jnp.concatenate([vt_ref[:, pl.ds(k0, tk)], v_extra], axis=0)
        for c, sl in enumerate(subs):
            s = s_sc[:, sl]
            if masked:
                kpos = k0 + lax.broadcasted_iota(jnp.int32, (tk, qs), 0)
                qpos = first_q + c * qs + lax.broadcasted_iota(jnp.int32, (tk, qs), 1)
                s = jnp.where(kpos <= qpos, s, NEG_INF)
                mb = jnp.max(s, axis=0, keepdims=True)
            else:
                mb = mb_sc[:, sl]
            m_prev = m_sc[:, sl]
            m_new = jnp.maximum(m_prev, mb)
            alpha = jnp.exp2(m_prev - m_new)
            p = jnp.exp2(s - m_new)
            acc_sc[:, sl] = alpha * acc_sc[:, sl] + _mm(vt, p, precise)
            m_sc[:, sl] = m_new

    scores(0)
    if ka_ref.shape[0] == tk:
        consume(0, True)
    else:
        assert tq == tk and q_off == 0

        def body(j, c):
            k0 = pl.multiple_of(j * tk, tk)
            consume(k0, False)
            scores(pl.multiple_of(k0 + tk, tk))
            return c

        lax.fori_loop(0, i, body, 0)
        consume(pl.multiple_of(i * tk, tk), True)
    acc = acc_sc[...]
    o_ref[...] = (acc[:FOX_HD] / acc[FOX_HD:FOX_HD + 1]).astype(o_ref.dtype)


def _fox_call(q_t, k_a, v_t, tq, tk, q_off, precise):
    bsz, _, t_q = q_t.shape
    t_k = k_a.shape[1]
    qs = min(FOX_QSUB, tq)
    act = _act_dtype(precise)
    qo = pl.BlockSpec((None, FOX_HD, tq), lambda b, hh, i: (b, hh, i))
    return pl.pallas_call(
        functools.partial(_fox_kernel, tq=tq, tk=tk, qs=qs, q_off=q_off, precise=precise),
        grid=(bsz, FOX_HEADS, t_q // tq),
        in_specs=[qo, pl.BlockSpec((None, t_k, FOX_DK), lambda b, hh, i: (b, 0, hh)),
                  pl.BlockSpec((None, FOX_HD, t_k), lambda b, hh, i: (b, hh, 0))],
        out_specs=qo, out_shape=jax.ShapeDtypeStruct((bsz, D_B, t_q), act),
        scratch_shapes=[pltpu.VMEM((FOX_DK, tq), act), pltpu.VMEM((1, tq), F32), pltpu.VMEM((FOX_DV, tq), F32),
                        pltpu.VMEM((tk, tq), F32), pltpu.VMEM((1, tq), F32)],
        compiler_params=_cparams(("parallel", "parallel", "arbitrary")), name="fox_attn")(q_t, k_a, v_t)


def _ssd_kernel(xbc_ref, z0_ref, z1_ref, dtp_ref, dtt_ref, h0_ref, c0_ref, cw_ref, cb_ref, dtb_row_ref, dtb_col_ref,
                alog_row_ref, alog_col_ref, dvec_ref, ng_ref, tril_ref, triu_ref,
                yc_ref, hfin_ref, cnew_ref, h_sc, pad_sc, y_sc, *, L, precise):
    c = pl.program_id(1)

    @pl.when(c == 0)
    def _():
        h_sc[...] = h0_ref[...]
        pad_sc[CONV_PAD - (CONV_W - 1):CONV_PAD, :] = c0_ref[...]

    xin = xbc_ref[...]
    pad_sc[CONV_PAD:CONV_PAD + L, :] = xin
    y = cb_ref[...]
    for j in range(CONV_W - 1):
        r0 = CONV_PAD - (CONV_W - 1) + j
        y = y + pad_sc[r0:r0 + L, :] * cw_ref[j:j + 1, :]
    y = y + xin * cw_ref[CONV_W - 1:CONV_W, :]
    tail = pad_sc[CONV_PAD + L - (CONV_W - 1):CONV_PAD + L, :]
    pad_sc[CONV_PAD - (CONV_W - 1):CONV_PAD, :] = tail
    cnew_ref[...] = tail
    act = y * _sigmoid(y)

    dt_c = _softplus(dtp_ref[...] + dtb_row_ref[...])
    a_c = dt_c * (-jnp.exp(alog_row_ref[...]))
    tril = tril_ref[...]
    hi, mid, lo = _split3(a_c)
    acum_c = (_dot(tril, hi) + _dot(tril, mid)) + _dot(tril, lo)
    dt_r = _softplus(dtt_ref[...] + dtb_col_ref[...])
    a_r = dt_r * (-jnp.exp(alog_col_ref[...]))
    triu = triu_ref[...]
    hi, mid, lo = _split3(a_r)
    acum_r = (_dot(hi, triu) + _dot(mid, triu)) + _dot(lo, triu)

    row = lax.broadcasted_iota(jnp.int32, (L, L), 0)
    col = lax.broadcasted_iota(jnp.int32, (L, L), 1)
    causal = col <= row
    rep = SSM_HEADS // SSM_GROUPS
    for g in range(SSM_GROUPS):
        b0 = D_C + g * SSM_N
        c0 = D_C + SSM_GROUPS * SSM_N + g * SSM_N
        b_g = act[:, b0:b0 + SSM_N]
        c_g = act[:, c0:c0 + SSM_N]
        b_gs = _split2(b_g) if precise else (b_g.astype(BF16), None)
        cb = _mm(c_g, b_gs, precise, "nt")
        for hh in range(rep):
            h = g * rep + hh
            lane = DT_LANE0 + h
            acol = acum_c[:, lane:lane + 1]
            arow = acum_r[h:h + 1, :]
            alast = acum_c[L - 1:L, lane:lane + 1]
            decay = jnp.exp(jnp.where(causal, acol - arow, -jnp.inf))
            m = (cb * decay) * dt_r[h:h + 1, :]
            x_h = act[:, h * SSM_HD:(h + 1) * SSM_HD]
            y_diag = _mm(m, x_h, precise)
            h_prev = h_sc[h]
            y_off = _mm(c_g * jnp.exp(acol), h_prev, precise, "nt")
            wcol = jnp.exp(alast - acol) * dt_c[:, lane:lane + 1]
            s_h = _mm(x_h * wcol, b_gs, precise, "tn")
            h_sc[h] = jnp.exp(alast) * h_prev + s_h
            y_sc[:, h * SSM_HD:(h + 1) * SSM_HD] = (y_diag + y_off) + dvec_ref[:, h * SSM_HD:(h + 1) * SSM_HD] * x_h

    @pl.when(c == pl.num_programs(1) - 1)
    def _():
        hfin_ref[...] = h_sc[...]

    z = jnp.concatenate([z0_ref[...], z1_ref[...]], axis=1)
    yg =y_sc[...] * (z * _sigmoid(z))
    ms = jnp.mean(yg * yg, axis=-1, keepdims=True)
    yc_ref[...] = ((yg * lax.rsqrt(ms + RMS_EPS)) * ng_ref[...]).astype(yc_ref.dtype)


def _ssd_call(proj, small, dt_t, h0, conv0, lw, bsz, t, precise):
    L = min(SSD_L, t)
    nc = t // L
    zhalf = D_C // 2
    zblk = OFF_Z // zhalf

    def const(shape):
        return pl.BlockSpec(shape, lambda b, c: (0,) * len(shape))

    in_specs = [
        pl.BlockSpec((L, CONV_DIM), lambda b, c: (b * nc + c, OFF_XBC // CONV_DIM)),
        pl.BlockSpec((L, zhalf), lambda b, c: (b * nc + c, zblk)),
        pl.BlockSpec((L, zhalf), lambda b, c: (b * nc + c, zblk + 1)),
        pl.BlockSpec((L, LANES), lambda b, c: (b * nc + c, 0)),
        pl.BlockSpec((None, SSM_HEADS, L), lambda b, c: (b, 0, c)),
        pl.BlockSpec((None, SSM_HEADS, SSM_HD, SSM_N), lambda b, c: (b, 0, 0, 0)),
        pl.BlockSpec((None, CONV_W - 1, CONV_DIM), lambda b, c: (b, 0, 0)),
        const((CONV_W, CONV_DIM)), const((1, CONV_DIM)), const((1, LANES)), const((SSM_HEADS, 1)),
        const((1, LANES)), const((SSM_HEADS, 1)), const((1, D_C)), const((1, D_C)), const((L, L)), const((L, L)),
    ]
    out_specs = [
        pl.BlockSpec((L, D_C), lambda b, c: (b * nc + c, 0)),
        pl.BlockSpec((None, SSM_HEADS, SSM_HD, SSM_N), lambda b, c: (b, 0, 0, 0)),
        pl.BlockSpec((None, CONV_W - 1, CONV_DIM), lambda b, c: (b, 0, 0)),
    ]
    out_shape = [jax.ShapeDtypeStruct((bsz * t, D_C), _act_dtype(precise)),
                 jax.ShapeDtypeStruct((bsz, SSM_HEADS, SSM_HD, SSM_N), F32),
                 jax.ShapeDtypeStruct((bsz, CONV_W - 1, CONV_DIM), F32)]
    ones = jnp.ones((L, L), F32)
    return pl.pallas_call(
        functools.partial(_ssd_kernel, L=L, precise=precise), grid=(bsz, nc), in_specs=in_specs,
        out_specs=out_specs, out_shape=out_shape,
        scratch_shapes=[pltpu.VMEM((SSM_HEADS, SSM_HD, SSM_N), F32), pltpu.VMEM((CONV_PAD + L, CONV_DIM), F32),
                        pltpu.VMEM((L, D_C), F32)],
        compiler_params=_cparams(("parallel", "arbitrary")), name="ssd")(
            proj, proj, proj, small, dt_t, h0, conv0, lw["conv_w"], lw["conv_b"], lw["dtb_row"], lw["dtb_col"],
            lw["alog_row"], lw["alog_col"], lw["dvec"], lw["norm_g"],
            jnp.tril(ones).astype(BF16), jnp.triu(ones).astype(BF16))


def _merge_kernel(*refs, precise):
    it = iter(refs)
    ya_ref, ybt_ref, yc_ref, g0_ref, g1_ref, g2_ref, x_ref = [next(it) for _ in range(7)]
    wa, wb, wc, wo = [_take_w(it, precise) for _ in range(4)]
    lg_ref, lb_ref, o_ref, ob_ref = next(it), next(it), next(it), next(it)
    merged = _sigmoid(g0_ref[...]) * _mm(ya_ref[...], _ld(wa), precise)
    merged = merged + _sigmoid(g1_ref[...]) * _mm(ybt_ref[...], _ld(wb), precise, "tn")
    merged = merged + _sigmoid(g2_ref[...]) * _mm(yc_ref[...], _ld(wc), precise)
    mix = _mm(merged, _ld(wo), precise)
    y = _ln_rows(ALPHA * x_ref[...] + mix, lg_ref[...], lb_ref[...])
    o_ref[...] = y
    ob_ref[...] = y.astype(BF16)


def _merge_call(ya, yb_t, yc, proj, x, lw, tm, bsz, t, precise):
    n = x.shape[0]
    gblk = OFF_GATE // D_MODEL
    per_b = t // tm

    def rows(w):
        return pl.BlockSpec((tm, w), lambda i: (i, 0))

    def const(shape):
        return pl.BlockSpec(shape, lambda i: (0, 0))

    gates = [pl.BlockSpec((tm, D_MODEL), lambda i, j=j: (i, gblk + j)) for j in range(N_BRANCH)]
    wspecs, wargs = [], []
    for name, k in (("w_a", D_AV), ("w_b", D_B), ("w_c", D_C), ("w_o", D_MODEL)):
        ws = _wargs(lw, name, precise)
        wargs += ws
        wspecs += [const((k, D_MODEL))] * len(ws)
    return pl.pallas_call(
        functools.partial(_merge_kernel, precise=precise), grid=(n // tm,),
        in_specs=[rows(D_AV), pl.BlockSpec((None, D_B, tm), lambda i: (i // per_b, 0, i % per_b)), rows(D_C)]
        + gates + [rows(D_MODEL)] + wspecs + [const((1, D_MODEL)), const((1, D_MODEL))],
        out_specs=[rows(D_MODEL), rows(D_MODEL)],
        out_shape=[jax.ShapeDtypeStruct((n, D_MODEL), F32), jax.ShapeDtypeStruct((n, D_MODEL), BF16)],
        compiler_params=_cparams(("parallel",)), name="merge")(
            ya, yb_t, yc, proj, proj, proj, x, *wargs, lw["ln1_g"], lw["ln1_b"])


def _router_kernel(h_ref, whi_ref, wlo_ref, br_ref, o_ref):
    logits = _mm(h_ref[...], (whi_ref[...], wlo_ref[...]), True) + br_ref[...]
    lane = lax.broadcasted_iota(jnp.int32, logits.shape, 1)
    lg = jnp.where(lane < N_EXPERTS, logits, -jnp.inf)
    m1 = jnp.max(lg, axis=-1, keepdims=True)
    i1 = jnp.min(jnp.where(lg == m1, lane, LANES), axis=-1, keepdims=True)
    lg2 = jnp.where(lane == i1, -jnp.inf, lg)
    m2 = jnp.max(lg2, axis=-1, keepdims=True)
    i2 = jnp.min(jnp.where(lg2 == m2, lane, LANES), axis=-1, keepdims=True)
    e = jnp.exp(m2 - m1)
    den = 1.0 + e
    o_ref[...] = jnp.where(lane == i1, 1.0 / den, 0.0) + jnp.where(lane == i2, e / den, 0.0)


def _router_call(h, lw, tm):
    n = h.shape[0]
    return pl.pallas_call(
        _router_kernel, grid=(n // tm,),
        in_specs=[pl.BlockSpec((tm, D_MODEL), lambda i: (i, 0)), pl.BlockSpec((D_MODEL, LANES), lambda i: (0, 0)),
                  pl.BlockSpec((D_MODEL, LANES), lambda i: (0, 0)), pl.BlockSpec((1, LANES), lambda i: (0, 0))],
        out_specs=pl.BlockSpec((tm, LANES), lambda i: (i, 0)),
        out_shape=jax.ShapeDtypeStruct((n, LANES), F32),
        compiler_params=_cparams(("parallel",)), name="router")(h, lw["wr"], lw["wr_lo"], lw["br"])


def _ffn_kernel(*refs, weighted, precise):
    it = iter(refs)
    hb_ref, h_ref, p_ref, comb_ref = next(it), next(it), next(it), next(it)
    wg, wu, wd, wp, wpg = [_take_w(it, precise) for _ in range(5)]
    lg_ref, lb_ref, o_ref, ob_ref, acc_sc = next(it), next(it), next(it), next(it), next(it)
    j = pl.program_id(1)
    hb = h_ref[...] if precise else hb_ref[...]

    @pl.when(j == 0)
    def _():
        ple = _mm(p_ref[...], _ld(wp), precise) * _sigmoid(_mm(hb, _ld(wpg), precise))
        acc_sc[...] = ALPHA * h_ref[...] + ple

    gate = _mm(hb, _ld(wg), precise)
    up = _mm(hb, _ld(wu), precise)
    out = _mm((gate * _sigmoid(gate)) * up, _ld(wd), precise)
    if weighted:
        comb = comb_ref[...]
        lane = lax.broadcasted_iota(jnp.int32, comb.shape, 1)
        out = jnp.sum(jnp.where(lane == j, comb, 0.0), axis=-1, keepdims=True) * out
    acc_sc[...] += out

    @pl.when(j == pl.num_programs(1) - 1)
    def _():
        y = _ln_rows(acc_sc[...], lg_ref[...], lb_ref[...])
        o_ref[...] = y
        ob_ref[...] = y.astype(BF16)


def _ffn_call(hb, h, p, comb, lw, tm, precise):
    n = h.shape[0]
    weighted = comb is not None
    if weighted:
        n_e = N_EXPERTS
        wg_spec = pl.BlockSpec((None, D_MODEL, D_FF_EXPERT), lambda i, j: (j, 0, 0))
        wd_spec = pl.BlockSpec((None, D_FF_EXPERT, D_MODEL), lambda i, j: (j, 0, 0))
    else:
        n_e = D_FF // D_FF_EXPERT
        comb = jnp.zeros((n, LANES), F32)
        wg_spec = pl.BlockSpec((D_MODEL, D_FF_EXPERT), lambda i, j: (0, j))
        wd_spec = pl.BlockSpec((D_FF_EXPERT, D_MODEL), lambda i, j: (j, 0))

    def rows(w):
        return pl.BlockSpec((tm, w), lambda i, j: (i, 0))

    def const(shape):
        return pl.BlockSpec(shape, lambda i, j: (0, 0))

    wspecs, wargs = [], []
    for name, spec in (("w_gate", wg_spec), ("w_up", wg_spec), ("w_down", wd_spec),
                       ("w_ple", const((PLE_DIM, D_MODEL))), ("w_pleg", const((D_MODEL, D_MODEL)))):
        ws = _wargs(lw, name, precise)
        wargs += ws
        wspecs += [spec] * len(ws)
    return pl.pallas_call(
        functools.partial(_ffn_kernel, weighted=weighted, precise=precise), grid=(n // tm, n_e),
        in_specs=[rows(D_MODEL), rows(D_MODEL), rows(PLE_DIM), rows(LANES)] + wspecs
        + [const((1, D_MODEL)), const((1, D_MODEL))],
        out_specs=[rows(D_MODEL), rows(D_MODEL)],
        out_shape=[jax.ShapeDtypeStruct((n, D_MODEL), F32), jax.ShapeDtypeStruct((n, D_MODEL), BF16)],
        scratch_shapes=[pltpu.VMEM((tm, D_MODEL), F32)],
        compiler_params=_cparams(("parallel", "arbitrary")), name="ffn")(
            hb, h, p, comb, *wargs, lw["ln2_g"], lw["ln2_b"])


def _layer_weights(i, W):
    w_in = W["w_in"][i]
    sizes = [D_AV, D_AV, D_B, D_B, D_B, FOX_HEADS, D_C, CONV_DIM, SSM_HEADS, N_BRANCH * D_MODEL]
    o = [0] + [int(s) for s in np.cumsum(sizes)]
    seg = [w_in[:, o[k]:o[k + 1]] for k in range(len(sizes))]
    w_u, w_v, w_q, w_k, w_va, w_f, w_z, w_xbc, w_dt, w_gate = seg
    pad = jnp.zeros((D_MODEL, LANES - SSM_HEADS - FOX_HEADS), F32)

    def lane_row(v, lane0=0):
        return jnp.pad(v.astype(F32), (lane0, LANES - lane0 - v.shape[0])).reshape(1, LANES)

    lw = dict(
        fox_b=lane_row(W["fox_b_forget"][i]),
        gmlp_g=W["gmlp_ln_g"][i], gmlp_b=W["gmlp_ln_b"][i], gmlp_w=W["gmlp_w_spatial"][i], gmlp_bs=W["gmlp_b_spatial"][i],
        conv_w=W["ssm_conv_w"][i], conv_b=W["ssm_conv_b"][i].reshape(1, CONV_DIM),
        dtb_row=lane_row(W["ssm_dt_bias"][i], DT_LANE0), dtb_col=W["ssm_dt_bias"][i].reshape(SSM_HEADS, 1),
        alog_row=lane_row(W["ssm_a_log"][i], DT_LANE0), alog_col=W["ssm_a_log"][i].reshape(SSM_HEADS, 1),
        dvec=jnp.repeat(W["ssm_d"][i], SSM_HD).reshape(1, D_C), norm_g=W["ssm_norm_g"][i].reshape(1, D_C),
        ln1_g=W["ln1_g"][i].reshape(1, D_MODEL), ln1_b=W["ln1_b"][i].reshape(1, D_MODEL),
        ln2_g=W["ln2_g"][i].reshape(1, D_MODEL), ln2_b=W["ln2_b"][i].reshape(1, D_MODEL),
    )
    j = i // 2
    mats = dict(
        w_main=jnp.concatenate([w_gate, w_xbc, w_z, w_u, w_v, w_k, w_va], axis=1),
        w_small=jnp.concatenate([w_f, w_dt, pad], axis=1),
        w_qt=jnp.transpose(w_q), w_vt=jnp.transpose(w_va),
        w_a=W["w_branch_a"][i], w_b=W["w_branch_b"][i], w_c=W["w_branch_c"][i], w_o=W["w_out"][i],
        w_ple=W["ple_w_proj"][i], w_pleg=W["ple_w_gate"][i],
    )
    if i % 2 == 0:
        mats.update(w_gate=W["ffn_w_gate"][j], w_up=W["ffn_w_up"][j], w_down=W["ffn_w_down"][j])
    else:
        mats.update(w_gate=W["moe_w_gate"][j], w_up=W["moe_w_up"][j], w_down=W["moe_w_down"][j],
                    wr=jnp.pad(W["moe_w_router"][j], ((0, 0), (0, LANES - N_EXPERTS))))
        lw["br"] = lane_row(W["moe_b_router"][j])
    for name, w in mats.items():
        c = w * np.float32(2.0**16 + 1.0)
        hi_f = c - (c - w)
        lw[name] = hi_f.astype(BF16)
        lw[name + "_lo"] = (w - hi_f).astype(BF16)
    return lw


def _trunk_layer(x, xb, p, i, lw, bsz, t, fox_cache, ssm0, conv0, tm, precise):
    n = bsz * t
    act = _act_dtype(precise)
    xin = x if precise else xb
    tmm = min(tm, t)
    proj = _matmul_call(xin, lw, "w_main", min(n, 2048), MAIN_TN, precise, "in_proj")
    small = _matmul_call(xin, lw, "w_small", tm, LANES, precise, "in_proj_small")

    w_eff, bias_eff = _gmlp_weights(lw["gmlp_w"], lw["gmlp_bs"], t)
    y_a, v_n = _gmlp_call(proj, lw["gmlp_g"], lw["gmlp_b"], w_eff, bias_eff, min(n, 512), precise)

    k_new = proj[:, OFF_K:OFF_K + D_B].reshape(bsz, t, D_B)
    v_new = proj[:, OFF_VA:OFF_VA + D_B].reshape(bsz, t, FOX_HEADS, FOX_HD)
    logf = _logsig_call(small, lw["fox_b"], tm).reshape(bsz, t, FOX_HEADS)
    q_t, v_t = _qvt_call(xin, lw, tmm, bsz, t, precise)
    if fox_cache is None:
        k_all, logf_all, q_off = k_new, logf, 0
    else:
        k_c, v_c, logf_c = fox_cache
        q_off = k_c.shape[1]
        k_all = jnp.concatenate([k_c.reshape(bsz, q_off, D_B), k_new], axis=1)
        v_t = jnp.concatenate([jnp.transpose(v_c.reshape(bsz, q_off, D_B), (0, 2, 1)).astype(act), v_t], axis=2)
        logf_all = jnp.concatenate([logf_c, logf], axis=1)
    t_k = k_all.shape[1]
    k_a = _keys_call(k_all, logf_all, 512 if t_k % 512 == 0 else t_k, precise)
    tq, tk = (512, 512) if t % 512 == 0 else (t, t_k)
    y_bt = _fox_call(q_t, k_a, v_t, tq, tk, q_off, precise)

    dt_t = jnp.transpose(small[:, DT_LANE0:DT_LANE0 + SSM_HEADS].reshape(bsz, t, SSM_HEADS), (0, 2, 1))
    y_c, h_fin, conv_new = _ssd_call(proj, small, dt_t, ssm0, conv0, lw, bsz, t, precise)

    x1, x1b = _merge_call(y_a, y_bt, y_c, proj, x, lw, tmm, bsz, t, precise)
    comb = _router_call(x1, lw, tm) if i % 2 == 1 else None
    x2, x2b = _ffn_call(x1b, x1, p, comb, lw, tm, precise)
    return x2, x2b, (k_new.reshape(bsz, t, FOX_HEADS, FOX_HD), v_new, logf, h_fin, conv_new,
                     v_n.reshape(bsz, t, D_AV))


def kernel(x_prompt, x_sample, p_prompt, p_sample, cache_fox_k, cache_fox_v, cache_fox_logf, state_ssm, state_conv, ln0_g, ln0_b, w_in, fox_b_forget, gmlp_ln_g, gmlp_ln_b, gmlp_w_spatial, gmlp_b_spatial, ssm_conv_w, ssm_conv_b, ssm_dt_bias, ssm_a_log, ssm_d, ssm_norm_g, w_branch_a, w_branch_b, w_branch_c, w_out, ln1_g, ln1_b, ln2_g, ln2_b, ffn_w_gate, ffn_w_up, ffn_w_down, moe_w_router, moe_b_router, moe_w_gate, moe_w_up, moe_w_down, ple_w_proj, ple_w_gate):
    W = dict(w_in=w_in, fox_b_forget=fox_b_forget, gmlp_ln_g=gmlp_ln_g, gmlp_ln_b=gmlp_ln_b,
             gmlp_w_spatial=gmlp_w_spatial, gmlp_b_spatial=gmlp_b_spatial, ssm_conv_w=ssm_conv_w,
             ssm_conv_b=ssm_conv_b, ssm_dt_bias=ssm_dt_bias, ssm_a_log=ssm_a_log, ssm_d=ssm_d,
             ssm_norm_g=ssm_norm_g, w_branch_a=w_branch_a, w_branch_b=w_branch_b,
             w_branch_c=w_branch_c, w_out=w_out, ln1_g=ln1_g, ln1_b=ln1_b, ln2_g=ln2_g, ln2_b=ln2_b,
             ffn_w_gate=ffn_w_gate, ffn_w_up=ffn_w_up, ffn_w_down=ffn_w_down,
             moe_w_router=moe_w_router, moe_b_router=moe_b_router, moe_w_gate=moe_w_gate, moe_w_up=moe_w_up,
             moe_w_down=moe_w_down, ple_w_proj=ple_w_proj, ple_w_gate=ple_w_gate)
    bp, tp, _ = x_prompt.shape
    bs, ts, _ = x_sample.shape
    tm_p = 512 if (bp * tp) % 512 == 0 else bp * tp
    tm_s = bs * ts
    xp, xpb = _layer_norm_call(x_prompt.reshape(bp * tp, D_MODEL), ln0_g, ln0_b, tm_p)
    xs, xsb = _layer_norm_call(x_sample.reshape(bs * ts, D_MODEL), ln0_g, ln0_b, tm_s)
    outs_p, outs_s = [], []
    for i in range(DEPTH):
        lw = _layer_weights(i, W)
        ssm0 = jnp.zeros((bp, SSM_HEADS, SSM_HD, SSM_N), F32)
        conv0 = jnp.zeros((bp, CONV_W - 1, CONV_DIM), F32)
        xp, xpb, st_p = _trunk_layer(xp, xpb, p_prompt[i].reshape(bp * tp, PLE_DIM), i, lw, bp, tp,
                                     None, ssm0, conv0, tm_p, False)
        xs, xsb, st_s = _trunk_layer(xs, xsb, p_sample[i].reshape(bs * ts, PLE_DIM), i, lw, bs, ts,
                                     (cache_fox_k[i], cache_fox_v[i], cache_fox_logf[i]),
                                     state_ssm[i], state_conv[i], tm_s, True)
        outs_p.append(st_p)
        outs_s.append(st_s)

    def stack(outs, k):
        return jnp.stack([o[k] for o in outs])

    return (xp.reshape(bp, tp, D_MODEL), xs.reshape(bs, ts, D_MODEL),
            stack(outs_p, 0), stack(outs_p, 1), stack(outs_p, 2), stack(outs_p, 3), stack(outs_p, 4),
            stack(outs_s, 0), stack(outs_s, 1), stack(outs_s, 2), stack(outs_s, 3), stack(outs_s, 4), stack(outs_s, 5))
```

```python
import functools
import math

import numpy as np
import jax
import jax.numpy as jnp
from jax import lax
from jax.experimental import pallas as pl
from jax.experimental.pallas import tpu as pltpu

F32 = jnp.float32
BF16 = jnp.bfloat16

D_MODEL = 1024
DEPTH = 2
CHUNK = 64
PLE_DIM = 256
GMLP_CHUNK = 128
GMLP_GROUPS = 4
D_AV = D_MODEL // 2
GMLP_GROUP_W = D_AV // GMLP_GROUPS
FOX_HD = 64
D_B = D_MODEL // 2
FOX_HEADS = D_B // FOX_HD
D_C = D_MODEL
SSM_HD = 64
SSM_HEADS = D_C // SSM_HD
SSM_N = 128
SSM_GROUPS = 2
CONV_W = 4
CONV_DIM = D_C + 2 * SSM_GROUPS * SSM_N
N_BRANCH = 3
D_FF = 11 * D_MODEL // 4
N_EXPERTS = 8
D_FF_EXPERT = D_FF // 2
ALPHA = (2.0 * DEPTH) ** 0.25
LN_EPS = 1e-5
RMS_EPS = 1e-5
NEG_INF = -1e30
LOG2E = math.log2(math.e)

LANES = 128
VMEM_LIMIT = 56 * 2**20
N_MAIN = N_BRANCH * D_MODEL + CONV_DIM + D_C + 2 * D_AV + 2 * D_B
OFF_GATE, OFF_XBC, OFF_Z, OFF_U, OFF_V, OFF_K, OFF_VA = 0, 3072, 4608, 5632, 6144, 6656, 7168
MAIN_TN = 768
DT_LANE0 = FOX_HEADS
SSD_L = 128
CONV_PAD = 8
FOX_NBIAS = 3
FOX_DK = 128
FOX_DV = 80
FOX_QSUB = 256


def _cparams(sem):
    return pltpu.CompilerParams(dimension_semantics=sem, vmem_limit_bytes=VMEM_LIMIT)


def _sigmoid(x):
    return 1.0 / (1.0 + jnp.exp(-x))


def _softplus(x):
    return jnp.maximum(x, 0.0) + jnp.log1p(jnp.exp(-jnp.abs(x)))


def _gelu(x):
    c = np.float32(np.sqrt(2.0 / np.pi))
    return x * (0.5 * (1.0 + jnp.tanh(c * (x + 0.044715 * (x * x * x)))))


def _ln_rows(x, g, b):
    mu = jnp.mean(x, axis=-1, keepdims=True)
    xc = x - mu
    var = jnp.mean(xc * xc, axis=-1, keepdims=True)
    return xc * lax.rsqrt(var + LN_EPS) * g + b


def _bf16_part(x):
    u = lax.bitcast_convert_type(x, jnp.uint32) & jnp.uint32(0xFFFF0000)
    return lax.bitcast_convert_type(u, F32)


def _split2(x):
    hi = _bf16_part(x)
    return hi.astype(BF16), (x - hi).astype(BF16)


def _split3(x):
    hi = _bf16_part(x)
    r1 = x - hi
    mid = _bf16_part(r1)
    return hi.astype(BF16), mid.astype(BF16), (r1 - mid).astype(BF16)


def _dot(a, b):
    return jnp.dot(a, b, preferred_element_type=F32)


def _dot_nt(a, b):
    return lax.dot_general(a, b, (((1,), (1,)), ((), ())), preferred_element_type=F32)


def _dot_tn(a, b):
    return lax.dot_general(a, b, (((0,), (0,)), ((), ())), preferred_element_type=F32)


_DOTS = {"nn": _dot, "nt": _dot_nt, "tn": _dot_tn}


def _mm(a, b, precise, dims="nn"):
    dot = _DOTS[dims]
    if not precise:
        a = a[0] if isinstance(a, tuple) else a.astype(BF16)
        b = b[0] if isinstance(b, tuple) else b.astype(BF16)
        return dot(a, b)
    ah, al = a if isinstance(a, tuple) else _split2(a)
    bh, bl = b if isinstance(b, tuple) else _split2(b)
    return (dot(ah, bh) + dot(al, bh)) + dot(ah, bl)


def _take_w(it, precise):
    hi = next(it)
    return hi, (next(it) if precise else None)


def _ld(w, idx=...):
    return w[0][idx], (None if w[1] is None else w[1][idx])


def _wargs(lw, name, precise):
    return [lw[name], lw[name + "_lo"]] if precise else [lw[name]]


def _act_dtype(precise):
    return F32 if precise else BF16


def _ln_kernel(x_ref, g_ref, b_ref, o_ref, ob_ref):
    y = _ln_rows(x_ref[...], g_ref[...], b_ref[...])
    o_ref[...] = y
    ob_ref[...] = y.astype(BF16)


def _layer_norm_call(x, g, b, tm):
    n, d = x.shape
    row = pl.BlockSpec((tm, d), lambda i: (i, 0))
    vec = pl.BlockSpec((1, d), lambda i: (0, 0))
    return pl.pallas_call(
        _ln_kernel, grid=(n // tm,), in_specs=[row, vec, vec], out_specs=[row, row],
        out_shape=[jax.ShapeDtypeStruct((n, d), F32), jax.ShapeDtypeStruct((n, d), BF16)],
        compiler_params=_cparams(("parallel",)), name="ln_in")(x, g.reshape(1, d), b.reshape(1, d))


def _mm_kernel(*refs, precise):
    it = iter(refs)
    x_ref = next(it)
    w = _take_w(it, precise)
    o_ref = next(it)
    o_ref[...] = _mm(x_ref[...], _ld(w), precise)


def _matmul_call(x, lw, wname, tm, tn, precise, name):
    m, k = x.shape
    ws = _wargs(lw, wname, precise)
    n = ws[0].shape[1]
    wspec = pl.BlockSpec((k, tn), lambda i, j: (0, j))
    return pl.pallas_call(
        functools.partial(_mm_kernel, precise=precise), grid=(m // tm, n // tn),
        in_specs=[pl.BlockSpec((tm, k), lambda i, j: (i, 0))] + [wspec] * len(ws),
        out_specs=pl.BlockSpec((tm, tn), lambda i, j: (i, j)),
        out_shape=jax.ShapeDtypeStruct((m, n), F32),
        compiler_params=_cparams(("parallel", "parallel")), name=name)(x, *ws)


def _gmlp_kernel(*refs, rows, precise):
    it = iter(refs)
    u_ref, v_ref, g_ref, b_ref = next(it), next(it), next(it), next(it)
    w = _take_w(it, precise)
    bias_ref, ya_ref, vn_ref = next(it), next(it), next(it)
    u = _gelu(u_ref[...])
    vn = _ln_rows(_gelu(v_ref[...]), g_ref[...], b_ref[...])
    vn_ref[...] = vn
    vb = vn if precise else vn.astype(BF16)
    for c in range(rows // GMLP_CHUNK):
        r0 = c * GMLP_CHUNK
        for g in range(GMLP_GROUPS):
            c0 = g * GMLP_GROUP_W
            mixed = _mm(_ld(w, g), vb[r0:r0 + GMLP_CHUNK, c0:c0 + GMLP_GROUP_W], precise)
            mixed = mixed + bias_ref[:, c0:c0 + GMLP_GROUP_W]
            ya_ref[r0:r0 + GMLP_CHUNK, c0:c0 + GMLP_GROUP_W] = (
                u[r0:r0 + GMLP_CHUNK, c0:c0 + GMLP_GROUP_W] * mixed).astype(ya_ref.dtype)


def _gmlp_call(proj, ln_g, ln_b, w_eff, bias_eff, rows, precise):
    n = proj.shape[0]
    ublk, vblk = OFF_U // D_AV, OFF_V // D_AV
    vec = pl.BlockSpec((1, D_AV), lambda i: (0, 0))
    ws = list(_split2(w_eff)) if precise else [w_eff.astype(BF16)]
    wspec = pl.BlockSpec((GMLP_GROUPS, GMLP_CHUNK, GMLP_CHUNK), lambda i: (0, 0, 0))
    return pl.pallas_call(
        functools.partial(_gmlp_kernel, rows=rows, precise=precise), grid=(n // rows,),
        in_specs=[pl.BlockSpec((rows, D_AV), lambda i: (i, ublk)), pl.BlockSpec((rows, D_AV), lambda i: (i, vblk)),
                  vec, vec] + [wspec] * len(ws) + [pl.BlockSpec((GMLP_CHUNK, D_AV), lambda i: (0, 0))],
        out_specs=[pl.BlockSpec((rows, D_AV), lambda i: (i, 0)), pl.BlockSpec((rows, D_AV), lambda i: (i, 0))],
        out_shape=[jax.ShapeDtypeStruct((n, D_AV), _act_dtype(precise)), jax.ShapeDtypeStruct((n, D_AV), F32)],
        compiler_params=_cparams(("parallel",)), name="gmlp")(
            proj, proj, ln_g.reshape(1, D_AV), ln_b.reshape(1, D_AV), *ws, bias_eff)


def _gmlp_weights(w_s, b_s, t):
    l = min(GMLP_CHUNK, t)
    pos = np.arange(l)
    mask = (pos[None, :] // CHUNK) <= (pos[:, None] // CHUNK)
    w = jnp.where(mask[None], w_s[:, :l, :l], 0.0)
    bias = jnp.transpose(b_s[:, :l])
    reps = GMLP_CHUNK // l
    if reps > 1:
        eye = jnp.eye(reps, dtype=w.dtype)
        w = jnp.einsum("ab,gij->gaibj", eye, w).reshape(GMLP_GROUPS, GMLP_CHUNK, GMLP_CHUNK)
        bias = jnp.tile(bias, (reps, 1))
    return w, jnp.repeat(bias, GMLP_GROUP_W, axis=1)


def _logsig_kernel(a_ref, b_ref, o_ref):
    o_ref[...] = (-_softplus(-(a_ref[...] + b_ref[...])))[:, :FOX_HEADS]


def _logsig_call(small, bias_row, tm):
    n = small.shape[0]
    return pl.pallas_call(
        _logsig_kernel, grid=(n // tm,),
        in_specs=[pl.BlockSpec((tm, LANES), lambda i: (i, 0)), pl.BlockSpec((1, LANES), lambda i: (0, 0))],
        out_specs=pl.BlockSpec((tm, FOX_HEADS), lambda i: (i, 0)),
        out_shape=jax.ShapeDtypeStruct((n, FOX_HEADS), F32),
        compiler_params=_cparams(("parallel",)), name="fox_logf")(small, bias_row)


def _qvt_kernel(*refs, precise):
    it = iter(refs)
    x_ref = next(it)
    wq, wv = _take_w(it, precise), _take_w(it, precise)
    qt_ref, vt_ref = next(it), next(it)
    x = x_ref[...]
    qt_ref[...] = (_mm(_ld(wq), x, precise, "nt") * (FOX_HD ** -0.5 * LOG2E)).astype(qt_ref.dtype)
    vt_ref[...] = _mm(_ld(wv), x, precise, "nt").astype(vt_ref.dtype)


def _qvt_call(x, lw, tm, bsz, t, precise):
    n = x.shape[0]
    per_b = t // tm
    act = _act_dtype(precise)
    ws = _wargs(lw, "w_qt", precise) + _wargs(lw, "w_vt", precise)
    out = pl.BlockSpec((None, D_B, tm), lambda i: (i // per_b, 0, i % per_b))
    return pl.pallas_call(
        functools.partial(_qvt_kernel, precise=precise), grid=(n // tm,),
        in_specs=[pl.BlockSpec((tm, D_MODEL), lambda i: (i, 0))]
        + [pl.BlockSpec((D_B, D_MODEL), lambda i: (0, 0))] * len(ws),
        out_specs=[out, out],
        out_shape=[jax.ShapeDtypeStruct((bsz, D_B, t), act), jax.ShapeDtypeStruct((bsz, D_B, t), act)],
        compiler_params=_cparams(("parallel",)), name="fox_qv_t")(x, *ws)


def _place(x, sel, precise):
    if not precise:
        return _dot(x.astype(BF16), sel)
    hi, mid, lo = _split3(x)
    return (_dot(hi, sel) + _dot(mid, sel)) + _dot(lo, sel)


def _keys_kernel(k_ref, lf_ref, tril_ref, selk_ref, selg_ref, o_ref, carry_ref, *, precise):
    @pl.when(pl.program_id(1) == 0)
    def _():
        carry_ref[...] = jnp.zeros_like(carry_ref)

    tril = tril_ref[...]
    hi, mid, lo = _split3(lf_ref[...])
    f = (_dot(tril, hi) + _dot(tril, mid)) + _dot(tril, lo) + carry_ref[0:1, :]
    tc = f.shape[0]
    carry_ref[...] = jnp.broadcast_to(f[tc - 1:tc, :], carry_ref.shape)
    g = f * (-LOG2E)
    g1 = _bf16_part(g)
    r = g - g1
    g2 = _bf16_part(r)
    ka = _place(k_ref[...], selk_ref[...], precise) + _place(g1, selg_ref[0], precise)
    ka = ka + _place(g2, selg_ref[1], precise) + _place(r - g2, selg_ref[2], precise)
    o_ref[...] = ka.astype(o_ref.dtype)


def _keys_call(k_all, logf_all, tc, precise):
    bsz, t_k, _ = k_all.shape
    sel_k = np.zeros((D_B, FOX_HEADS * FOX_DK), np.float32)
    sel_g = np.zeros((FOX_NBIAS, FOX_HEADS, FOX_HEADS * FOX_DK), np.float32)
    for h in range(FOX_HEADS):
        sel_k[h * FOX_HD + np.arange(FOX_HD), h * FOX_DK + np.arange(FOX_HD)] = 1.0
        for j in range(FOX_NBIAS):
            sel_g[j, h, h * FOX_DK + FOX_HD + j] = 1.0
    tril = jnp.tril(jnp.ones((tc, tc), F32)).astype(BF16)
    return pl.pallas_call(
        functools.partial(_keys_kernel, precise=precise), grid=(bsz, t_k // tc),
        in_specs=[pl.BlockSpec((None, tc, D_B), lambda b, j: (b, j, 0)),
                  pl.BlockSpec((None, tc, FOX_HEADS), lambda b, j: (b, j, 0)),
                  pl.BlockSpec((tc, tc), lambda b, j: (0, 0)),
                  pl.BlockSpec(sel_k.shape, lambda b, j: (0, 0)), pl.BlockSpec(sel_g.shape, lambda b, j: (0, 0, 0))],
        out_specs=pl.BlockSpec((None, tc, FOX_HEADS * FOX_DK), lambda b, j: (b, j, 0)),
        out_shape=jax.ShapeDtypeStruct((bsz, t_k, FOX_HEADS * FOX_DK), _act_dtype(precise)),
        scratch_shapes=[pltpu.VMEM((8, FOX_HEADS), F32)],
        compiler_params=_cparams(("parallel", "arbitrary")), name="fox_keys")(
            k_all, logf_all, tril, jnp.asarray(sel_k, BF16), jnp.asarray(sel_g, BF16))


def _fox_kernel(qt_ref, ka_ref, vt_ref, o_ref, qa_sc, m_sc, acc_sc, s_sc, mb_sc, *, tq, tk, qs, q_off, precise):
    i = pl.program_id(2)
    act = qa_sc.dtype
    nh = qa_sc.shape[0]
    rows = lax.broadcasted_iota(jnp.int32, (FOX_DK - FOX_HD, tq), 0)
    ones_rows = jnp.where(rows < FOX_NBIAS, 1.0, 0.0).astype(act)
    for h in range(nh):
        qa_sc[h] = jnp.concatenate([qt_ref[h * FOX_HD:(h + 1) * FOX_HD, :], ones_rows], axis=0)
    vrows = lax.broadcasted_iota(jnp.int32, (FOX_DV - FOX_HD, tk), 0)
    v_extra = jnp.where(vrows < 1, 1.0, 0.0).astype(act)
    m_sc[...] = jnp.full_like(m_sc, -jnp.inf)
    acc_sc[...] = jnp.zeros_like(acc_sc)
    first_q = q_off + i * tq
    subs = [(h, c, slice(c * qs, (c + 1) * qs)) for h in range(nh) for c in range(tq // qs)]

    def scores(k0):
        for h, _, sl in subs:
            ka = ka_ref[pl.ds(k0, tk), h * FOX_DK:(h + 1) * FOX_DK]
            s = _mm(ka, qa_sc[h, :, sl], precise)
            s_sc[h, :, sl] = s
            mb_sc[h, :, sl] = jnp.max(s, axis=0, keepdims=True)

    def consume(k0, masked):
        for h, c, sl in subs:
            vt = jnp.concatenate([vt_ref[h * FOX_HD:(h + 1) * FOX_HD, pl.ds(k0, tk)], v_extra], axis=0)
            s = s_sc[h, :, sl]
            if masked:
                kpos = k0 + lax.broadcasted_iota(jnp.int32, (tk, qs), 0)
                qpos = first_q + c * qs + lax.broadcasted_iota(jnp.int32, (tk, qs), 1)
                s = jnp.where(kpos <= qpos, s, NEG_INF)
                mb = jnp.max(s, axis=0, keepdims=True)
            else:
                mb = mb_sc[h, :, sl]
            m_prev = m_sc[h, :, sl]
            m_new = jnp.maximum(m_prev, mb)
            alpha = jnp.exp2(m_prev - m_new)
            p = jnp.exp2(s - m_new)
            acc_sc[h, :, sl] = alpha * acc_sc[h, :, sl] + _mm(vt, p, precise)
            m_sc[h, :, sl] = m_new

    scores(0)
    if ka_ref.shape[0] == tk:
        consume(0, True)
    else:
        assert tq == tk and q_off == 0

        def body(j, c):
            k0 = pl.multiple_of(j * tk, tk)
            consume(k0, False)
            scores(pl.multiple_of(k0 + tk, tk))
            return c

        lax.fori_loop(0, i, body, 0)
        consume(pl.multiple_of(i * tk, tk), True)
    for h in range(nh):
        acc = acc_sc[h]
        o_ref[h * FOX_HD:(h + 1) * FOX_HD, :] = (acc[:FOX_HD] / acc[FOX_HD:FOX_HD + 1]).astype(o_ref.dtype)


def _fox_call(q_t, k_a, v_t, tq, tk, q_off, nh, precise):
    bsz, _, t_q = q_t.shape
    t_k = k_a.shape[1]
    qs = min(FOX_QSUB, tq)
    act = _act_dtype(precise)
    qo = pl.BlockSpec((None, nh * FOX_HD, tq), lambda b, hh, i: (b, hh, i))
    return pl.pallas_call(
        functools.partial(_fox_kernel, tq=tq, tk=tk, qs=qs, q_off=q_off, precise=precise),
        grid=(bsz, FOX_HEADS // nh, t_q // tq),
        in_specs=[qo, pl.BlockSpec((None, t_k, nh * FOX_DK), lambda b, hh, i: (b, 0, hh)),
                  pl.BlockSpec((None, nh * FOX_HD, t_k), lambda b, hh, i: (b, hh, 0))],
        out_specs=qo, out_shape=jax.ShapeDtypeStruct((bsz, D_B, t_q), act),
        scratch_shapes=[pltpu.VMEM((nh, FOX_DK, tq), act), pltpu.VMEM((nh, 1, tq), F32),
                        pltpu.VMEM((nh, FOX_DV, tq), F32), pltpu.VMEM((nh, tk, tq), F32),
                        pltpu.VMEM((nh, 1, tq), F32)],
        compiler_params=_cparams(("parallel", "parallel", "arbitrary")), name="fox_attn")(q_t, k_a, v_t)


def _ssd_kernel(xbc_ref, z0_ref, z1_ref, dtp_ref, dtt_ref, h0_ref, c0_ref, cw_ref, cb_ref, dtb_row_ref, dtb_col_ref,
                alog_row_ref, alog_col_ref, dvec_ref, ng_ref, tril_ref, triu_ref,
                yc_ref, hfin_ref, cnew_ref, h_sc, pad_sc, y_sc, *, L, precise):
    c = pl.program_id(1)

    @pl.when(c == 0)
    def _():
        h_sc[...] = h0_ref[...]
        pad_sc[CONV_PAD - (CONV_W - 1):CONV_PAD, :] = c0_ref[...]

    xin = xbc_ref[...]
    pad_sc[CONV_PAD:CONV_PAD + L, :] = xin
    y = cb_ref[...]
    for j in range(CONV_W - 1):
        r0 = CONV_PAD - (CONV_W - 1) + j
        y = y + pad_sc[r0:r0 + L, :] * cw_ref[j:j + 1, :]
    y = y + xin * cw_ref[CONV_W - 1:CONV_W, :]
    tail = pad_sc[CONV_PAD + L - (CONV_W - 1):CONV_PAD + L, :]
    pad_sc[CONV_PAD - (CONV_W - 1):CONV_PAD, :] = tail
    cnew_ref[...] = tail
    act = y * _sigmoid(y)

    dt_c = _softplus(dtp_ref[...] + dtb_row_ref[...])
    a_c = dt_c * (-jnp.exp(alog_row_ref[...]))
    tril = tril_ref[...]
    hi, mid, lo = _split3(a_c)
    acum_c = (_dot(tril, hi) + _dot(tril, mid)) + _dot(tril, lo)
    dt_r = _softplus(dtt_ref[...] + dtb_col_ref[...])
    a_r = dt_r * (-jnp.exp(alog_col_ref[...]))
    triu = triu_ref[...]
    hi, mid, lo = _split3(a_r)
    acum_r = (_dot(hi, triu) + _dot(mid, triu)) + _dot(lo, triu)

    row = lax.broadcasted_iota(jnp.int32, (L, L), 0)
    col = lax.broadcasted_iota(jnp.int32, (L, L), 1)
    causal = col <= row
    rep = SSM_HEADS // SSM_GROUPS
    for g in range(SSM_GROUPS):
        b0 = D_C + g * SSM_N
        c0 = D_C + SSM_GROUPS * SSM_N + g * SSM_N
        b_g = act[:, b0:b0 + SSM_N]
        c_g = act[:, c0:c0 + SSM_N]
        b_gs = _split2(b_g) if precise else (b_g.astype(BF16), None)
        cb = _mm(c_g, b_gs, precise, "nt")
        for hh in range(rep):
            h = g * rep + hh
            lane = DT_LANE0 + h
            acol = acum_c[:, lane:lane + 1]
            arow = acum_r[h:h + 1, :]
            alast = acum_c[L - 1:L, lane:lane + 1]
            decay = jnp.exp(jnp.where(causal, acol - arow, -jnp.inf))
            m = (cb * decay) * dt_r[h:h + 1, :]
            x_h = act[:, h * SSM_HD:(h + 1) * SSM_HD]
            y_diag = _mm(m, x_h, precise)
            h_prev = h_sc[h]
            y_off = _mm(c_g * jnp.exp(acol), h_prev, precise, "nt")
            wcol = jnp.exp(alast - acol) * dt_c[:, lane:lane + 1]
            s_h = _mm(x_h * wcol, b_gs, precise, "tn")
            h_sc[h] = jnp.exp(alast) * h_prev + s_h
            y_sc[:, h * SSM_HD:(h + 1) * SSM_HD] = (y_diag + y_off) + dvec_ref[:, h * SSM_HD:(h + 1) * SSM_HD] * x_h

    @pl.when(c == pl.num_programs(1) - 1)
    def _():
        hfin_ref[...] = h_sc[...]

    z = jnp.concatenate([z0_ref[...], z1_ref[...]], axis=1)
    yg =y_sc[...] * (z * _sigmoid(z))
    ms = jnp.mean(yg * yg, axis=-1, keepdims=True)
    yc_ref[...] = ((yg * lax.rsqrt(ms + RMS_EPS)) * ng_ref[...]).astype(yc_ref.dtype)


def _ssd_call(proj, small, dt_t, h0, conv0, lw, bsz, t, precise):
    L = min(SSD_L, t)
    nc = t // L
    zhalf = D_C // 2
    zblk = OFF_Z // zhalf

    def const(shape):
        return pl.BlockSpec(shape, lambda b, c: (0,) * len(shape))

    in_specs = [
        pl.BlockSpec((L, CONV_DIM), lambda b, c: (b * nc + c, OFF_XBC // CONV_DIM)),
        pl.BlockSpec((L, zhalf), lambda b, c: (b * nc + c, zblk)),
        pl.BlockSpec((L, zhalf), lambda b, c: (b * nc + c, zblk + 1)),
        pl.BlockSpec((L, LANES), lambda b, c: (b * nc + c, 0)),
        pl.BlockSpec((None, SSM_HEADS, L), lambda b, c: (b, 0, c)),
        pl.BlockSpec((None, SSM_HEADS, SSM_HD, SSM_N), lambda b, c: (b, 0, 0, 0)),
        pl.BlockSpec((None, CONV_W - 1, CONV_DIM), lambda b, c: (b, 0, 0)),
        const((CONV_W, CONV_DIM)), const((1, CONV_DIM)), const((1, LANES)), const((SSM_HEADS, 1)),
        const((1, LANES)), const((SSM_HEADS, 1)), const((1, D_C)), const((1, D_C)), const((L, L)), const((L, L)),
    ]
    out_specs = [
        pl.BlockSpec((L, D_C), lambda b, c: (b * nc + c, 0)),
        pl.BlockSpec((None, SSM_HEADS, SSM_HD, SSM_N), lambda b, c: (b, 0, 0, 0)),
        pl.BlockSpec((None, CONV_W - 1, CONV_DIM), lambda b, c: (b, 0, 0)),
    ]
    out_shape = [jax.ShapeDtypeStruct((bsz * t, D_C), _act_dtype(precise)),
                 jax.ShapeDtypeStruct((bsz, SSM_HEADS, SSM_HD, SSM_N), F32),
                 jax.ShapeDtypeStruct((bsz, CONV_W - 1, CONV_DIM), F32)]
    ones = jnp.ones((L, L), F32)
    return pl.pallas_call(
        functools.partial(_ssd_kernel, L=L, precise=precise), grid=(bsz, nc), in_specs=in_specs,
        out_specs=out_specs, out_shape=out_shape,
        scratch_shapes=[pltpu.VMEM((SSM_HEADS, SSM_HD, SSM_N), F32), pltpu.VMEM((CONV_PAD + L, CONV_DIM), F32),
                        pltpu.VMEM((L, D_C), F32)],
        compiler_params=_cparams(("parallel", "arbitrary")), name="ssd")(
            proj, proj, proj, small, dt_t, h0, conv0, lw["conv_w"], lw["conv_b"], lw["dtb_row"], lw["dtb_col"],
            lw["alog_row"], lw["alog_col"], lw["dvec"], lw["norm_g"],
            jnp.tril(ones).astype(BF16), jnp.triu(ones).astype(BF16))


def _merge_kernel(*refs, precise):
    it = iter(refs)
    ya_ref, ybt_ref, yc_ref, g0_ref, g1_ref, g2_ref, x_ref = [next(it) for _ in range(7)]
    wa, wb, wc, wo = [_take_w(it, precise) for _ in range(4)]
    lg_ref, lb_ref, o_ref, ob_ref = next(it), next(it), next(it), next(it)
    merged = _sigmoid(g0_ref[...]) * _mm(ya_ref[...], _ld(wa), precise)
    merged = merged + _sigmoid(g1_ref[...]) * _mm(ybt_ref[...], _ld(wb), precise, "tn")
    merged = merged + _sigmoid(g2_ref[...]) * _mm(yc_ref[...], _ld(wc), precise)
    mix = _mm(merged, _ld(wo), precise)
    y = _ln_rows(ALPHA * x_ref[...] + mix, lg_ref[...], lb_ref[...])
    o_ref[...] = y
    ob_ref[...] = y.astype(BF16)


def _merge_call(ya, yb_t, yc, proj, x, lw, tm, bsz, t, precise):
    n = x.shape[0]
    gblk = OFF_GATE // D_MODEL
    per_b = t // tm

    def rows(w):
        return pl.BlockSpec((tm, w), lambda i: (i, 0))

    def const(shape):
        return pl.BlockSpec(shape, lambda i: (0, 0))

    gates = [pl.BlockSpec((tm, D_MODEL), lambda i, j=j: (i, gblk + j)) for j in range(N_BRANCH)]
    wspecs, wargs = [], []
    for name, k in (("w_a", D_AV), ("w_b", D_B), ("w_c", D_C), ("w_o", D_MODEL)):
        ws = _wargs(lw, name, precise)
        wargs += ws
        wspecs += [const((k, D_MODEL))] * len(ws)
    return pl.pallas_call(
        functools.partial(_merge_kernel, precise=precise), grid=(n // tm,),
        in_specs=[rows(D_AV), pl.BlockSpec((None, D_B, tm), lambda i: (i // per_b, 0, i % per_b)), rows(D_C)]
        + gates + [rows(D_MODEL)] + wspecs + [const((1, D_MODEL)), const((1, D_MODEL))],
        out_specs=[rows(D_MODEL), rows(D_MODEL)],
        out_shape=[jax.ShapeDtypeStruct((n, D_MODEL), F32), jax.ShapeDtypeStruct((n, D_MODEL), BF16)],
        compiler_params=_cparams(("parallel",)), name="merge")(
            ya, yb_t, yc, proj, proj, proj, x, *wargs, lw["ln1_g"], lw["ln1_b"])


def _router_kernel(h_ref, whi_ref, wlo_ref, br_ref, o_ref, id_ref, w_ref):
    logits = _mm(h_ref[...], (whi_ref[...], wlo_ref[...]), True) + br_ref[...]
    lane = lax.broadcasted_iota(jnp.int32, logits.shape, 1)
    lg = jnp.where(lane < N_EXPERTS, logits, -jnp.inf)
    m1 = jnp.max(lg, axis=-1, keepdims=True)
    i1 = jnp.min(jnp.where(lg == m1, lane, LANES), axis=-1, keepdims=True)
    lg2 = jnp.where(lane == i1, -jnp.inf, lg)
    m2 = jnp.max(lg2, axis=-1, keepdims=True)
    i2 = jnp.min(jnp.where(lg2 == m2, lane, LANES), axis=-1, keepdims=True)
    e = jnp.exp(m2 - m1)
    den = 1.0 + e
    w1, w2 = 1.0 / den, e / den
    o_ref[...] = jnp.where(lane == i1, w1, 0.0) + jnp.where(lane == i2, w2, 0.0)
    id_ref[...] = jnp.where(lane == 0, i1, jnp.where(lane == 1, i2, 0))
    w_ref[...] = jnp.where(lane == 0, w1, jnp.where(lane == 1, w2, 0.0))


def _router_call(h, lw, tm):
    n = h.shape[0]
    out = pl.BlockSpec((tm, LANES), lambda i: (i, 0))
    return pl.pallas_call(
        _router_kernel, grid=(n // tm,),
        in_specs=[pl.BlockSpec((tm, D_MODEL), lambda i: (i, 0)), pl.BlockSpec((D_MODEL, LANES), lambda i: (0, 0)),
                  pl.BlockSpec((D_MODEL, LANES), lambda i: (0, 0)), pl.BlockSpec((1, LANES), lambda i: (0, 0))],
        out_specs=[out, out, out],
        out_shape=[jax.ShapeDtypeStruct((n, LANES), F32), jax.ShapeDtypeStruct((n, LANES), jnp.int32),
                   jax.ShapeDtypeStruct((n, LANES), F32)],
        compiler_params=_cparams(("parallel",)), name="router")(h, lw["wr"], lw["wr_lo"], lw["br"])


MOE_ROWS = 256
MOE_TM = 256
TOP_K = 2


def _moe_plan(ids, n):
    r = MOE_ROWS
    e = ids[:, :TOP_K].reshape(-1)
    na = n * TOP_K
    n_tiles = na // r + N_EXPERTS
    order = jnp.argsort(e, stable=True)
    e_sorted = e[order]
    counts = jnp.sum(e[:, None] == jnp.arange(N_EXPERTS)[None, :], axis=0).astype(jnp.int32)
    padded = ((counts + r - 1) // r) * r
    ends = jnp.cumsum(padded)
    starts = ends - padded
    first = jnp.cumsum(counts) - counts
    dest = starts[e_sorted] + (jnp.arange(na, dtype=jnp.int32) - first[e_sorted])
    src_tok = jnp.zeros((n_tiles * r,), jnp.int32).at[dest].set((order // TOP_K).astype(jnp.int32))
    pos = jnp.zeros((na,), jnp.int32).at[order].set(dest.astype(jnp.int32))
    tile_expert = jnp.minimum(jnp.searchsorted(ends, jnp.arange(n_tiles, dtype=jnp.int32) * r, side="right"),
                              N_EXPERTS - 1).astype(jnp.int32)
    n_used = (ends[-1] // r).astype(jnp.int32).reshape(1)
    return src_tok.reshape(n_tiles, 1, r), pos.reshape(n // MOE_TM, 1, MOE_TM * TOP_K), tile_expert, n_used


def _row_copy(src_hbm, dst_vmem, sem, src_row, dst_row):
    return pltpu.make_async_copy(src_hbm.at[pl.ds(src_row, 1), :], dst_vmem.at[pl.ds(dst_row, 1), :], sem)


def _experts_kernel(te_ref, nu_ref, src_ref, h_hbm, wg_ref, wu_ref, wd_ref, y_ref, x_buf, sem):
    i = pl.program_id(0)
    r = x_buf.shape[0]

    @pl.when(i < nu_ref[0])
    def _():
        def start(k, c):
            _row_copy(h_hbm, x_buf, sem.at[0], src_ref[0, k], k).start()
            return c

        def wait(k, c):
            _row_copy(h_hbm, x_buf, sem.at[0], 0, k).wait()
            return c

        lax.fori_loop(0, r, start, 0, unroll=8)
        lax.fori_loop(0, r, wait, 0, unroll=8)
        xb = x_buf[...].astype(BF16)
        gate = _dot(xb, wg_ref[...])
        up = _dot(xb, wu_ref[...])
        y_ref[...] = _dot(((gate * _sigmoid(gate)) * up).astype(BF16), wd_ref[...])

    @pl.when(i >= nu_ref[0])
    def _():
        y_ref[...] = jnp.zeros_like(y_ref)


def _experts_call(h, src_tok, tile_expert, n_used, lw):
    n_tiles, _, r = src_tok.shape
    wg_spec = pl.BlockSpec((None, D_MODEL, D_FF_EXPERT), lambda i, te, nu: (te[i], 0, 0))
    wd_spec = pl.BlockSpec((None, D_FF_EXPERT, D_MODEL), lambda i, te, nu: (te[i], 0, 0))
    grid_spec = pltpu.PrefetchScalarGridSpec(
        num_scalar_prefetch=2, grid=(n_tiles,),
        in_specs=[pl.BlockSpec((None, 1, r), lambda i, te, nu: (i, 0, 0), memory_space=pltpu.SMEM),
                  pl.BlockSpec(memory_space=pl.ANY), wg_spec, wg_spec, wd_spec],
        out_specs=pl.BlockSpec((r, D_MODEL), lambda i, te, nu: (i, 0)),
        scratch_shapes=[pltpu.VMEM((r, D_MODEL), F32), pltpu.SemaphoreType.DMA((1,))])
    return pl.pallas_call(
        _experts_kernel, grid_spec=grid_spec,
        out_shape=jax.ShapeDtypeStruct((n_tiles * r, D_MODEL), F32),
        compiler_params=_cparams(("arbitrary",)), name="moe_experts")(
            tile_expert, n_used, src_tok, h, lw["w_gate"], lw["w_up"], lw["w_down"])


def _combine_kernel(pos_ref, hb_ref, h_ref, p_ref, w_ref, y_hbm, wp_ref, wpg_ref, lg_ref, lb_ref,
                    o_ref, ob_ref, y_buf, sem):
    tm = h_ref.shape[0]

    def start(k, c):
        for j in range(TOP_K):
            _row_copy(y_hbm, y_buf.at[j], sem.at[0], pos_ref[0, TOP_K * k + j], k).start()
        return c

    def wait(k, c):
        for j in range(TOP_K):
            _row_copy(y_hbm, y_buf.at[j], sem.at[0], 0, k).wait()
        return c

    lax.fori_loop(0, tm, start, 0, unroll=8)
    hb = hb_ref[...]
    ple = _dot(p_ref[...].astype(BF16), wp_ref[...]) * _sigmoid(_dot(hb, wpg_ref[...]))
    acc = ALPHA * h_ref[...] + ple
    lax.fori_loop(0, tm, wait, 0, unroll=8)
    w = w_ref[...]
    for j in range(TOP_K):
        acc = acc + w[:, j:j + 1] * y_buf[j]
    y = _ln_rows(acc, lg_ref[...], lb_ref[...])
    o_ref[...] = y
    ob_ref[...] = y.astype(BF16)


def _combine_call(hb, h, p, wsel, pos, y_sorted, lw):
    n = h.shape[0]
    tm = MOE_TM

    def rows(w):
        return pl.BlockSpec((tm, w), lambda i: (i, 0))

    def const(shape):
        return pl.BlockSpec(shape, lambda i: (0, 0))

    return pl.pallas_call(
        _combine_kernel, grid=(n // tm,),
        in_specs=[pl.BlockSpec((None, 1, tm * TOP_K), lambda i: (i, 0, 0), memory_space=pltpu.SMEM),
                  rows(D_MODEL), rows(D_MODEL), rows(PLE_DIM), rows(LANES), pl.BlockSpec(memory_space=pl.ANY),
                  const((PLE_DIM, D_MODEL)), const((D_MODEL, D_MODEL)), const((1, D_MODEL)), const((1, D_MODEL))],
        out_specs=[rows(D_MODEL), rows(D_MODEL)],
        out_shape=[jax.ShapeDtypeStruct((n, D_MODEL), F32), jax.ShapeDtypeStruct((n, D_MODEL), BF16)],
        scratch_shapes=[pltpu.VMEM((TOP_K, tm, D_MODEL), F32), pltpu.SemaphoreType.DMA((1,))],
        compiler_params=_cparams(("arbitrary",)), name="moe_combine")(
            pos, hb, h, p, wsel, y_sorted, lw["w_ple"], lw["w_pleg"], lw["ln2_g"], lw["ln2_b"])


def _ffn_kernel(*refs, weighted, precise):
    it = iter(refs)
    hb_ref, h_ref, p_ref, comb_ref = next(it), next(it), next(it), next(it)
    wg, wu, wd, wp, wpg = [_take_w(it, precise) for _ in range(5)]
    lg_ref, lb_ref, o_ref, ob_ref, acc_sc = next(it), next(it), next(it), next(it), next(it)
    j = pl.program_id(1)
    hb = h_ref[...] if precise else hb_ref[...]

    @pl.when(j == 0)
    def _():
        ple = _mm(p_ref[...], _ld(wp), precise) * _sigmoid(_mm(hb, _ld(wpg), precise))
        acc_sc[...] = ALPHA * h_ref[...] + ple

    gate = _mm(hb, _ld(wg), precise)
    up = _mm(hb, _ld(wu), precise)
    out = _mm((gate * _sigmoid(gate)) * up, _ld(wd), precise)
    if weighted:
        comb = comb_ref[...]
        lane = lax.broadcasted_iota(jnp.int32, comb.shape, 1)
        out = jnp.sum(jnp.where(lane == j, comb, 0.0), axis=-1, keepdims=True) * out
    acc_sc[...] += out

    @pl.when(j == pl.num_programs(1) - 1)
    def _():
        y = _ln_rows(acc_sc[...], lg_ref[...], lb_ref[...])
        o_ref[...] = y
        ob_ref[...] = y.astype(BF16)


def _ffn_call(hb, h, p, comb, lw, tm, precise):
    n = h.shape[0]
    weighted = comb is not None
    if weighted:
        n_e = N_EXPERTS
        wg_spec = pl.BlockSpec((None, D_MODEL, D_FF_EXPERT), lambda i, j: (j, 0, 0))
        wd_spec = pl.BlockSpec((None, D_FF_EXPERT, D_MODEL), lambda i, j: (j, 0, 0))
    else:
        n_e = D_FF // D_FF_EXPERT
        comb = jnp.zeros((n, LANES), F32)
        wg_spec = pl.BlockSpec((D_MODEL, D_FF_EXPERT), lambda i, j: (0, j))
        wd_spec = pl.BlockSpec((D_FF_EXPERT, D_MODEL), lambda i, j: (j, 0))

    def rows(w):
        return pl.BlockSpec((tm, w), lambda i, j: (i, 0))

    def const(shape):
        return pl.BlockSpec(shape, lambda i, j: (0, 0))

    wspecs, wargs = [], []
    for name, spec in (("w_gate", wg_spec), ("w_up", wg_spec), ("w_down", wd_spec),
                       ("w_ple", const((PLE_DIM, D_MODEL))), ("w_pleg", const((D_MODEL, D_MODEL)))):
        ws = _wargs(lw, name, precise)
        wargs += ws
        wspecs += [spec] * len(ws)
    return pl.pallas_call(
        functools.partial(_ffn_kernel, weighted=weighted, precise=precise), grid=(n // tm, n_e),
        in_specs=[rows(D_MODEL), rows(D_MODEL), rows(PLE_DIM), rows(LANES)] + wspecs
        + [const((1, D_MODEL)), const((1, D_MODEL))],
        out_specs=[rows(D_MODEL), rows(D_MODEL)],
        out_shape=[jax.ShapeDtypeStruct((n, D_MODEL), F32), jax.ShapeDtypeStruct((n, D_MODEL), BF16)],
        scratch_shapes=[pltpu.VMEM((tm, D_MODEL), F32)],
        compiler_params=_cparams(("parallel", "arbitrary")), name="ffn")(
            hb, h, p, comb, *wargs, lw["ln2_g"], lw["ln2_b"])


def _layer_weights(i, W):
    w_in = W["w_in"][i]
    sizes = [D_AV, D_AV, D_B, D_B, D_B, FOX_HEADS, D_C, CONV_DIM, SSM_HEADS, N_BRANCH * D_MODEL]
    o = [0] + [int(s) for s in np.cumsum(sizes)]
    seg = [w_in[:, o[k]:o[k + 1]] for k in range(len(sizes))]
    w_u, w_v, w_q, w_k, w_va, w_f, w_z, w_xbc, w_dt, w_gate = seg
    pad = jnp.zeros((D_MODEL, LANES - SSM_HEADS - FOX_HEADS), F32)

    def lane_row(v, lane0=0):
        return jnp.pad(v.astype(F32), (lane0, LANES - lane0 - v.shape[0])).reshape(1, LANES)

    lw = dict(
        fox_b=lane_row(W["fox_b_forget"][i]),
        gmlp_g=W["gmlp_ln_g"][i], gmlp_b=W["gmlp_ln_b"][i], gmlp_w=W["gmlp_w_spatial"][i], gmlp_bs=W["gmlp_b_spatial"][i],
        conv_w=W["ssm_conv_w"][i], conv_b=W["ssm_conv_b"][i].reshape(1, CONV_DIM),
        dtb_row=lane_row(W["ssm_dt_bias"][i], DT_LANE0), dtb_col=W["ssm_dt_bias"][i].reshape(SSM_HEADS, 1),
        alog_row=lane_row(W["ssm_a_log"][i], DT_LANE0), alog_col=W["ssm_a_log"][i].reshape(SSM_HEADS, 1),
        dvec=jnp.repeat(W["ssm_d"][i], SSM_HD).reshape(1, D_C), norm_g=W["ssm_norm_g"][i].reshape(1, D_C),
        ln1_g=W["ln1_g"][i].reshape(1, D_MODEL), ln1_b=W["ln1_b"][i].reshape(1, D_MODEL),
        ln2_g=W["ln2_g"][i].reshape(1, D_MODEL), ln2_b=W["ln2_b"][i].reshape(1, D_MODEL),
    )
    j = i // 2
    mats = dict(
        w_main=jnp.concatenate([w_gate, w_xbc, w_z, w_u, w_v, w_k, w_va], axis=1),
        w_small=jnp.concatenate([w_f, w_dt, pad], axis=1),
        w_qt=jnp.transpose(w_q), w_vt=jnp.transpose(w_va),
        w_a=W["w_branch_a"][i], w_b=W["w_branch_b"][i], w_c=W["w_branch_c"][i], w_o=W["w_out"][i],
        w_ple=W["ple_w_proj"][i], w_pleg=W["ple_w_gate"][i],
    )
    if i % 2 == 0:
        mats.update(w_gate=W["ffn_w_gate"][j], w_up=W["ffn_w_up"][j], w_down=W["ffn_w_down"][j])
    else:
        mats.update(w_gate=W["moe_w_gate"][j], w_up=W["moe_w_up"][j], w_down=W["moe_w_down"][j],
                    wr=jnp.pad(W["moe_w_router"][j], ((0, 0), (0, LANES - N_EXPERTS))))
        lw["br"] = lane_row(W["moe_b_router"][j])
    for name, w in mats.items():
        c = w * np.float32(2.0**16 + 1.0)
        hi_f = c - (c - w)
        lw[name] = hi_f.astype(BF16)
        lw[name + "_lo"] = (w - hi_f).astype(BF16)
    return lw


def _trunk_layer(x, xb, p, i, lw, bsz, t, fox_cache, ssm0, conv0, tm, precise):
    n = bsz * t
    act = _act_dtype(precise)
    xin = x if precise else xb
    tmm = min(tm, t)
    proj = _matmul_call(xin, lw, "w_main", min(n, 2048), MAIN_TN, precise, "in_proj")
    small = _matmul_call(xin, lw, "w_small", tm, LANES, precise, "in_proj_small")

    w_eff, bias_eff = _gmlp_weights(lw["gmlp_w"], lw["gmlp_bs"], t)
    y_a, v_n = _gmlp_call(proj, lw["gmlp_g"], lw["gmlp_b"], w_eff, bias_eff, min(n, 512), precise)

    k_new = proj[:, OFF_K:OFF_K + D_B].reshape(bsz, t, D_B)
    v_new = proj[:, OFF_VA:OFF_VA + D_B].reshape(bsz, t, FOX_HEADS, FOX_HD)
    logf = _logsig_call(small, lw["fox_b"], tm).reshape(bsz, t, FOX_HEADS)
    q_t, v_t = _qvt_call(xin, lw, tmm, bsz, t, precise)
    if fox_cache is None:
        k_all, logf_all, q_off = k_new, logf, 0
    else:
        k_c, v_c, logf_c = fox_cache
        q_off = k_c.shape[1]
        k_all = jnp.concatenate([k_c.reshape(bsz, q_off, D_B), k_new], axis=1)
        v_t = jnp.concatenate([jnp.transpose(v_c.reshape(bsz, q_off, D_B), (0, 2, 1)).astype(act), v_t], axis=2)
        logf_all = jnp.concatenate([logf_c, logf], axis=1)
    t_k = k_all.shape[1]
    k_a = _keys_call(k_all, logf_all, 512 if t_k % 512 == 0 else t_k, precise)
    tq, tk = (512, 512) if t % 512 == 0 else (t, t_k)
    y_bt = _fox_call(q_t, k_a, v_t, tq, tk, q_off, 4, precise)

    dt_t = jnp.transpose(small[:, DT_LANE0:DT_LANE0 + SSM_HEADS].reshape(bsz, t, SSM_HEADS), (0, 2, 1))
    y_c, h_fin, conv_new = _ssd_call(proj, small, dt_t, ssm0, conv0, lw, bsz, t, precise)

    x1, x1b = _merge_call(y_a, y_bt, y_c, proj, x, lw, tmm, bsz, t, precise)
    if i % 2 == 0:
        x2, x2b = _ffn_call(x1b, x1, p, None, lw, tm, precise)
    else:
        comb, ids, wsel = _router_call(x1, lw, tm)
        if precise or n % MOE_TM or n < N_EXPERTS * MOE_ROWS:
            x2, x2b = _ffn_call(x1b, x1, p, comb, lw, tm, precise)
        else:
            src_tok, pos, tile_expert, n_used = _moe_plan(ids, n)
            y_sorted = _experts_call(x1, src_tok, tile_expert, n_used, lw)
            x2, x2b = _combine_call(x1b, x1, p, wsel, pos, y_sorted, lw)
    return x2, x2b, (k_new.reshape(bsz, t, FOX_HEADS, FOX_HD), v_new, logf, h_fin, conv_new,
                     v_n.reshape(bsz, t, D_AV))


def kernel(x_prompt, x_sample, p_prompt, p_sample, cache_fox_k, cache_fox_v, cache_fox_logf, state_ssm, state_conv, ln0_g, ln0_b, w_in, fox_b_forget, gmlp_ln_g, gmlp_ln_b, gmlp_w_spatial, gmlp_b_spatial, ssm_conv_w, ssm_conv_b, ssm_dt_bias, ssm_a_log, ssm_d, ssm_norm_g, w_branch_a, w_branch_b, w_branch_c, w_out, ln1_g, ln1_b, ln2_g, ln2_b, ffn_w_gate, ffn_w_up, ffn_w_down, moe_w_router, moe_b_router, moe_w_gate, moe_w_up, moe_w_down, ple_w_proj, ple_w_gate):
    W = dict(w_in=w_in, fox_b_forget=fox_b_forget, gmlp_ln_g=gmlp_ln_g, gmlp_ln_b=gmlp_ln_b,
             gmlp_w_spatial=gmlp_w_spatial, gmlp_b_spatial=gmlp_b_spatial, ssm_conv_w=ssm_conv_w,
             ssm_conv_b=ssm_conv_b, ssm_dt_bias=ssm_dt_bias, ssm_a_log=ssm_a_log, ssm_d=ssm_d,
             ssm_norm_g=ssm_norm_g, w_branch_a=w_branch_a, w_branch_b=w_branch_b,
             w_branch_c=w_branch_c, w_out=w_out, ln1_g=ln1_g, ln1_b=ln1_b, ln2_g=ln2_g, ln2_b=ln2_b,
             ffn_w_gate=ffn_w_gate, ffn_w_up=ffn_w_up, ffn_w_down=ffn_w_down,
             moe_w_router=moe_w_router, moe_b_router=moe_b_router, moe_w_gate=moe_w_gate, moe_w_up=moe_w_up,
             moe_w_down=moe_w_down, ple_w_proj=ple_w_proj, ple_w_gate=ple_w_gate)
    bp, tp, _ = x_prompt.shape
    bs, ts, _ = x_sample.shape
    tm_p = 512 if (bp * tp) % 512 == 0 else bp * tp
    tm_s = bs * ts
    xp, xpb = _layer_norm_call(x_prompt.reshape(bp * tp, D_MODEL), ln0_g, ln0_b, tm_p)
    xs, xsb = _layer_norm_call(x_sample.reshape(bs * ts, D_MODEL), ln0_g, ln0_b, tm_s)
    outs_p, outs_s = [], []
    for i in range(DEPTH):
        lw = _layer_weights(i, W)
        ssm0 = jnp.zeros((bp, SSM_HEADS, SSM_HD, SSM_N), F32)
        conv0 = jnp.zeros((bp, CONV_W - 1, CONV_DIM), F32)
        xp, xpb, st_p = _trunk_layer(xp, xpb, p_prompt[i].reshape(bp * tp, PLE_DIM), i, lw, bp, tp,
                                     None, ssm0, conv0, tm_p, False)
        xs, xsb, st_s = _trunk_layer(xs, xsb, p_sample[i].reshape(bs * ts, PLE_DIM), i, lw, bs, ts,
                                     (cache_fox_k[i], cache_fox_v[i], cache_fox_logf[i]),
                                     state_ssm[i], state_conv[i], tm_s, True)
        outs_p.append(st_p)
        outs_s.append(st_s)

    def stack(outs, k):
        return jnp.stack([o[k] for o in outs])

    return (xp.reshape(bp, tp, D_MODEL), xs.reshape(bs, ts, D_MODEL),
            stack(outs_p, 0), stack(outs_p, 1), stack(outs_p, 2), stack(outs_p, 3), stack(outs_p, 4),
            stack(outs_s, 0), stack(outs_s, 1), stack(outs_s, 2), stack(outs_s, 3), stack(outs_s, 4), stack(outs_s, 5))
```

```python
import functools
import math

import numpy as np
import jax
import jax.numpy as jnp
from jax import lax
from jax.experimental import pallas as pl
from jax.experimental.pallas import tpu as pltpu

F32 = jnp.float32
BF16 = jnp.bfloat16

D_MODEL = 1024
DEPTH = 2
CHUNK = 64
PLE_DIM = 256
GMLP_CHUNK = 128
GMLP_GROUPS = 4
D_AV = D_MODEL // 2
GMLP_GROUP_W = D_AV // GMLP_GROUPS
FOX_HD = 64
D_B = D_MODEL // 2
FOX_HEADS = D_B // FOX_HD
D_C = D_MODEL
SSM_HD = 64
SSM_HEADS = D_C // SSM_HD
SSM_N = 128
SSM_GROUPS = 2
CONV_W = 4
CONV_DIM = D_C + 2 * SSM_GROUPS * SSM_N
N_BRANCH = 3
D_FF = 11 * D_MODEL // 4
N_EXPERTS = 8
D_FF_EXPERT = D_FF // 2
ALPHA = (2.0 * DEPTH) ** 0.25
LN_EPS = 1e-5
RMS_EPS = 1e-5
NEG_INF = -1e30
LOG2E = math.log2(math.e)

LANES = 128
VMEM_LIMIT = 56 * 2**20
N_MAIN = N_BRANCH * D_MODEL + CONV_DIM + D_C + 2 * D_AV + 2 * D_B
OFF_GATE, OFF_XBC, OFF_Z, OFF_U, OFF_V, OFF_K, OFF_VA = 0, 3072, 4608, 5632, 6144, 6656, 7168
MAIN_TN = 768
DT_LANE0 = FOX_HEADS
SSD_L = 128
CONV_PAD = 8
FOX_NBIAS = 3
FOX_DK = 128
FOX_DV = 80
FOX_QSUB = 256


def _cparams(sem):
    return pltpu.CompilerParams(dimension_semantics=sem, vmem_limit_bytes=VMEM_LIMIT)


def _sigmoid(x):
    return 1.0 / (1.0 + jnp.exp(-x))


def _softplus(x):
    return jnp.maximum(x, 0.0) + jnp.log1p(jnp.exp(-jnp.abs(x)))


def _gelu(x):
    c = np.float32(np.sqrt(2.0 / np.pi))
    return x * (0.5 * (1.0 + jnp.tanh(c * (x + 0.044715 * (x * x * x)))))


def _ln_rows(x, g, b):
    mu = jnp.mean(x, axis=-1, keepdims=True)
    xc = x - mu
    var = jnp.mean(xc * xc, axis=-1, keepdims=True)
    return xc * lax.rsqrt(var + LN_EPS) * g + b


def _bf16_part(x):
    u = lax.bitcast_convert_type(x, jnp.uint32) & jnp.uint32(0xFFFF0000)
    return lax.bitcast_convert_type(u, F32)


def _split2(x):
    hi = _bf16_part(x)
    return hi.astype(BF16), (x - hi).astype(BF16)


def _split3(x):
    hi = _bf16_part(x)
    r1 = x - hi
    mid = _bf16_part(r1)
    return hi.astype(BF16), mid.astype(BF16), (r1 - mid).astype(BF16)


def _dot(a, b):
    return jnp.dot(a, b, preferred_element_type=F32)


def _dot_nt(a, b):
    return lax.dot_general(a, b, (((1,), (1,)), ((), ())), preferred_element_type=F32)


def _dot_tn(a, b):
    return lax.dot_general(a, b, (((0,), (0,)), ((), ())), preferred_element_type=F32)


_DOTS = {"nn": _dot, "nt": _dot_nt, "tn": _dot_tn}


def _mm(a, b, precise, dims="nn"):
    dot = _DOTS[dims]
    if not precise:
        a = a[0] if isinstance(a, tuple) else a.astype(BF16)
        b = b[0] if isinstance(b, tuple) else b.astype(BF16)
        return dot(a, b)
    ah, al = a if isinstance(a, tuple) else _split2(a)
    bh, bl = b if isinstance(b, tuple) else _split2(b)
    return (dot(ah, bh) + dot(al, bh)) + dot(ah, bl)


def _take_w(it, precise):
    hi = next(it)
    return hi, (next(it) if precise else None)


def _ld(w, idx=...):
    return w[0][idx], (None if w[1] is None else w[1][idx])


def _wargs(lw, name, precise):
    return [lw[name], lw[name + "_lo"]] if precise else [lw[name]]


def _act_dtype(precise):
    return F32 if precise else BF16


def _ln_kernel(x_ref, g_ref, b_ref, o_ref, ob_ref):
    y = _ln_rows(x_ref[...], g_ref[...], b_ref[...])
    o_ref[...] = y
    ob_ref[...] = y.astype(BF16)


def _layer_norm_call(x, g, b, tm):
    n, d = x.shape
    row = pl.BlockSpec((tm, d), lambda i: (i, 0))
    vec = pl.BlockSpec((1, d), lambda i: (0, 0))
    return pl.pallas_call(
        _ln_kernel, grid=(n // tm,), in_specs=[row, vec, vec], out_specs=[row, row],
        out_shape=[jax.ShapeDtypeStruct((n, d), F32), jax.ShapeDtypeStruct((n, d), BF16)],
        compiler_params=_cparams(("parallel",)), name="ln_in")(x, g.reshape(1, d), b.reshape(1, d))


def _mm_kernel(*refs, precise):
    it = iter(refs)
    x_ref = next(it)
    w = _take_w(it, precise)
    o_ref = next(it)
    o_ref[...] = _mm(x_ref[...], _ld(w), precise)


def _matmul_call(x, lw, wname, tm, tn, precise, name):
    m, k = x.shape
    ws = _wargs(lw, wname, precise)
    n = ws[0].shape[1]
    wspec = pl.BlockSpec((k, tn), lambda i, j: (0, j))
    return pl.pallas_call(
        functools.partial(_mm_kernel, precise=precise), grid=(m // tm, n // tn),
        in_specs=[pl.BlockSpec((tm, k), lambda i, j: (i, 0))] + [wspec] * len(ws),
        out_specs=pl.BlockSpec((tm, tn), lambda i, j: (i, j)),
        out_shape=jax.ShapeDtypeStruct((m, n), F32),
        compiler_params=_cparams(("parallel", "parallel")), name=name)(x, *ws)


def _gmlp_kernel(*refs, rows, precise):
    it = iter(refs)
    u_ref, v_ref, g_ref, b_ref = next(it), next(it), next(it), next(it)
    w = _take_w(it, precise)
    bias_ref, ya_ref, vn_ref = next(it), next(it), next(it)
    u = _gelu(u_ref[...])
    vn = _ln_rows(_gelu(v_ref[...]), g_ref[...], b_ref[...])
    vn_ref[...] = vn
    vb = vn if precise else vn.astype(BF16)
    for c in range(rows // GMLP_CHUNK):
        r0 = c * GMLP_CHUNK
        for g in range(GMLP_GROUPS):
            c0 = g * GMLP_GROUP_W
            mixed = _mm(_ld(w, g), vb[r0:r0 + GMLP_CHUNK, c0:c0 + GMLP_GROUP_W], precise)
            mixed = mixed + bias_ref[:, c0:c0 + GMLP_GROUP_W]
            ya_ref[r0:r0 + GMLP_CHUNK, c0:c0 + GMLP_GROUP_W] = (
                u[r0:r0 + GMLP_CHUNK, c0:c0 + GMLP_GROUP_W] * mixed).astype(ya_ref.dtype)


def _gmlp_call(proj, ln_g, ln_b, w_eff, bias_eff, rows, precise):
    n = proj.shape[0]
    ublk, vblk = OFF_U // D_AV, OFF_V // D_AV
    vec = pl.BlockSpec((1, D_AV), lambda i: (0, 0))
    ws = list(_split2(w_eff)) if precise else [w_eff.astype(BF16)]
    wspec = pl.BlockSpec((GMLP_GROUPS, GMLP_CHUNK, GMLP_CHUNK), lambda i: (0, 0, 0))
    return pl.pallas_call(
        functools.partial(_gmlp_kernel, rows=rows, precise=precise), grid=(n // rows,),
        in_specs=[pl.BlockSpec((rows, D_AV), lambda i: (i, ublk)), pl.BlockSpec((rows, D_AV), lambda i: (i, vblk)),
                  vec, vec] + [wspec] * len(ws) + [pl.BlockSpec((GMLP_CHUNK, D_AV), lambda i: (0, 0))],
        out_specs=[pl.BlockSpec((rows, D_AV), lambda i: (i, 0)), pl.BlockSpec((rows, D_AV), lambda i: (i, 0))],
        out_shape=[jax.ShapeDtypeStruct((n, D_AV), _act_dtype(precise)), jax.ShapeDtypeStruct((n, D_AV), F32)],
        compiler_params=_cparams(("parallel",)), name="gmlp")(
            proj, proj, ln_g.reshape(1, D_AV), ln_b.reshape(1, D_AV), *ws, bias_eff)


def _gmlp_weights(w_s, b_s, t):
    l = min(GMLP_CHUNK, t)
    pos = np.arange(l)
    mask = (pos[None, :] // CHUNK) <= (pos[:, None] // CHUNK)
    w = jnp.where(mask[None], w_s[:, :l, :l], 0.0)
    bias = jnp.transpose(b_s[:, :l])
    reps = GMLP_CHUNK // l
    if reps > 1:
        eye = jnp.eye(reps, dtype=w.dtype)
        w = jnp.einsum("ab,gij->gaibj", eye, w).reshape(GMLP_GROUPS, GMLP_CHUNK, GMLP_CHUNK)
        bias = jnp.tile(bias, (reps, 1))
    return w, jnp.repeat(bias, GMLP_GROUP_W, axis=1)


def _logsig_kernel(a_ref, b_ref, o_ref):
    o_ref[...] = (-_softplus(-(a_ref[...] + b_ref[...])))[:, :FOX_HEADS]


def _logsig_call(small, bias_row, tm):
    n = small.shape[0]
    return pl.pallas_call(
        _logsig_kernel, grid=(n // tm,),
        in_specs=[pl.BlockSpec((tm, LANES), lambda i: (i, 0)), pl.BlockSpec((1, LANES), lambda i: (0, 0))],
        out_specs=pl.BlockSpec((tm, FOX_HEADS), lambda i: (i, 0)),
        out_shape=jax.ShapeDtypeStruct((n, FOX_HEADS), F32),
        compiler_params=_cparams(("parallel",)), name="fox_logf")(small, bias_row)


def _qvt_kernel(*refs, precise):
    it = iter(refs)
    x_ref = next(it)
    wq, wv = _take_w(it, precise), _take_w(it, precise)
    qt_ref, vt_ref = next(it), next(it)
    x = x_ref[...]
    qt_ref[...] = (_mm(_ld(wq), x, precise, "nt") * (FOX_HD ** -0.5 * LOG2E)).astype(qt_ref.dtype)
    vt_ref[...] = _mm(_ld(wv), x, precise, "nt").astype(vt_ref.dtype)


def _qvt_call(x, lw, tm, bsz, t, precise):
    n = x.shape[0]
    per_b = t // tm
    act = _act_dtype(precise)
    ws = _wargs(lw, "w_qt", precise) + _wargs(lw, "w_vt", precise)
    out = pl.BlockSpec((None, D_B, tm), lambda i: (i // per_b, 0, i % per_b))
    return pl.pallas_call(
        functools.partial(_qvt_kernel, precise=precise), grid=(n // tm,),
        in_specs=[pl.BlockSpec((tm, D_MODEL), lambda i: (i, 0))]
        + [pl.BlockSpec((D_B, D_MODEL), lambda i: (0, 0))] * len(ws),
        out_specs=[out, out],
        out_shape=[jax.ShapeDtypeStruct((bsz, D_B, t), act), jax.ShapeDtypeStruct((bsz, D_B, t), act)],
        compiler_params=_cparams(("parallel",)), name="fox_qv_t")(x, *ws)


def _place(x, sel, precise):
    if not precise:
        return _dot(x.astype(BF16), sel)
    hi, mid, lo = _split3(x)
    return (_dot(hi, sel) + _dot(mid, sel)) + _dot(lo, sel)


def _keys_kernel(k_ref, lf_ref, tril_ref, selk_ref, selg_ref, o_ref, carry_ref, *, precise):
    @pl.when(pl.program_id(1) == 0)
    def _():
        carry_ref[...] = jnp.zeros_like(carry_ref)

    tril = tril_ref[...]
    hi, mid, lo = _split3(lf_ref[...])
    f = (_dot(tril, hi) + _dot(tril, mid)) + _dot(tril, lo) + carry_ref[0:1, :]
    tc = f.shape[0]
    carry_ref[...] = jnp.broadcast_to(f[tc - 1:tc, :], carry_ref.shape)
    g = f * (-LOG2E)
    g1 = _bf16_part(g)
    r = g - g1
    g2 = _bf16_part(r)
    ka = _place(k_ref[...], selk_ref[...], precise) + _place(g1, selg_ref[0], precise)
    ka = ka + _place(g2, selg_ref[1], precise) + _place(r - g2, selg_ref[2], precise)
    o_ref[...] = ka.astype(o_ref.dtype)


def _keys_call(k_all, logf_all, tc, precise):
    bsz, t_k, _ = k_all.shape
    sel_k = np.zeros((D_B, FOX_HEADS * FOX_DK), np.float32)
    sel_g = np.zeros((FOX_NBIAS, FOX_HEADS, FOX_HEADS * FOX_DK), np.float32)
    for h in range(FOX_HEADS):
        sel_k[h * FOX_HD + np.arange(FOX_HD), h * FOX_DK + np.arange(FOX_HD)] = 1.0
        for j in range(FOX_NBIAS):
            sel_g[j, h, h * FOX_DK + FOX_HD + j] = 1.0
    tril = jnp.tril(jnp.ones((tc, tc), F32)).astype(BF16)
    return pl.pallas_call(
        functools.partial(_keys_kernel, precise=precise), grid=(bsz, t_k // tc),
        in_specs=[pl.BlockSpec((None, tc, D_B), lambda b, j: (b, j, 0)),
                  pl.BlockSpec((None, tc, FOX_HEADS), lambda b, j: (b, j, 0)),
                  pl.BlockSpec((tc, tc), lambda b, j: (0, 0)),
                  pl.BlockSpec(sel_k.shape, lambda b, j: (0, 0)), pl.BlockSpec(sel_g.shape, lambda b, j: (0, 0, 0))],
        out_specs=pl.BlockSpec((None, tc, FOX_HEADS * FOX_DK), lambda b, j: (b, j, 0)),
        out_shape=jax.ShapeDtypeStruct((bsz, t_k, FOX_HEADS * FOX_DK), _act_dtype(precise)),
        scratch_shapes=[pltpu.VMEM((8, FOX_HEADS), F32)],
        compiler_params=_cparams(("parallel", "arbitrary")), name="fox_keys")(
            k_all, logf_all, tril, jnp.asarray(sel_k, BF16), jnp.asarray(sel_g, BF16))


def _fox_kernel(qt_ref, ka_ref, vt_ref, o_ref, qa_sc, m_sc, acc_sc, s_sc, mb_sc, *, tq, tk, qs, q_off, precise):
    i = pl.program_id(2)
    act = qa_sc.dtype
    nh = qa_sc.shape[0]
    rows = lax.broadcasted_iota(jnp.int32, (FOX_DK - FOX_HD, tq), 0)
    ones_rows = jnp.where(rows < FOX_NBIAS, 1.0, 0.0).astype(act)
    for h in range(nh):
        qa_sc[h] = jnp.concatenate([qt_ref[h * FOX_HD:(h + 1) * FOX_HD, :], ones_rows], axis=0)
    vrows = lax.broadcasted_iota(jnp.int32, (FOX_DV - FOX_HD, tk), 0)
    v_extra = jnp.where(vrows < 1, 1.0, 0.0).astype(act)
    m_sc[...] = jnp.full_like(m_sc, -jnp.inf)
    acc_sc[...] = jnp.zeros_like(acc_sc)
    first_q = q_off + i * tq
    subs = [(h, c, slice(c * qs, (c + 1) * qs)) for h in range(nh) for c in range(tq // qs)]

    def scores(k0):
        for h, _, sl in subs:
            ka = ka_ref[pl.ds(k0, tk), h * FOX_DK:(h + 1) * FOX_DK]
            s = _mm(ka, qa_sc[h, :, sl], precise)
            s_sc[h, :, sl] = s
            mb_sc[h, :, sl] = jnp.max(s, axis=0, keepdims=True)

    def consume(k0, masked):
        for h, c, sl in subs:
            vt = jnp.concatenate([vt_ref[h * FOX_HD:(h + 1) * FOX_HD, pl.ds(k0, tk)], v_extra], axis=0)
            s = s_sc[h, :, sl]
            if masked:
                kpos = k0 + lax.broadcasted_iota(jnp.int32, (tk, qs), 0)
                qpos = first_q + c * qs + lax.broadcasted_iota(jnp.int32, (tk, qs), 1)
                s = jnp.where(kpos <= qpos, s, NEG_INF)
                mb = jnp.max(s, axis=0, keepdims=True)
            else:
                mb = mb_sc[h, :, sl]
            m_prev = m_sc[h, :, sl]
            m_new = jnp.maximum(m_prev, mb)
            alpha = jnp.exp2(m_prev - m_new)
            p = jnp.exp2(s - m_new)
            acc_sc[h, :, sl] = alpha * acc_sc[h, :, sl] + _mm(vt, p, precise)
            m_sc[h, :, sl] = m_new

    scores(0)
    if ka_ref.shape[0] == tk:
        consume(0, True)
    else:
        assert tq == tk and q_off == 0

        def body(j, c):
            k0 = pl.multiple_of(j * tk, tk)
            consume(k0, False)
            scores(pl.multiple_of(k0 + tk, tk))
            return c

        lax.fori_loop(0, i, body, 0)
        consume(pl.multiple_of(i * tk, tk), True)
    for h in range(nh):
        acc = acc_sc[h]
        o_ref[h * FOX_HD:(h + 1) * FOX_HD, :] = (acc[:FOX_HD] / acc[FOX_HD:FOX_HD + 1]).astype(o_ref.dtype)


def _fox_call(q_t, k_a, v_t, tq, tk, q_off, nh, precise):
    bsz, _, t_q = q_t.shape
    t_k = k_a.shape[1]
    qs = min(FOX_QSUB, tq)
    act = _act_dtype(precise)
    qo = pl.BlockSpec((None, nh * FOX_HD, tq), lambda b, hh, i: (b, hh, i))
    return pl.pallas_call(
        functools.partial(_fox_kernel, tq=tq, tk=tk, qs=qs, q_off=q_off, precise=precise),
        grid=(bsz, FOX_HEADS // nh, t_q // tq),
        in_specs=[qo, pl.BlockSpec((None, t_k, nh * FOX_DK), lambda b, hh, i: (b, 0, hh)),
                  pl.BlockSpec((None, nh * FOX_HD, t_k), lambda b, hh, i: (b, hh, 0))],
        out_specs=qo, out_shape=jax.ShapeDtypeStruct((bsz, D_B, t_q), act),
        scratch_shapes=[pltpu.VMEM((nh, FOX_DK, tq), act), pltpu.VMEM((nh, 1, tq), F32),
                        pltpu.VMEM((nh, FOX_DV, tq), F32), pltpu.VMEM((nh, tk, tq), F32),
                        pltpu.VMEM((nh, 1, tq), F32)],
        compiler_params=_cparams(("parallel", "parallel", "arbitrary")), name="fox_attn")(q_t, k_a, v_t)


def _ssd_kernel(xbc_ref, z0_ref, z1_ref, dtp_ref, dtt_ref, h0_ref, c0_ref, cw_ref, cb_ref, dtb_row_ref, dtb_col_ref,
                alog_row_ref, alog_col_ref, aloge_ref, dvec_ref, ng_ref, tril_ref, triu_ref, expand_ref,
                yc_ref, hfin_ref, cnew_ref, h_sc, pad_sc, y_sc, *, L, precise):
    c = pl.program_id(1)

    @pl.when(c == 0)
    def _():
        h_sc[...] = h0_ref[...]
        pad_sc[CONV_PAD - (CONV_W - 1):CONV_PAD, :] = c0_ref[...]

    xin = xbc_ref[...]
    pad_sc[CONV_PAD:CONV_PAD + L, :] = xin
    y = cb_ref[...]
    for j in range(CONV_W - 1):
        r0 = CONV_PAD - (CONV_W - 1) + j
        y = y + pad_sc[r0:r0 + L, :] * cw_ref[j:j + 1, :]
    y = y + xin * cw_ref[CONV_W - 1:CONV_W, :]
    tail = pad_sc[CONV_PAD + L - (CONV_W - 1):CONV_PAD + L, :]
    pad_sc[CONV_PAD - (CONV_W - 1):CONV_PAD, :] = tail
    cnew_ref[...] = tail
    act = y * _sigmoid(y)

    dt_c = _softplus(dtp_ref[...] + dtb_row_ref[...])
    a_c = dt_c * (-jnp.exp(alog_row_ref[...]))
    tril = tril_ref[...]
    hi, mid, lo = _split3(a_c)
    acum_c = (_dot(tril, hi) + _dot(tril, mid)) + _dot(tril, lo)
    dt_r = _softplus(dtt_ref[...] + dtb_col_ref[...])
    a_r = dt_r * (-jnp.exp(alog_col_ref[...]))
    triu = triu_ref[...]
    hi, mid, lo = _split3(a_r)
    acum_r = (_dot(hi, triu) + _dot(mid, triu)) + _dot(lo, triu)

    dt_e = _place(dt_c, expand_ref[...], True)
    hi, mid, lo = _split3(dt_e * (-jnp.exp(aloge_ref[...])))
    acum_e = (_dot(tril, hi) + _dot(tril, mid)) + _dot(tril, lo)
    xs = act[:, :D_C]
    e_in = jnp.exp(acum_e)
    xw = xs * (jnp.exp(acum_e[L - 1:L, :] - acum_e) * dt_e)
    dx = dvec_ref[...] * xs

    row = lax.broadcasted_iota(jnp.int32, (L, L), 0)
    col = lax.broadcasted_iota(jnp.int32, (L, L), 1)
    causal = col <= row
    first_of_pair = lax.broadcasted_iota(jnp.int32, (L, 2 * SSM_HD), 1) < SSM_HD
    rep = SSM_HEADS // SSM_GROUPS
    gw = rep * SSM_HD
    for g in range(SSM_GROUPS):
        b0 = D_C + g * SSM_N
        c0 = D_C + SSM_GROUPS * SSM_N + g * SSM_N
        b_g = act[:, b0:b0 + SSM_N]
        c_g = act[:, c0:c0 + SSM_N]
        b_gs = _split2(b_g) if precise else (b_g.astype(BF16), None)
        cb = _mm(c_g, b_gs, precise, "nt")
        h_prev = h_sc[g * rep:(g + 1) * rep].reshape(gw, SSM_N)
        y_off = _mm(c_g, h_prev, precise, "nt") * e_in[:, g * gw:(g + 1) * gw]
        s_g = _mm(xw[:, g * gw:(g + 1) * gw], b_gs, precise, "tn")
        for hh in range(rep):
            h = g * rep + hh
            alast = acum_c[L - 1:L, DT_LANE0 + h:DT_LANE0 + h + 1]
            h_sc[h] = jnp.exp(alast) * h_sc[h] + s_g[hh * SSM_HD:(hh + 1) * SSM_HD, :]
        for pr in range(rep // 2):
            ch0 = g * gw + pr * 2 * SSM_HD
            x_pair = xs[:, ch0:ch0 + 2 * SSM_HD]
            x_pair = _split2(x_pair) if precise else (x_pair.astype(BF16), None)
            y_pair = []
            for hh in (2 * pr, 2 * pr + 1):
                h = g * rep + hh
                lane = DT_LANE0 + h
                decay = jnp.exp(jnp.where(causal, acum_c[:, lane:lane + 1] - acum_r[h:h + 1, :], -jnp.inf))
                y_pair.append(_mm((cb * decay) * dt_r[h:h + 1, :], x_pair, precise))
            y_diag = jnp.where(first_of_pair, y_pair[0], y_pair[1])
            y_sc[:, ch0:ch0 + 2 * SSM_HD] = (y_diag + y_off[:, ch0 - g * gw:ch0 - g * gw + 2 * SSM_HD]) + dx[:, ch0:ch0 + 2 * SSM_HD]

    @pl.when(c == pl.num_programs(1) - 1)
    def _():
        hfin_ref[...] = h_sc[...]

    z = jnp.concatenate([z0_ref[...], z1_ref[...]], axis=1)
    yg =y_sc[...] * (z * _sigmoid(z))
    ms = jnp.mean(yg * yg, axis=-1, keepdims=True)
    yc_ref[...] = ((yg * lax.rsqrt(ms + RMS_EPS)) * ng_ref[...]).astype(yc_ref.dtype)


def _ssd_call(proj, small, dt_t, h0, conv0, lw, bsz, t, precise):
    L = min(SSD_L, t)
    nc = t // L
    zhalf = D_C // 2
    zblk = OFF_Z // zhalf

    def const(shape):
        return pl.BlockSpec(shape, lambda b, c: (0,) * len(shape))

    in_specs = [
        pl.BlockSpec((L, CONV_DIM), lambda b, c: (b * nc + c, OFF_XBC // CONV_DIM)),
        pl.BlockSpec((L, zhalf), lambda b, c: (b * nc + c, zblk)),
        pl.BlockSpec((L, zhalf), lambda b, c: (b * nc + c, zblk + 1)),
        pl.BlockSpec((L, LANES), lambda b, c: (b * nc + c, 0)),
        pl.BlockSpec((None, SSM_HEADS, L), lambda b, c: (b, 0, c)),
        pl.BlockSpec((None, SSM_HEADS, SSM_HD, SSM_N), lambda b, c: (b, 0, 0, 0)),
        pl.BlockSpec((None, CONV_W - 1, CONV_DIM), lambda b, c: (b, 0, 0)),
        const((CONV_W, CONV_DIM)), const((1, CONV_DIM)), const((1, LANES)), const((SSM_HEADS, 1)),
        const((1, LANES)), const((SSM_HEADS, 1)), const((1, D_C)), const((1, D_C)), const((1, D_C)),
        const((L, L)), const((L, L)), const((LANES, D_C)),
    ]
    expand = np.zeros((LANES, D_C), np.float32)
    for h in range(SSM_HEADS):
        expand[DT_LANE0 + h, h * SSM_HD:(h + 1) * SSM_HD] = 1.0
    out_specs = [
        pl.BlockSpec((L, D_C), lambda b, c: (b * nc + c, 0)),
        pl.BlockSpec((None, SSM_HEADS, SSM_HD, SSM_N), lambda b, c: (b, 0, 0, 0)),
        pl.BlockSpec((None, CONV_W - 1, CONV_DIM), lambda b, c: (b, 0, 0)),
    ]
    out_shape = [jax.ShapeDtypeStruct((bsz * t, D_C), _act_dtype(precise)),
                 jax.ShapeDtypeStruct((bsz, SSM_HEADS, SSM_HD, SSM_N), F32),
                 jax.ShapeDtypeStruct((bsz, CONV_W - 1, CONV_DIM), F32)]
    ones = jnp.ones((L, L), F32)
    return pl.pallas_call(
        functools.partial(_ssd_kernel, L=L, precise=precise), grid=(bsz, nc), in_specs=in_specs,
        out_specs=out_specs, out_shape=out_shape,
        scratch_shapes=[pltpu.VMEM((SSM_HEADS, SSM_HD, SSM_N), F32), pltpu.VMEM((CONV_PAD + L, CONV_DIM), F32),
                        pltpu.VMEM((L, D_C), F32)],
        compiler_params=_cparams(("parallel", "arbitrary")), name="ssd")(
            proj, proj, proj, small, dt_t, h0, conv0, lw["conv_w"], lw["conv_b"], lw["dtb_row"], lw["dtb_col"],
            lw["alog_row"], lw["alog_col"], lw["alog_e"], lw["dvec"], lw["norm_g"],
            jnp.tril(ones).astype(BF16), jnp.triu(ones).astype(BF16), jnp.asarray(expand, BF16))


def _merge_kernel(*refs, precise):
    it = iter(refs)
    ya_ref, ybt_ref, yc_ref, g0_ref, g1_ref, g2_ref, x_ref = [next(it) for _ in range(7)]
    wa, wb, wc, wo = [_take_w(it, precise) for _ in range(4)]
    lg_ref, lb_ref, o_ref, ob_ref = next(it), next(it), next(it), next(it)
    merged = _sigmoid(g0_ref[...]) * _mm(ya_ref[...], _ld(wa), precise)
    merged = merged + _sigmoid(g1_ref[...]) * _mm(ybt_ref[...], _ld(wb), precise, "tn")
    merged = merged + _sigmoid(g2_ref[...]) * _mm(yc_ref[...], _ld(wc), precise)
    mix = _mm(merged, _ld(wo), precise)
    y = _ln_rows(ALPHA * x_ref[...] + mix, lg_ref[...], lb_ref[...])
    o_ref[...] = y
    ob_ref[...] = y.astype(BF16)


def _merge_call(ya, yb_t, yc, proj, x, lw, tm, bsz, t, precise):
    n = x.shape[0]
    gblk = OFF_GATE // D_MODEL
    per_b = t // tm

    def rows(w):
        return pl.BlockSpec((tm, w), lambda i: (i, 0))

    def const(shape):
        return pl.BlockSpec(shape, lambda i: (0, 0))

    gates = [pl.BlockSpec((tm, D_MODEL), lambda i, j=j: (i, gblk + j)) for j in range(N_BRANCH)]
    wspecs, wargs = [], []
    for name, k in (("w_a", D_AV), ("w_b", D_B), ("w_c", D_C), ("w_o", D_MODEL)):
        ws = _wargs(lw, name, precise)
        wargs += ws
        wspecs += [const((k, D_MODEL))] * len(ws)
    return pl.pallas_call(
        functools.partial(_merge_kernel, precise=precise), grid=(n // tm,),
        in_specs=[rows(D_AV), pl.BlockSpec((None, D_B, tm), lambda i: (i // per_b, 0, i % per_b)), rows(D_C)]
        + gates + [rows(D_MODEL)] + wspecs + [const((1, D_MODEL)), const((1, D_MODEL))],
        out_specs=[rows(D_MODEL), rows(D_MODEL)],
        out_shape=[jax.ShapeDtypeStruct((n, D_MODEL), F32), jax.ShapeDtypeStruct((n, D_MODEL), BF16)],
        compiler_params=_cparams(("parallel",)), name="merge")(
            ya, yb_t, yc, proj, proj, proj, x, *wargs, lw["ln1_g"], lw["ln1_b"])


def _router_kernel(h_ref, whi_ref, wlo_ref, br_ref, o_ref, id_ref, w_ref):
    logits = _mm(h_ref[...], (whi_ref[...], wlo_ref[...]), True) + br_ref[...]
    lane = lax.broadcasted_iota(jnp.int32, logits.shape, 1)
    lg = jnp.where(lane < N_EXPERTS, logits, -jnp.inf)
    m1 = jnp.max(lg, axis=-1, keepdims=True)
    i1 = jnp.min(jnp.where(lg == m1, lane, LANES), axis=-1, keepdims=True)
    lg2 = jnp.where(lane == i1, -jnp.inf, lg)
    m2 = jnp.max(lg2, axis=-1, keepdims=True)
    i2 = jnp.min(jnp.where(lg2 == m2, lane, LANES), axis=-1, keepdims=True)
    e = jnp.exp(m2 - m1)
    den = 1.0 + e
    w1, w2 = 1.0 / den, e / den
    o_ref[...] = jnp.where(lane == i1, w1, 0.0) + jnp.where(lane == i2, w2, 0.0)
    id_ref[...] = jnp.where(lane == 0, i1, jnp.where(lane == 1, i2, 0))
    w_ref[...] = jnp.where(lane == 0, w1, jnp.where(lane == 1, w2, 0.0))


def _router_call(h, lw, tm):
    n = h.shape[0]
    out = pl.BlockSpec((tm, LANES), lambda i: (i, 0))
    return pl.pallas_call(
        _router_kernel, grid=(n // tm,),
        in_specs=[pl.BlockSpec((tm, D_MODEL), lambda i: (i, 0)), pl.BlockSpec((D_MODEL, LANES), lambda i: (0, 0)),
                  pl.BlockSpec((D_MODEL, LANES), lambda i: (0, 0)), pl.BlockSpec((1, LANES), lambda i: (0, 0))],
        out_specs=[out, out, out],
        out_shape=[jax.ShapeDtypeStruct((n, LANES), F32), jax.ShapeDtypeStruct((n, LANES), jnp.int32),
                   jax.ShapeDtypeStruct((n, LANES), F32)],
        compiler_params=_cparams(("parallel",)), name="router")(h, lw["wr"], lw["wr_lo"], lw["br"])


MOE_ROWS = 256
MOE_TM = 256
TOP_K = 2


def _moe_plan(ids, n):
    r = MOE_ROWS
    e = ids[:, :TOP_K].reshape(-1)
    na = n * TOP_K
    n_tiles = na // r + N_EXPERTS
    order = jnp.argsort(e, stable=True).astype(jnp.int32)
    inv = jnp.argsort(order).astype(jnp.int32)
    onehot = (e[:, None] == jnp.arange(N_EXPERTS, dtype=jnp.int32)[None, :]).astype(jnp.int32)
    counts = jnp.sum(onehot, axis=0)
    padded = ((counts + r - 1) // r) * r
    ends = jnp.cumsum(padded)
    starts = ends - padded
    first = jnp.cumsum(counts) - counts
    shift = starts - first
    pos = inv + jnp.sum(onehot * shift[None, :], axis=1)
    tile_row0 = jnp.arange(n_tiles, dtype=jnp.int32) * r
    tile_expert = jnp.minimum(jnp.sum(ends[None, :] <= tile_row0[:, None], axis=1), N_EXPERTS - 1).astype(jnp.int32)
    last = (first + counts - 1)[tile_expert]
    srt = jnp.minimum(tile_row0[:, None] - shift[tile_expert][:, None] + jnp.arange(r, dtype=jnp.int32)[None, :],
                      last[:, None])
    src_tok = order[jnp.clip(srt, 0, na - 1)] // TOP_K
    n_used = (ends[-1] // r).astype(jnp.int32).reshape(1)
    return src_tok.reshape(n_tiles, 1, r), pos.reshape(n // MOE_TM, 1, MOE_TM * TOP_K), tile_expert, n_used


def _row_copy(src_hbm, dst_vmem, sem, src_row, dst_row):
    return pltpu.make_async_copy(src_hbm.at[pl.ds(src_row, 1), :], dst_vmem.at[pl.ds(dst_row, 1), :], sem)


def _experts_kernel(te_ref, nu_ref, src_ref, h_hbm, wg_ref, wu_ref, wd_ref, y_ref, x_buf, sem):
    i = pl.program_id(0)
    r = x_buf.shape[0]

    @pl.when(i < nu_ref[0])
    def _():
        def start(k, c):
            _row_copy(h_hbm, x_buf, sem.at[0], src_ref[0, k], k).start()
            return c

        def wait(k, c):
            _row_copy(h_hbm, x_buf, sem.at[0], 0, k).wait()
            return c

        lax.fori_loop(0, r, start, 0, unroll=8)
        lax.fori_loop(0, r, wait, 0, unroll=8)
        xb = x_buf[...].astype(BF16)
        gate = _dot(xb, wg_ref[...])
        up = _dot(xb, wu_ref[...])
        y_ref[...] = _dot(((gate * _sigmoid(gate)) * up).astype(BF16), wd_ref[...])

    @pl.when(i >= nu_ref[0])
    def _():
        y_ref[...] = jnp.zeros_like(y_ref)


def _experts_call(h, src_tok, tile_expert, n_used, lw):
    n_tiles, _, r = src_tok.shape
    wg_spec = pl.BlockSpec((None, D_MODEL, D_FF_EXPERT), lambda i, te, nu: (te[i], 0, 0))
    wd_spec = pl.BlockSpec((None, D_FF_EXPERT, D_MODEL), lambda i, te, nu: (te[i], 0, 0))
    grid_spec = pltpu.PrefetchScalarGridSpec(
        num_scalar_prefetch=2, grid=(n_tiles,),
        in_specs=[pl.BlockSpec((None, 1, r), lambda i, te, nu: (i, 0, 0), memory_space=pltpu.SMEM),
                  pl.BlockSpec(memory_space=pl.ANY), wg_spec, wg_spec, wd_spec],
        out_specs=pl.BlockSpec((r, D_MODEL), lambda i, te, nu: (i, 0)),
        scratch_shapes=[pltpu.VMEM((r, D_MODEL), F32), pltpu.SemaphoreType.DMA((1,))])
    return pl.pallas_call(
        _experts_kernel, grid_spec=grid_spec,
        out_shape=jax.ShapeDtypeStruct((n_tiles * r, D_MODEL), F32),
        compiler_params=_cparams(("arbitrary",)), name="moe_experts")(
            tile_expert, n_used, src_tok, h, lw["w_gate"], lw["w_up"], lw["w_down"])


def _combine_kernel(pos_ref, hb_ref, h_ref, p_ref, w_ref, y_hbm, wp_ref, wpg_ref, lg_ref, lb_ref,
                    o_ref, ob_ref, y_buf, sem):
    tm = h_ref.shape[0]

    def start(k, c):
        for j in range(TOP_K):
            _row_copy(y_hbm, y_buf.at[j], sem.at[0], pos_ref[0, TOP_K * k + j], k).start()
        return c

    def wait(k, c):
        for j in range(TOP_K):
            _row_copy(y_hbm, y_buf.at[j], sem.at[0], 0, k).wait()
        return c

    lax.fori_loop(0, tm, start, 0, unroll=8)
    hb = hb_ref[...]
    ple = _dot(p_ref[...].astype(BF16), wp_ref[...]) * _sigmoid(_dot(hb, wpg_ref[...]))
    acc = ALPHA * h_ref[...] + ple
    lax.fori_loop(0, tm, wait, 0, unroll=8)
    w = w_ref[...]
    for j in range(TOP_K):
        acc = acc + w[:, j:j + 1] * y_buf[j]
    y = _ln_rows(acc, lg_ref[...], lb_ref[...])
    o_ref[...] = y
    ob_ref[...] = y.astype(BF16)


def _combine_call(hb, h, p, wsel, pos, y_sorted, lw):
    n = h.shape[0]
    tm = MOE_TM

    def rows(w):
        return pl.BlockSpec((tm, w), lambda i: (i, 0))

    def const(shape):
        return pl.BlockSpec(shape, lambda i: (0, 0))

    return pl.pallas_call(
        _combine_kernel, grid=(n // tm,),
        in_specs=[pl.BlockSpec((None, 1, tm * TOP_K), lambda i: (i, 0, 0), memory_space=pltpu.SMEM),
                  rows(D_MODEL), rows(D_MODEL), rows(PLE_DIM), rows(LANES), pl.BlockSpec(memory_space=pl.ANY),
                  const((PLE_DIM, D_MODEL)), const((D_MODEL, D_MODEL)), const((1, D_MODEL)), const((1, D_MODEL))],
        out_specs=[rows(D_MODEL), rows(D_MODEL)],
        out_shape=[jax.ShapeDtypeStruct((n, D_MODEL), F32), jax.ShapeDtypeStruct((n, D_MODEL), BF16)],
        scratch_shapes=[pltpu.VMEM((TOP_K, tm, D_MODEL), F32), pltpu.SemaphoreType.DMA((1,))],
        compiler_params=_cparams(("arbitrary",)), name="moe_combine")(
            pos, hb, h, p, wsel, y_sorted, lw["w_ple"], lw["w_pleg"], lw["ln2_g"], lw["ln2_b"])


def _ffn_kernel(*refs, weighted, precise):
    it = iter(refs)
    hb_ref, h_ref, p_ref, comb_ref = next(it), next(it), next(it), next(it)
    wg, wu, wd, wp, wpg = [_take_w(it, precise) for _ in range(5)]
    lg_ref, lb_ref, o_ref, ob_ref, acc_sc = next(it), next(it), next(it), next(it), next(it)
    j = pl.program_id(1)
    hb = h_ref[...] if precise else hb_ref[...]

    @pl.when(j == 0)
    def _():
        ple = _mm(p_ref[...], _ld(wp), precise) * _sigmoid(_mm(hb, _ld(wpg), precise))
        acc_sc[...] = ALPHA * h_ref[...] + ple

    gate = _mm(hb, _ld(wg), precise)
    up = _mm(hb, _ld(wu), precise)
    out = _mm((gate * _sigmoid(gate)) * up, _ld(wd), precise)
    if weighted:
        comb = comb_ref[...]
        lane = lax.broadcasted_iota(jnp.int32, comb.shape, 1)
        out = jnp.sum(jnp.where(lane == j, comb, 0.0), axis=-1, keepdims=True) * out
    acc_sc[...] += out

    @pl.when(j == pl.num_programs(1) - 1)
    def _():
        y = _ln_rows(acc_sc[...], lg_ref[...], lb_ref[...])
        o_ref[...] = y
        ob_ref[...] = y.astype(BF16)


def _ffn_call(hb, h, p, comb, lw, tm, precise):
    n = h.shape[0]
    weighted = comb is not None
    if weighted:
        n_e = N_EXPERTS
        wg_spec = pl.BlockSpec((None, D_MODEL, D_FF_EXPERT), lambda i, j: (j, 0, 0))
        wd_spec = pl.BlockSpec((None, D_FF_EXPERT, D_MODEL), lambda i, j: (j, 0, 0))
    else:
        n_e = D_FF // D_FF_EXPERT
        comb = jnp.zeros((n, LANES), F32)
        wg_spec = pl.BlockSpec((D_MODEL, D_FF_EXPERT), lambda i, j: (0, j))
        wd_spec = pl.BlockSpec((D_FF_EXPERT, D_MODEL), lambda i, j: (j, 0))

    def rows(w):
        return pl.BlockSpec((tm, w), lambda i, j: (i, 0))

    def const(shape):
        return pl.BlockSpec(shape, lambda i, j: (0, 0))

    wspecs, wargs = [], []
    for name, spec in (("w_gate", wg_spec), ("w_up", wg_spec), ("w_down", wd_spec),
                       ("w_ple", const((PLE_DIM, D_MODEL))), ("w_pleg", const((D_MODEL, D_MODEL)))):
        ws = _wargs(lw, name, precise)
        wargs += ws
        wspecs += [spec] * len(ws)
    return pl.pallas_call(
        functools.partial(_ffn_kernel, weighted=weighted, precise=precise), grid=(n // tm, n_e),
        in_specs=[rows(D_MODEL), rows(D_MODEL), rows(PLE_DIM), rows(LANES)] + wspecs
        + [const((1, D_MODEL)), const((1, D_MODEL))],
        out_specs=[rows(D_MODEL), rows(D_MODEL)],
        out_shape=[jax.ShapeDtypeStruct((n, D_MODEL), F32), jax.ShapeDtypeStruct((n, D_MODEL), BF16)],
        scratch_shapes=[pltpu.VMEM((tm, D_MODEL), F32)],
        compiler_params=_cparams(("parallel", "arbitrary")), name="ffn")(
            hb, h, p, comb, *wargs, lw["ln2_g"], lw["ln2_b"])


def _layer_weights(i, W):
    w_in = W["w_in"][i]
    sizes = [D_AV, D_AV, D_B, D_B, D_B, FOX_HEADS, D_C, CONV_DIM, SSM_HEADS, N_BRANCH * D_MODEL]
    o = [0] + [int(s) for s in np.cumsum(sizes)]
    seg = [w_in[:, o[k]:o[k + 1]] for k in range(len(sizes))]
    w_u, w_v, w_q, w_k, w_va, w_f, w_z, w_xbc, w_dt, w_gate = seg
    pad = jnp.zeros((D_MODEL, LANES - SSM_HEADS - FOX_HEADS), F32)

    def lane_row(v, lane0=0):
        return jnp.pad(v.astype(F32), (lane0, LANES - lane0 - v.shape[0])).reshape(1, LANES)

    lw = dict(
        fox_b=lane_row(W["fox_b_forget"][i]),
        gmlp_g=W["gmlp_ln_g"][i], gmlp_b=W["gmlp_ln_b"][i], gmlp_w=W["gmlp_w_spatial"][i], gmlp_bs=W["gmlp_b_spatial"][i],
        conv_w=W["ssm_conv_w"][i], conv_b=W["ssm_conv_b"][i].reshape(1, CONV_DIM),
        dtb_row=lane_row(W["ssm_dt_bias"][i], DT_LANE0), dtb_col=W["ssm_dt_bias"][i].reshape(SSM_HEADS, 1),
        alog_row=lane_row(W["ssm_a_log"][i], DT_LANE0), alog_col=W["ssm_a_log"][i].reshape(SSM_HEADS, 1),
        alog_e=jnp.repeat(W["ssm_a_log"][i], SSM_HD).reshape(1, D_C),
        dvec=jnp.repeat(W["ssm_d"][i], SSM_HD).reshape(1, D_C), norm_g=W["ssm_norm_g"][i].reshape(1, D_C),
        ln1_g=W["ln1_g"][i].reshape(1, D_MODEL), ln1_b=W["ln1_b"][i].reshape(1, D_MODEL),
        ln2_g=W["ln2_g"][i].reshape(1, D_MODEL), ln2_b=W["ln2_b"][i].reshape(1, D_MODEL),
    )
    j = i // 2
    mats = dict(
        w_main=jnp.concatenate([w_gate, w_xbc, w_z, w_u, w_v, w_k, w_va], axis=1),
        w_small=jnp.concatenate([w_f, w_dt, pad], axis=1),
        w_qt=jnp.transpose(w_q), w_vt=jnp.transpose(w_va),
        w_a=W["w_branch_a"][i], w_b=W["w_branch_b"][i], w_c=W["w_branch_c"][i], w_o=W["w_out"][i],
        w_ple=W["ple_w_proj"][i], w_pleg=W["ple_w_gate"][i],
    )
    if i % 2 == 0:
        mats.update(w_gate=W["ffn_w_gate"][j], w_up=W["ffn_w_up"][j], w_down=W["ffn_w_down"][j])
    else:
        mats.update(w_gate=W["moe_w_gate"][j], w_up=W["moe_w_up"][j], w_down=W["moe_w_down"][j],
                    wr=jnp.pad(W["moe_w_router"][j], ((0, 0), (0, LANES - N_EXPERTS))))
        lw["br"] = lane_row(W["moe_b_router"][j])
    for name, w in mats.items():
        c = w * np.float32(2.0**16 + 1.0)
        hi_f = c - (c - w)
        lw[name] = hi_f.astype(BF16)
        lw[name + "_lo"] = (w - hi_f).astype(BF16)
    return lw


def _trunk_layer(x, xb, p, i, lw, bsz, t, fox_cache, ssm0, conv0, tm, precise):
    n = bsz * t
    act = _act_dtype(precise)
    xin = x if precise else xb
    tmm = min(tm, t)
    proj = _matmul_call(xin, lw, "w_main", min(n, 2048), MAIN_TN, precise, "in_proj")
    small = _matmul_call(xin, lw, "w_small", tm, LANES, precise, "in_proj_small")

    w_eff, bias_eff = _gmlp_weights(lw["gmlp_w"], lw["gmlp_bs"], t)
    y_a, v_n = _gmlp_call(proj, lw["gmlp_g"], lw["gmlp_b"], w_eff, bias_eff, min(n, 512), precise)

    k_new = proj[:, OFF_K:OFF_K + D_B].reshape(bsz, t, D_B)
    v_new = proj[:, OFF_VA:OFF_VA + D_B].reshape(bsz, t, FOX_HEADS, FOX_HD)
    logf = _logsig_call(small, lw["fox_b"], tm).reshape(bsz, t, FOX_HEADS)
    q_t, v_t = _qvt_call(xin, lw, tmm, bsz, t, precise)
    if fox_cache is None:
        k_all, logf_all, q_off = k_new, logf, 0
    else:
        k_c, v_c, logf_c = fox_cache
        q_off = k_c.shape[1]
        k_all = jnp.concatenate([k_c.reshape(bsz, q_off, D_B), k_new], axis=1)
        v_t = jnp.concatenate([jnp.transpose(v_c.reshape(bsz, q_off, D_B), (0, 2, 1)).astype(act), v_t], axis=2)
        logf_all = jnp.concatenate([logf_c, logf], axis=1)
    t_k = k_all.shape[1]
    k_a = _keys_call(k_all, logf_all, 512 if t_k % 512 == 0 else t_k, precise)
    tq, tk = (512, 512) if t % 512 == 0 else (t, t_k)
    y_bt = _fox_call(q_t, k_a, v_t, tq, tk, q_off, 4, precise)

    dt_t = jnp.transpose(small[:, DT_LANE0:DT_LANE0 + SSM_HEADS].reshape(bsz, t, SSM_HEADS), (0, 2, 1))
    y_c, h_fin, conv_new = _ssd_call(proj, small, dt_t, ssm0, conv0, lw, bsz, t, precise)

    x1, x1b = _merge_call(y_a, y_bt, y_c, proj, x, lw, tmm, bsz, t, precise)
    if i % 2 == 0:
        x2, x2b = _ffn_call(x1b, x1, p, None, lw, tm, precise)
    else:
        comb, ids, wsel = _router_call(x1, lw, tm)
        if precise or n % MOE_TM or n < N_EXPERTS * MOE_ROWS:
            x2, x2b = _ffn_call(x1b, x1, p, comb, lw, tm, precise)
        else:
            src_tok, pos, tile_expert, n_used = _moe_plan(ids, n)
            y_sorted = _experts_call(x1, src_tok, tile_expert, n_used, lw)
            x2, x2b = _combine_call(x1b, x1, p, wsel, pos, y_sorted, lw)
    return x2, x2b, (k_new.reshape(bsz, t, FOX_HEADS, FOX_HD), v_new, logf, h_fin, conv_new,
                     v_n.reshape(bsz, t, D_AV))


def kernel(x_prompt, x_sample, p_prompt, p_sample, cache_fox_k, cache_fox_v, cache_fox_logf, state_ssm, state_conv, ln0_g, ln0_b, w_in, fox_b_forget, gmlp_ln_g, gmlp_ln_b, gmlp_w_spatial, gmlp_b_spatial, ssm_conv_w, ssm_conv_b, ssm_dt_bias, ssm_a_log, ssm_d, ssm_norm_g, w_branch_a, w_branch_b, w_branch_c, w_out, ln1_g, ln1_b, ln2_g, ln2_b, ffn_w_gate, ffn_w_up, ffn_w_down, moe_w_router, moe_b_router, moe_w_gate, moe_w_up, moe_w_down, ple_w_proj, ple_w_gate):
    W = dict(w_in=w_in, fox_b_forget=fox_b_forget, gmlp_ln_g=gmlp_ln_g, gmlp_ln_b=gmlp_ln_b,
             gmlp_w_spatial=gmlp_w_spatial, gmlp_b_spatial=gmlp_b_spatial, ssm_conv_w=ssm_conv_w,
             ssm_conv_b=ssm_conv_b, ssm_dt_bias=ssm_dt_bias, ssm_a_log=ssm_a_log, ssm_d=ssm_d,
             ssm_norm_g=ssm_norm_g, w_branch_a=w_branch_a, w_branch_b=w_branch_b,
             w_branch_c=w_branch_c, w_out=w_out, ln1_g=ln1_g, ln1_b=ln1_b, ln2_g=ln2_g, ln2_b=ln2_b,
             ffn_w_gate=ffn_w_gate, ffn_w_up=ffn_w_up, ffn_w_down=ffn_w_down,
             moe_w_router=moe_w_router, moe_b_router=moe_b_router, moe_w_gate=moe_w_gate, moe_w_up=moe_w_up,
             moe_w_down=moe_w_down, ple_w_proj=ple_w_proj, ple_w_gate=ple_w_gate)
    bp, tp, _ = x_prompt.shape
    bs, ts, _ = x_sample.shape
    tm_p = 512 if (bp * tp) % 512 == 0 else bp * tp
    tm_s = bs * ts
    xp, xpb = _layer_norm_call(x_prompt.reshape(bp * tp, D_MODEL), ln0_g, ln0_b, tm_p)
    xs, xsb = _layer_norm_call(x_sample.reshape(bs * ts, D_MODEL), ln0_g, ln0_b, tm_s)
    outs_p, outs_s = [], []
    for i in range(DEPTH):
        lw = _layer_weights(i, W)
        ssm0 = jnp.zeros((bp, SSM_HEADS, SSM_HD, SSM_N), F32)
        conv0 = jnp.zeros((bp, CONV_W - 1, CONV_DIM), F32)
        xp, xpb, st_p = _trunk_layer(xp, xpb, p_prompt[i].reshape(bp * tp, PLE_DIM), i, lw, bp, tp,
                                     None, ssm0, conv0, tm_p, False)
        xs, xsb, st_s = _trunk_layer(xs, xsb, p_sample[i].reshape(bs * ts, PLE_DIM), i, lw, bs, ts,
                                     (cache_fox_k[i], cache_fox_v[i], cache_fox_logf[i]),
                                     state_ssm[i], state_conv[i], tm_s, True)
        outs_p.append(st_p)
        outs_s.append(st_s)

    def stack(outs, k):
        return jnp.stack([o[k] for o in outs])

    return (xp.reshape(bp, tp, D_MODEL), xs.reshape(bs, ts, D_MODEL),
            stack(outs_p, 0), stack(outs_p, 1), stack(outs_p, 2), stack(outs_p, 3), stack(outs_p, 4),
            stack(outs_s, 0), stack(outs_s, 1), stack(outs_s, 2), stack(outs_s, 3), stack(outs_s, 4), stack(outs_s, 5))
```

```python
import functools
import math

import numpy as np
import jax
import jax.numpy as jnp
from jax import lax
from jax.experimental import pallas as pl
from jax.experimental.pallas import tpu as pltpu

F32 = jnp.float32
BF16 = jnp.bfloat16

D_MODEL = 1024
DEPTH = 2
CHUNK = 64
PLE_DIM = 256
GMLP_CHUNK = 128
GMLP_GROUPS = 4
D_AV = D_MODEL // 2
GMLP_GROUP_W = D_AV // GMLP_GROUPS
FOX_HD = 64
D_B = D_MODEL // 2
FOX_HEADS = D_B // FOX_HD
D_C = D_MODEL
SSM_HD = 64
SSM_HEADS = D_C // SSM_HD
SSM_N = 128
SSM_GROUPS = 2
CONV_W = 4
CONV_DIM = D_C + 2 * SSM_GROUPS * SSM_N
N_BRANCH = 3
D_FF = 11 * D_MODEL // 4
N_EXPERTS = 8
D_FF_EXPERT = D_FF // 2
ALPHA = (2.0 * DEPTH) ** 0.25
LN_EPS = 1e-5
RMS_EPS = 1e-5
NEG_INF = -1e30
LOG2E = math.log2(math.e)

LANES = 128
VMEM_LIMIT = 56 * 2**20
N_MAIN = CONV_DIM + D_C + 2 * D_AV
OFF_XBC, OFF_Z, OFF_U, OFF_V = 0, 1536, 2560, 3072
MAIN_TN = 512
GATE_TN = 1024
DT_LANE0 = FOX_HEADS
SSD_L = 128
CONV_PAD = 8
FOX_NBIAS = 3
FOX_DK = 128
FOX_DV = 80
FOX_QSUB = 256


def _cparams(sem):
    return pltpu.CompilerParams(dimension_semantics=sem, vmem_limit_bytes=VMEM_LIMIT)


def _sigmoid(x):
    return 1.0 / (1.0 + jnp.exp(-x))


def _softplus(x):
    return jnp.maximum(x, 0.0) + jnp.log1p(jnp.exp(-jnp.abs(x)))


def _gelu(x):
    c = np.float32(np.sqrt(2.0 / np.pi))
    return x * (0.5 * (1.0 + jnp.tanh(c * (x + 0.044715 * (x * x * x)))))


def _ln_rows(x, g, b):
    mu = jnp.mean(x, axis=-1, keepdims=True)
    xc = x - mu
    var = jnp.mean(xc * xc, axis=-1, keepdims=True)
    return xc * lax.rsqrt(var + LN_EPS) * g + b


def _bf16_part(x):
    u = lax.bitcast_convert_type(x, jnp.uint32) & jnp.uint32(0xFFFF0000)
    return lax.bitcast_convert_type(u, F32)


def _split2(x):
    hi = _bf16_part(x)
    return hi.astype(BF16), (x - hi).astype(BF16)


def _split3(x):
    hi = _bf16_part(x)
    r1 = x - hi
    mid = _bf16_part(r1)
    return hi.astype(BF16), mid.astype(BF16), (r1 - mid).astype(BF16)


def _dot(a, b):
    return jnp.dot(a, b, preferred_element_type=F32)


def _dot_nt(a, b):
    return lax.dot_general(a, b, (((1,), (1,)), ((), ())), preferred_element_type=F32)


def _dot_tn(a, b):
    return lax.dot_general(a, b, (((0,), (0,)), ((), ())), preferred_element_type=F32)


_DOTS = {"nn": _dot, "nt": _dot_nt, "tn": _dot_tn}


def _mm(a, b, precise, dims="nn"):
    dot = _DOTS[dims]
    if not precise:
        a = a[0] if isinstance(a, tuple) else a.astype(BF16)
        b = b[0] if isinstance(b, tuple) else b.astype(BF16)
        return dot(a, b)
    ah, al = a if isinstance(a, tuple) else _split2(a)
    bh, bl = b if isinstance(b, tuple) else _split2(b)
    return (dot(ah, bh) + dot(al, bh)) + dot(ah, bl)


def _take_w(it, precise):
    hi = next(it)
    return hi, (next(it) if precise else None)


def _ld(w, idx=...):
    return w[0][idx], (None if w[1] is None else w[1][idx])


def _wargs(lw, name, precise):
    return [lw[name], lw[name + "_lo"]] if precise else [lw[name]]


def _act_dtype(precise):
    return F32 if precise else BF16


def _ln_kernel(x_ref, g_ref, b_ref, o_ref, ob_ref):
    y = _ln_rows(x_ref[...], g_ref[...], b_ref[...])
    o_ref[...] = y
    ob_ref[...] = y.astype(BF16)


def _layer_norm_call(x, g, b, tm):
    n, d = x.shape
    row = pl.BlockSpec((tm, d), lambda i: (i, 0))
    vec = pl.BlockSpec((1, d), lambda i: (0, 0))
    return pl.pallas_call(
        _ln_kernel, grid=(n // tm,), in_specs=[row, vec, vec], out_specs=[row, row],
        out_shape=[jax.ShapeDtypeStruct((n, d), F32), jax.ShapeDtypeStruct((n, d), BF16)],
        compiler_params=_cparams(("parallel",)), name="ln_in")(x, g.reshape(1, d), b.reshape(1, d))


def _mm_kernel(*refs, precise, gate):
    it = iter(refs)
    x_ref = next(it)
    w = _take_w(it, precise)
    o_ref = next(it)
    y = _mm(x_ref[...], _ld(w), precise)
    o_ref[...] = (_sigmoid(y) if gate else y).astype(o_ref.dtype)


def _matmul_call(x, lw, wname, tm, tn, precise, name, gate=False):
    m, k = x.shape
    ws = _wargs(lw, wname, precise)
    n = ws[0].shape[1]
    wspec = pl.BlockSpec((k, tn), lambda i, j: (0, j))
    return pl.pallas_call(
        functools.partial(_mm_kernel, precise=precise, gate=gate), grid=(m // tm, n // tn),
        in_specs=[pl.BlockSpec((tm, k), lambda i, j: (i, 0))] + [wspec] * len(ws),
        out_specs=pl.BlockSpec((tm, tn), lambda i, j: (i, j)),
        out_shape=jax.ShapeDtypeStruct((m, n), _act_dtype(precise) if gate else F32),
        compiler_params=_cparams(("parallel", "parallel")), name=name)(x, *ws)


def _gmlp_kernel(*refs, rows, precise):
    it = iter(refs)
    u_ref, v_ref, g_ref, b_ref = next(it), next(it), next(it), next(it)
    w = _take_w(it, precise)
    bias_ref, ya_ref, vn_ref = next(it), next(it), next(it)
    u = _gelu(u_ref[...])
    vn = _ln_rows(_gelu(v_ref[...]), g_ref[...], b_ref[...])
    vn_ref[...] = vn
    vb = vn if precise else vn.astype(BF16)
    for c in range(rows // GMLP_CHUNK):
        r0 = c * GMLP_CHUNK
        for g in range(GMLP_GROUPS):
            c0 = g * GMLP_GROUP_W
            mixed = _mm(_ld(w, g), vb[r0:r0 + GMLP_CHUNK, c0:c0 + GMLP_GROUP_W], precise)
            mixed = mixed + bias_ref[:, c0:c0 + GMLP_GROUP_W]
            ya_ref[r0:r0 + GMLP_CHUNK, c0:c0 + GMLP_GROUP_W] = (
                u[r0:r0 + GMLP_CHUNK, c0:c0 + GMLP_GROUP_W] * mixed).astype(ya_ref.dtype)


def _gmlp_call(proj, ln_g, ln_b, w_eff, bias_eff, rows, precise):
    n = proj.shape[0]
    ublk, vblk = OFF_U // D_AV, OFF_V // D_AV
    vec = pl.BlockSpec((1, D_AV), lambda i: (0, 0))
    ws = list(_split2(w_eff)) if precise else [w_eff.astype(BF16)]
    wspec = pl.BlockSpec((GMLP_GROUPS, GMLP_CHUNK, GMLP_CHUNK), lambda i: (0, 0, 0))
    return pl.pallas_call(
        functools.partial(_gmlp_kernel, rows=rows, precise=precise), grid=(n // rows,),
        in_specs=[pl.BlockSpec((rows, D_AV), lambda i: (i, ublk)), pl.BlockSpec((rows, D_AV), lambda i: (i, vblk)),
                  vec, vec] + [wspec] * len(ws) + [pl.BlockSpec((GMLP_CHUNK, D_AV), lambda i: (0, 0))],
        out_specs=[pl.BlockSpec((rows, D_AV), lambda i: (i, 0)), pl.BlockSpec((rows, D_AV), lambda i: (i, 0))],
        out_shape=[jax.ShapeDtypeStruct((n, D_AV), _act_dtype(precise)), jax.ShapeDtypeStruct((n, D_AV), F32)],
        compiler_params=_cparams(("parallel",)), name="gmlp")(
            proj, proj, ln_g.reshape(1, D_AV), ln_b.reshape(1, D_AV), *ws, bias_eff)


def _gmlp_weights(w_s, b_s, t):
    l = min(GMLP_CHUNK, t)
    pos = np.arange(l)
    mask = (pos[None, :] // CHUNK) <= (pos[:, None] // CHUNK)
    w = jnp.where(mask[None], w_s[:, :l, :l], 0.0)
    bias = jnp.transpose(b_s[:, :l])
    reps = GMLP_CHUNK // l
    if reps > 1:
        eye = jnp.eye(reps, dtype=w.dtype)
        w = jnp.einsum("ab,gij->gaibj", eye, w).reshape(GMLP_GROUPS, GMLP_CHUNK, GMLP_CHUNK)
        bias = jnp.tile(bias, (reps, 1))
    return w, jnp.repeat(bias, GMLP_GROUP_W, axis=1)


def _kvs_kernel(*refs, precise):
    it = iter(refs)
    x_ref = next(it)
    wk, wv, ws = [_take_w(it, precise) for _ in range(3)]
    fb_ref, k_ref, v_ref, s_ref, lf_ref = [next(it) for _ in range(5)]
    x = x_ref[...]
    k_ref[...] = _mm(x, _ld(wk), precise)
    v_ref[...] = _mm(x, _ld(wv), precise)
    small = _mm(x, _ld(ws), precise)
    s_ref[...] = small
    lf_ref[...] = (-_softplus(-(small + fb_ref[...])))[:, :FOX_HEADS]


def _kvs_call(x, lw, tm, precise):
    n = x.shape[0]
    wargs, wspecs = [], []
    for name, width in (("w_k", D_B), ("w_va", D_B), ("w_small", LANES)):
        ws = _wargs(lw, name, precise)
        wargs += ws
        wspecs += [pl.BlockSpec((D_MODEL, width), lambda i: (0, 0))] * len(ws)

    def rows(w):
        return pl.BlockSpec((tm, w), lambda i: (i, 0))

    return pl.pallas_call(
        functools.partial(_kvs_kernel, precise=precise), grid=(n // tm,),
        in_specs=[rows(D_MODEL)] + wspecs + [pl.BlockSpec((1, LANES), lambda i: (0, 0))],
        out_specs=[rows(D_B), rows(D_B), rows(LANES), rows(FOX_HEADS)],
        out_shape=[jax.ShapeDtypeStruct((n, D_B), F32), jax.ShapeDtypeStruct((n, D_B), F32),
                   jax.ShapeDtypeStruct((n, LANES), F32), jax.ShapeDtypeStruct((n, FOX_HEADS), F32)],
        compiler_params=_cparams(("parallel",)), name="fox_kv")(x, *wargs, lw["fox_b"])


def _qvt_kernel(*refs, precise):
    it = iter(refs)
    x_ref = next(it)
    wq, wv = _take_w(it, precise), _take_w(it, precise)
    qt_ref, vt_ref = next(it), next(it)
    x = x_ref[...]
    qt_ref[...] = (_mm(_ld(wq), x, precise, "nt") * (FOX_HD ** -0.5 * LOG2E)).astype(qt_ref.dtype)
    vt_ref[...] = _mm(_ld(wv), x, precise, "nt").astype(vt_ref.dtype)


def _qvt_call(x, lw, tm, bsz, t, precise):
    n = x.shape[0]
    per_b = t // tm
    act = _act_dtype(precise)
    ws = _wargs(lw, "w_qt", precise) + _wargs(lw, "w_vt", precise)
    out = pl.BlockSpec((None, D_B, tm), lambda i: (i // per_b, 0, i % per_b))
    return pl.pallas_call(
        functools.partial(_qvt_kernel, precise=precise), grid=(n // tm,),
        in_specs=[pl.BlockSpec((tm, D_MODEL), lambda i: (i, 0))]
        + [pl.BlockSpec((D_B, D_MODEL), lambda i: (0, 0))] * len(ws),
        out_specs=[out, out],
        out_shape=[jax.ShapeDtypeStruct((bsz, D_B, t), act), jax.ShapeDtypeStruct((bsz, D_B, t), act)],
        compiler_params=_cparams(("parallel",)), name="fox_qv_t")(x, *ws)


def _place(x, sel, precise):
    if not precise:
        return _dot(x.astype(BF16), sel)
    hi, mid, lo = _split3(x)
    return (_dot(hi, sel) + _dot(mid, sel)) + _dot(lo, sel)


def _keys_kernel(k_ref, lf_ref, tril_ref, selk_ref, selg_ref, o_ref, carry_ref, *, precise):
    @pl.when(pl.program_id(1) == 0)
    def _():
        carry_ref[...] = jnp.zeros_like(carry_ref)

    tril = tril_ref[...]
    hi, mid, lo = _split3(lf_ref[...])
    f = (_dot(tril, hi) + _dot(tril, mid)) + _dot(tril, lo) + carry_ref[0:1, :]
    tc = f.shape[0]
    carry_ref[...] = jnp.broadcast_to(f[tc - 1:tc, :], carry_ref.shape)
    g = f * (-LOG2E)
    g1 = _bf16_part(g)
    r = g - g1
    g2 = _bf16_part(r)
    gcat = jnp.concatenate([g1, g2, r - g2], axis=1)
    ka = _place(k_ref[...], selk_ref[...], precise) + _place(gcat, selg_ref[...], precise)
    o_ref[...] = ka.astype(o_ref.dtype)


def _keys_call(k_all, logf_all, tc, precise):
    bsz, t_k, _ = k_all.shape
    sel_k = np.zeros((D_B, FOX_HEADS * FOX_DK), np.float32)
    sel_g = np.zeros((FOX_NBIAS * FOX_HEADS, FOX_HEADS * FOX_DK), np.float32)
    for h in range(FOX_HEADS):
        sel_k[h * FOX_HD + np.arange(FOX_HD), h * FOX_DK + np.arange(FOX_HD)] = 1.0
        for j in range(FOX_NBIAS):
            sel_g[j * FOX_HEADS + h, h * FOX_DK + FOX_HD + j] = 1.0
    tril = jnp.tril(jnp.ones((tc, tc), F32)).astype(BF16)
    return pl.pallas_call(
        functools.partial(_keys_kernel, precise=precise), grid=(bsz, t_k // tc),
        in_specs=[pl.BlockSpec((None, tc, D_B), lambda b, j: (b, j, 0)),
                  pl.BlockSpec((None, tc, FOX_HEADS), lambda b, j: (b, j, 0)),
                  pl.BlockSpec((tc, tc), lambda b, j: (0, 0)),
                  pl.BlockSpec(sel_k.shape, lambda b, j: (0, 0)), pl.BlockSpec(sel_g.shape, lambda b, j: (0, 0))],
        out_specs=pl.BlockSpec((None, tc, FOX_HEADS * FOX_DK), lambda b, j: (b, j, 0)),
        out_shape=jax.ShapeDtypeStruct((bsz, t_k, FOX_HEADS * FOX_DK), _act_dtype(precise)),
        scratch_shapes=[pltpu.VMEM((8, FOX_HEADS), F32)],
        compiler_params=_cparams(("parallel", "arbitrary")), name="fox_keys")(
            k_all, logf_all, tril, jnp.asarray(sel_k, BF16), jnp.asarray(sel_g, BF16))


def _fox_kernel(qt_ref, ka_ref, vt_ref, o_ref, qa_sc, m_sc, acc_sc, s_sc, mb_sc, *, tq, tk, qs, q_off, precise):
    i = pl.program_id(2)
    act = qa_sc.dtype
    nh = qa_sc.shape[0]
    rows = lax.broadcasted_iota(jnp.int32, (FOX_DK - FOX_HD, tq), 0)
    ones_rows = jnp.where(rows < FOX_NBIAS, 1.0, 0.0).astype(act)
    for h in range(nh):
        qa_sc[h] = jnp.concatenate([qt_ref[h * FOX_HD:(h + 1) * FOX_HD, :], ones_rows], axis=0)
    vrows = lax.broadcasted_iota(jnp.int32, (FOX_DV - FOX_HD, tk), 0)
    v_extra = jnp.where(vrows < 1, 1.0, 0.0).astype(act)
    m_sc[...] = jnp.full_like(m_sc, -jnp.inf)
    acc_sc[...] = jnp.zeros_like(acc_sc)
    first_q = q_off + i * tq
    subs = [(h, c, slice(c * qs, (c + 1) * qs)) for h in range(nh) for c in range(tq // qs)]

    def scores(k0):
        for h, _, sl in subs:
            ka = ka_ref[pl.ds(k0, tk), h * FOX_DK:(h + 1) * FOX_DK]
            s = _mm(ka, qa_sc[h, :, sl], precise)
            s_sc[h, :, sl] = s
            mb_sc[h, :, sl] = jnp.max(s, axis=0, keepdims=True)

    def consume(k0, masked):
        for h, c, sl in subs:
            vt = jnp.concatenate([vt_ref[h * FOX_HD:(h + 1) * FOX_HD, pl.ds(k0, tk)], v_extra], axis=0)
            s = s_sc[h, :, sl]
            if masked:
                kpos = k0 + lax.broadcasted_iota(jnp.int32, (tk, qs), 0)
                qpos = first_q + c * qs + lax.broadcasted_iota(jnp.int32, (tk, qs), 1)
                s = jnp.where(kpos <= qpos, s, NEG_INF)
                mb = jnp.max(s, axis=0, keepdims=True)
            else:
                mb = mb_sc[h, :, sl]
            m_prev = m_sc[h, :, sl]
            m_new = jnp.maximum(m_prev, mb)
            alpha = jnp.exp2(m_prev - m_new)
            p = jnp.exp2(s - m_new)
            acc_sc[h, :, sl] = alpha * acc_sc[h, :, sl] + _mm(vt, p, precise)
            m_sc[h, :, sl] = m_new

    scores(0)
    if ka_ref.shape[0] == tk:
        consume(0, True)
    else:
        assert tq == tk and q_off == 0

        def body(j, c):
            k0 = pl.multiple_of(j * tk, tk)
            consume(k0, False)
            scores(pl.multiple_of(k0 + tk, tk))
            return c

        lax.fori_loop(0, i, body, 0)
        consume(pl.multiple_of(i * tk, tk), True)
    for h in range(nh):
        acc = acc_sc[h]
        o_ref[h * FOX_HD:(h + 1) * FOX_HD, :] = (acc[:FOX_HD] / acc[FOX_HD:FOX_HD + 1]).astype(o_ref.dtype)


def _fox_call(q_t, k_a, v_t, tq, tk, q_off, nh, precise):
    bsz, _, t_q = q_t.shape
    t_k = k_a.shape[1]
    qs = min(FOX_QSUB, tq)
    act = _act_dtype(precise)
    qo = pl.BlockSpec((None, nh * FOX_HD, tq), lambda b, hh, i: (b, hh, i))
    return pl.pallas_call(
        functools.partial(_fox_kernel, tq=tq, tk=tk, qs=qs, q_off=q_off, precise=precise),
        grid=(bsz, FOX_HEADS // nh, t_q // tq),
        in_specs=[qo, pl.BlockSpec((None, t_k, nh * FOX_DK), lambda b, hh, i: (b, 0, hh)),
                  pl.BlockSpec((None, nh * FOX_HD, t_k), lambda b, hh, i: (b, hh, 0))],
        out_specs=qo, out_shape=jax.ShapeDtypeStruct((bsz, D_B, t_q), act),
        scratch_shapes=[pltpu.VMEM((nh, FOX_DK, tq), act), pltpu.VMEM((nh, 1, tq), F32),
                        pltpu.VMEM((nh, FOX_DV, tq), F32), pltpu.VMEM((nh, tk, tq), F32),
                        pltpu.VMEM((nh, 1, tq), F32)],
        compiler_params=_cparams(("parallel", "parallel", "arbitrary")), name="fox_attn")(q_t, k_a, v_t)


def _ssd_kernel(xbc_ref, z0_ref, z1_ref, dtp_ref, dtt_ref, h0_ref, c0_ref, cw_ref, cb_ref, dtb_row_ref, dtb_col_ref,
                alog_row_ref, alog_col_ref, aloge_ref, dvec_ref, ng_ref, tril_ref, triu_ref, expand_ref,
                yc_ref, hfin_ref, cnew_ref, h_sc, pad_sc, y_sc, *, L, precise):
    c = pl.program_id(1)

    @pl.when(c == 0)
    def _():
        h_sc[...] = h0_ref[...]
        pad_sc[CONV_PAD - (CONV_W - 1):CONV_PAD, :] = c0_ref[...]

    xin = xbc_ref[...]
    pad_sc[CONV_PAD:CONV_PAD + L, :] = xin
    y = cb_ref[...]
    for j in range(CONV_W - 1):
        r0 = CONV_PAD - (CONV_W - 1) + j
        y = y + pad_sc[r0:r0 + L, :] * cw_ref[j:j + 1, :]
    y = y + xin * cw_ref[CONV_W - 1:CONV_W, :]
    tail = pad_sc[CONV_PAD + L - (CONV_W - 1):CONV_PAD + L, :]
    pad_sc[CONV_PAD - (CONV_W - 1):CONV_PAD, :] = tail
    cnew_ref[...] = tail
    act = y * _sigmoid(y)

    dt_c = _softplus(dtp_ref[...] + dtb_row_ref[...])
    a_c = dt_c * (-jnp.exp(alog_row_ref[...]))
    tril = tril_ref[...]
    hi, mid, lo = _split3(a_c)
    acum_c = (_dot(tril, hi) + _dot(tril, mid)) + _dot(tril, lo)
    dt_r = _softplus(dtt_ref[...] + dtb_col_ref[...])
    a_r = dt_r * (-jnp.exp(alog_col_ref[...]))
    triu = triu_ref[...]
    hi, mid, lo = _split3(a_r)
    acum_r = (_dot(hi, triu) + _dot(mid, triu)) + _dot(lo, triu)

    dt_e = _place(dt_c, expand_ref[...], True)
    hi, mid, lo = _split3(dt_e * (-jnp.exp(aloge_ref[...])))
    acum_e = (_dot(tril, hi) + _dot(tril, mid)) + _dot(tril, lo)
    xs = act[:, :D_C]
    e_in = jnp.exp(acum_e)
    xw = xs * (jnp.exp(acum_e[L - 1:L, :] - acum_e) * dt_e)
    dx = dvec_ref[...] * xs

    row = lax.broadcasted_iota(jnp.int32, (L, L), 0)
    col = lax.broadcasted_iota(jnp.int32, (L, L), 1)
    causal = col <= row
    first_of_pair = lax.broadcasted_iota(jnp.int32, (L, 2 * SSM_HD), 1) < SSM_HD
    rep = SSM_HEADS // SSM_GROUPS
    gw = rep * SSM_HD
    for g in range(SSM_GROUPS):
        b0 = D_C + g * SSM_N
        c0 = D_C + SSM_GROUPS * SSM_N + g * SSM_N
        b_g = act[:, b0:b0 + SSM_N]
        c_g = act[:, c0:c0 + SSM_N]
        b_gs = _split2(b_g) if precise else (b_g.astype(BF16), None)
        cb = _mm(c_g, b_gs, precise, "nt")
        h_prev = h_sc[g * rep:(g + 1) * rep].reshape(gw, SSM_N)
        y_off = _mm(c_g, h_prev, precise, "nt") * e_in[:, g * gw:(g + 1) * gw]
        s_g = _mm(xw[:, g * gw:(g + 1) * gw], b_gs, precise, "tn")
        for hh in range(rep):
            h = g * rep + hh
            alast = acum_c[L - 1:L, DT_LANE0 + h:DT_LANE0 + h + 1]
            h_sc[h] = jnp.exp(alast) * h_sc[h] + s_g[hh * SSM_HD:(hh + 1) * SSM_HD, :]
        for pr in range(rep // 2):
            ch0 = g * gw + pr * 2 * SSM_HD
            x_pair = xs[:, ch0:ch0 + 2 * SSM_HD]
            x_pair = _split2(x_pair) if precise else (x_pair.astype(BF16), None)
            y_pair = []
            for hh in (2 * pr, 2 * pr + 1):
                h = g * rep + hh
                lane = DT_LANE0 + h
                decay = jnp.exp(jnp.where(causal, acum_c[:, lane:lane + 1] - acum_r[h:h + 1, :], -jnp.inf))
                y_pair.append(_mm((cb * decay) * dt_r[h:h + 1, :], x_pair, precise))
            y_diag = jnp.where(first_of_pair, y_pair[0], y_pair[1])
            y_sc[:, ch0:ch0 + 2 * SSM_HD] = (y_diag + y_off[:, ch0 - g * gw:ch0 - g * gw + 2 * SSM_HD]) + dx[:, ch0:ch0 + 2 * SSM_HD]

    @pl.when(c == pl.num_programs(1) - 1)
    def _():
        hfin_ref[...] = h_sc[...]

    z = jnp.concatenate([z0_ref[...], z1_ref[...]], axis=1)
    yg =y_sc[...] * (z * _sigmoid(z))
    ms = jnp.mean(yg * yg, axis=-1, keepdims=True)
    yc_ref[...] = ((yg * lax.rsqrt(ms + RMS_EPS)) * ng_ref[...]).astype(yc_ref.dtype)


def _ssd_call(proj, small, dt_t, h0, conv0, lw, bsz, t, precise):
    L = min(SSD_L, t)
    nc = t // L
    zhalf = D_C // 2
    zblk = OFF_Z // zhalf

    def const(shape):
        return pl.BlockSpec(shape, lambda b, c: (0,) * len(shape))

    in_specs = [
        pl.BlockSpec((L, CONV_DIM), lambda b, c: (b * nc + c, OFF_XBC // CONV_DIM)),
        pl.BlockSpec((L, zhalf), lambda b, c: (b * nc + c, zblk)),
        pl.BlockSpec((L, zhalf), lambda b, c: (b * nc + c, zblk + 1)),
        pl.BlockSpec((L, LANES), lambda b, c: (b * nc + c, 0)),
        pl.BlockSpec((None, SSM_HEADS, L), lambda b, c: (b, 0, c)),
        pl.BlockSpec((None, SSM_HEADS, SSM_HD, SSM_N), lambda b, c: (b, 0, 0, 0)),
        pl.BlockSpec((None, CONV_W - 1, CONV_DIM), lambda b, c: (b, 0, 0)),
        const((CONV_W, CONV_DIM)), const((1, CONV_DIM)), const((1, LANES)), const((SSM_HEADS, 1)),
        const((1, LANES)), const((SSM_HEADS, 1)), const((1, D_C)), const((1, D_C)), const((1, D_C)),
        const((L, L)), const((L, L)), const((LANES, D_C)),
    ]
    expand = np.zeros((LANES, D_C), np.float32)
    for h in range(SSM_HEADS):
        expand[DT_LANE0 + h, h * SSM_HD:(h + 1) * SSM_HD] = 1.0
    out_specs = [
        pl.BlockSpec((L, D_C), lambda b, c: (b * nc + c, 0)),
        pl.BlockSpec((None, SSM_HEADS, SSM_HD, SSM_N), lambda b, c: (b, 0, 0, 0)),
        pl.BlockSpec((None, CONV_W - 1, CONV_DIM), lambda b, c: (b, 0, 0)),
    ]
    out_shape = [jax.ShapeDtypeStruct((bsz * t, D_C), _act_dtype(precise)),
                 jax.ShapeDtypeStruct((bsz, SSM_HEADS, SSM_HD, SSM_N), F32),
                 jax.ShapeDtypeStruct((bsz, CONV_W - 1, CONV_DIM), F32)]
    ones = jnp.ones((L, L), F32)
    return pl.pallas_call(
        functools.partial(_ssd_kernel, L=L, precise=precise), grid=(bsz, nc), in_specs=in_specs,
        out_specs=out_specs, out_shape=out_shape,
        scratch_shapes=[pltpu.VMEM((SSM_HEADS, SSM_HD, SSM_N), F32), pltpu.VMEM((CONV_PAD + L, CONV_DIM), F32),
                        pltpu.VMEM((L, D_C), F32)],
        compiler_params=_cparams(("parallel", "arbitrary")), name="ssd")(
            proj, proj, proj, small, dt_t, h0, conv0, lw["conv_w"], lw["conv_b"], lw["dtb_row"], lw["dtb_col"],
            lw["alog_row"], lw["alog_col"], lw["alog_e"], lw["dvec"], lw["norm_g"],
            jnp.tril(ones).astype(BF16), jnp.triu(ones).astype(BF16), jnp.asarray(expand, BF16))


def _merge_kernel(*refs, precise):
    it = iter(refs)
    ya_ref, ybt_ref, yc_ref, g0_ref, g1_ref, g2_ref, x_ref = [next(it) for _ in range(7)]
    wa, wb, wc, wo = [_take_w(it, precise) for _ in range(4)]
    lg_ref, lb_ref, o_ref, ob_ref = next(it), next(it), next(it), next(it)
    merged = g0_ref[...].astype(F32) * _mm(ya_ref[...], _ld(wa), precise)
    merged = merged + g1_ref[...].astype(F32) * _mm(ybt_ref[...], _ld(wb), precise, "tn")
    merged = merged + g2_ref[...].astype(F32) * _mm(yc_ref[...], _ld(wc), precise)
    mix = _mm(merged, _ld(wo), precise)
    y = _ln_rows(ALPHA * x_ref[...] + mix, lg_ref[...], lb_ref[...])
    o_ref[...] = y
    ob_ref[...] = y.astype(BF16)


def _merge_call(ya, yb_t, yc, gates_sig, x, lw, tm, bsz, t, precise):
    n = x.shape[0]
    per_b = t // tm

    def rows(w):
        return pl.BlockSpec((tm, w), lambda i: (i, 0))

    def const(shape):
        return pl.BlockSpec(shape, lambda i: (0, 0))

    gates = [pl.BlockSpec((tm, D_MODEL), lambda i, j=j: (i, j)) for j in range(N_BRANCH)]
    wspecs, wargs = [], []
    for name, k in (("w_a", D_AV), ("w_b", D_B), ("w_c", D_C), ("w_o", D_MODEL)):
        ws = _wargs(lw, name, precise)
        wargs += ws
        wspecs += [const((k, D_MODEL))] * len(ws)
    return pl.pallas_call(
        functools.partial(_merge_kernel, precise=precise), grid=(n // tm,),
        in_specs=[rows(D_AV), pl.BlockSpec((None, D_B, tm), lambda i: (i // per_b, 0, i % per_b)), rows(D_C)]
        + gates + [rows(D_MODEL)] + wspecs + [const((1, D_MODEL)), const((1, D_MODEL))],
        out_specs=[rows(D_MODEL), rows(D_MODEL)],
        out_shape=[jax.ShapeDtypeStruct((n, D_MODEL), F32), jax.ShapeDtypeStruct((n, D_MODEL), BF16)],
        compiler_params=_cparams(("parallel",)), name="merge")(
            ya, yb_t, yc, gates_sig, gates_sig, gates_sig, x, *wargs, lw["ln1_g"], lw["ln1_b"])


def _router_kernel(h_ref, whi_ref, wlo_ref, br_ref, o_ref, id_ref, w_ref):
    logits = _mm(h_ref[...], (whi_ref[...], wlo_ref[...]), True) + br_ref[...]
    lane = lax.broadcasted_iota(jnp.int32, logits.shape, 1)
    lg = jnp.where(lane < N_EXPERTS, logits, -jnp.inf)
    m1 = jnp.max(lg, axis=-1, keepdims=True)
    i1 = jnp.min(jnp.where(lg == m1, lane, LANES), axis=-1, keepdims=True)
    lg2 = jnp.where(lane == i1, -jnp.inf, lg)
    m2 = jnp.max(lg2, axis=-1, keepdims=True)
    i2 = jnp.min(jnp.where(lg2 == m2, lane, LANES), axis=-1, keepdims=True)
    e = jnp.exp(m2 - m1)
    den = 1.0 + e
    w1, w2 = 1.0 / den, e / den
    o_ref[...] = jnp.where(lane == i1, w1, 0.0) + jnp.where(lane == i2, w2, 0.0)
    id_ref[...] = jnp.where(lane == 0, i1, jnp.where(lane == 1, i2, 0))
    w_ref[...] = jnp.where(lane == 0, w1, jnp.where(lane == 1, w2, 0.0))


def _router_call(h, lw, tm):
    n = h.shape[0]
    out = pl.BlockSpec((tm, LANES), lambda i: (i, 0))
    return pl.pallas_call(
        _router_kernel, grid=(n // tm,),
        in_specs=[pl.BlockSpec((tm, D_MODEL), lambda i: (i, 0)), pl.BlockSpec((D_MODEL, LANES), lambda i: (0, 0)),
                  pl.BlockSpec((D_MODEL, LANES), lambda i: (0, 0)), pl.BlockSpec((1, LANES), lambda i: (0, 0))],
        out_specs=[out, out, out],
        out_shape=[jax.ShapeDtypeStruct((n, LANES), F32), jax.ShapeDtypeStruct((n, LANES), jnp.int32),
                   jax.ShapeDtypeStruct((n, LANES), F32)],
        compiler_params=_cparams(("parallel",)), name="router")(h, lw["wr"], lw["wr_lo"], lw["br"])


MOE_ROWS = 256
MOE_TM = 256
TOP_K = 2


def _moe_plan(ids, n):
    r = MOE_ROWS
    e = ids[:, :TOP_K].reshape(-1)
    na = n * TOP_K
    n_tiles = na // r + N_EXPERTS
    order = jnp.argsort(e, stable=True).astype(jnp.int32)
    inv = jnp.argsort(order).astype(jnp.int32)
    onehot = (e[:, None] == jnp.arange(N_EXPERTS, dtype=jnp.int32)[None, :]).astype(jnp.int32)
    counts = jnp.sum(onehot, axis=0)
    padded = ((counts + r - 1) // r) * r
    ends = jnp.cumsum(padded)
    starts = ends - padded
    first = jnp.cumsum(counts) - counts
    shift = starts - first
    pos = inv + jnp.sum(onehot * shift[None, :], axis=1)
    tile_row0 = jnp.arange(n_tiles, dtype=jnp.int32) * r
    tile_expert = jnp.minimum(jnp.sum(ends[None, :] <= tile_row0[:, None], axis=1), N_EXPERTS - 1).astype(jnp.int32)
    last = (first + counts - 1)[tile_expert]
    srt = jnp.minimum(tile_row0[:, None] - shift[tile_expert][:, None] + jnp.arange(r, dtype=jnp.int32)[None, :],
                      last[:, None])
    src_tok = order[jnp.clip(srt, 0, na - 1)] // TOP_K
    n_used = (ends[-1] // r).astype(jnp.int32).reshape(1)
    return src_tok.reshape(n_tiles, 1, r), pos.reshape(n // MOE_TM, 1, MOE_TM * TOP_K), tile_expert, n_used


def _row_copy(src_hbm, dst_vmem, sem, src_row, dst_row):
    return pltpu.make_async_copy(src_hbm.at[pl.ds(src_row, 1), :], dst_vmem.at[pl.ds(dst_row, 1), :], sem)


def _experts_kernel(te_ref, nu_ref, src_ref, h_hbm, wg_ref, wu_ref, wd_ref, y_ref, x_buf, sem):
    i = pl.program_id(0)
    r = x_buf.shape[0]

    @pl.when(i < nu_ref[0])
    def _():
        def start(k, c):
            _row_copy(h_hbm, x_buf, sem.at[0], src_ref[0, k], k).start()
            return c

        def wait(k, c):
            _row_copy(h_hbm, x_buf, sem.at[0], 0, k).wait()
            return c

        lax.fori_loop(0, r, start, 0, unroll=8)
        lax.fori_loop(0, r, wait, 0, unroll=8)
        xb = x_buf[...].astype(BF16)
        gate = _dot(xb, wg_ref[...])
        up = _dot(xb, wu_ref[...])
        y_ref[...] = _dot(((gate * _sigmoid(gate)) * up).astype(BF16), wd_ref[...])

    @pl.when(i >= nu_ref[0])
    def _():
        y_ref[...] = jnp.zeros_like(y_ref)


def _experts_call(h, src_tok, tile_expert, n_used, lw):
    n_tiles, _, r = src_tok.shape
    wg_spec = pl.BlockSpec((None, D_MODEL, D_FF_EXPERT), lambda i, te, nu: (te[i], 0, 0))
    wd_spec = pl.BlockSpec((None, D_FF_EXPERT, D_MODEL), lambda i, te, nu: (te[i], 0, 0))
    grid_spec = pltpu.PrefetchScalarGridSpec(
        num_scalar_prefetch=2, grid=(n_tiles,),
        in_specs=[pl.BlockSpec((None, 1, r), lambda i, te, nu: (i, 0, 0), memory_space=pltpu.SMEM),
                  pl.BlockSpec(memory_space=pl.ANY), wg_spec, wg_spec, wd_spec],
        out_specs=pl.BlockSpec((r, D_MODEL), lambda i, te, nu: (i, 0)),
        scratch_shapes=[pltpu.VMEM((r, D_MODEL), F32), pltpu.SemaphoreType.DMA((1,))])
    return pl.pallas_call(
        _experts_kernel, grid_spec=grid_spec,
        out_shape=jax.ShapeDtypeStruct((n_tiles * r, D_MODEL), F32),
        compiler_params=_cparams(("arbitrary",)), name="moe_experts")(
            tile_expert, n_used, src_tok, h, lw["w_gate"], lw["w_up"], lw["w_down"])


def _combine_kernel(pos_ref, hb_ref, h_ref, p_ref, w_ref, y_hbm, wp_ref, wpg_ref, lg_ref, lb_ref,
                    o_ref, ob_ref, y_buf, sem):
    tm = h_ref.shape[0]

    def start(k, c):
        for j in range(TOP_K):
            _row_copy(y_hbm, y_buf.at[j], sem.at[0], pos_ref[0, TOP_K * k + j], k).start()
        return c

    def wait(k, c):
        for j in range(TOP_K):
            _row_copy(y_hbm, y_buf.at[j], sem.at[0], 0, k).wait()
        return c

    lax.fori_loop(0, tm, start, 0, unroll=8)
    hb = hb_ref[...]
    ple = _dot(p_ref[...].astype(BF16), wp_ref[...]) * _sigmoid(_dot(hb, wpg_ref[...]))
    acc = ALPHA * h_ref[...] + ple
    lax.fori_loop(0, tm, wait, 0, unroll=8)
    w = w_ref[...]
    for j in range(TOP_K):
        acc = acc + w[:, j:j + 1] * y_buf[j]
    y = _ln_rows(acc, lg_ref[...], lb_ref[...])
    o_ref[...] = y
    ob_ref[...] = y.astype(BF16)


def _combine_call(hb, h, p, wsel, pos, y_sorted, lw):
    n = h.shape[0]
    tm = MOE_TM

    def rows(w):
        return pl.BlockSpec((tm, w), lambda i: (i, 0))

    def const(shape):
        return pl.BlockSpec(shape, lambda i: (0, 0))

    return pl.pallas_call(
        _combine_kernel, grid=(n // tm,),
        in_specs=[pl.BlockSpec((None, 1, tm * TOP_K), lambda i: (i, 0, 0), memory_space=pltpu.SMEM),
                  rows(D_MODEL), rows(D_MODEL), rows(PLE_DIM), rows(LANES), pl.BlockSpec(memory_space=pl.ANY),
                  const((PLE_DIM, D_MODEL)), const((D_MODEL, D_MODEL)), const((1, D_MODEL)), const((1, D_MODEL))],
        out_specs=[rows(D_MODEL), rows(D_MODEL)],
        out_shape=[jax.ShapeDtypeStruct((n, D_MODEL), F32), jax.ShapeDtypeStruct((n, D_MODEL), BF16)],
        scratch_shapes=[pltpu.VMEM((TOP_K, tm, D_MODEL), F32), pltpu.SemaphoreType.DMA((1,))],
        compiler_params=_cparams(("arbitrary",)), name="moe_combine")(
            pos, hb, h, p, wsel, y_sorted, lw["w_ple"], lw["w_pleg"], lw["ln2_g"], lw["ln2_b"])


def _ffn_kernel(*refs, weighted, precise):
    it = iter(refs)
    hb_ref, h_ref, p_ref, comb_ref = next(it), next(it), next(it), next(it)
    wg, wu, wd, wp, wpg = [_take_w(it, precise) for _ in range(5)]
    lg_ref, lb_ref, o_ref, ob_ref, acc_sc = next(it), next(it), next(it), next(it), next(it)
    j = pl.program_id(1)
    hb = h_ref[...] if precise else hb_ref[...]

    @pl.when(j == 0)
    def _():
        ple = _mm(p_ref[...], _ld(wp), precise) * _sigmoid(_mm(hb, _ld(wpg), precise))
        acc_sc[...] = ALPHA * h_ref[...] + ple

    gate = _mm(hb, _ld(wg), precise)
    up = _mm(hb, _ld(wu), precise)
    out = _mm((gate * _sigmoid(gate)) * up, _ld(wd), precise)
    if weighted:
        comb = comb_ref[...]
        lane = lax.broadcasted_iota(jnp.int32, comb.shape, 1)
        out = jnp.sum(jnp.where(lane == j, comb, 0.0), axis=-1, keepdims=True) * out
    acc_sc[...] += out

    @pl.when(j == pl.num_programs(1) - 1)
    def _():
        y = _ln_rows(acc_sc[...], lg_ref[...], lb_ref[...])
        o_ref[...] = y
        ob_ref[...] = y.astype(BF16)


def _ffn_call(hb, h, p, comb, lw, tm, precise):
    n = h.shape[0]
    weighted = comb is not None
    if weighted:
        n_e = N_EXPERTS
        wg_spec = pl.BlockSpec((None, D_MODEL, D_FF_EXPERT), lambda i, j: (j, 0, 0))
        wd_spec = pl.BlockSpec((None, D_FF_EXPERT, D_MODEL), lambda i, j: (j, 0, 0))
    else:
        n_e = D_FF // D_FF_EXPERT
        comb = jnp.zeros((n, LANES), F32)
        wg_spec = pl.BlockSpec((D_MODEL, D_FF_EXPERT), lambda i, j: (0, j))
        wd_spec = pl.BlockSpec((D_FF_EXPERT, D_MODEL), lambda i, j: (j, 0))

    def rows(w):
        return pl.BlockSpec((tm, w), lambda i, j: (i, 0))

    def const(shape):
        return pl.BlockSpec(shape, lambda i, j: (0, 0))

    wspecs, wargs = [], []
    for name, spec in (("w_gate", wg_spec), ("w_up", wg_spec), ("w_down", wd_spec),
                       ("w_ple", const((PLE_DIM, D_MODEL))), ("w_pleg", const((D_MODEL, D_MODEL)))):
        ws = _wargs(lw, name, precise)
        wargs += ws
        wspecs += [spec] * len(ws)
    return pl.pallas_call(
        functools.partial(_ffn_kernel, weighted=weighted, precise=precise), grid=(n // tm, n_e),
        in_specs=[rows(D_MODEL), rows(D_MODEL), rows(PLE_DIM), rows(LANES)] + wspecs
        + [const((1, D_MODEL)), const((1, D_MODEL))],
        out_specs=[rows(D_MODEL), rows(D_MODEL)],
        out_shape=[jax.ShapeDtypeStruct((n, D_MODEL), F32), jax.ShapeDtypeStruct((n, D_MODEL), BF16)],
        scratch_shapes=[pltpu.VMEM((tm, D_MODEL), F32)],
        compiler_params=_cparams(("parallel", "arbitrary")), name="ffn")(
            hb, h, p, comb, *wargs, lw["ln2_g"], lw["ln2_b"])


def _layer_weights(i, W):
    w_in = W["w_in"][i]
    sizes = [D_AV, D_AV, D_B, D_B, D_B, FOX_HEADS, D_C, CONV_DIM, SSM_HEADS, N_BRANCH * D_MODEL]
    o = [0] + [int(s) for s in np.cumsum(sizes)]
    seg = [w_in[:, o[k]:o[k + 1]] for k in range(len(sizes))]
    w_u, w_v, w_q, w_k, w_va, w_f, w_z, w_xbc, w_dt, w_gate = seg
    pad = jnp.zeros((D_MODEL, LANES - SSM_HEADS - FOX_HEADS), F32)

    def lane_row(v, lane0=0):
        return jnp.pad(v.astype(F32), (lane0, LANES - lane0 - v.shape[0])).reshape(1, LANES)

    lw = dict(
        fox_b=lane_row(W["fox_b_forget"][i]),
        gmlp_g=W["gmlp_ln_g"][i], gmlp_b=W["gmlp_ln_b"][i], gmlp_w=W["gmlp_w_spatial"][i], gmlp_bs=W["gmlp_b_spatial"][i],
        conv_w=W["ssm_conv_w"][i], conv_b=W["ssm_conv_b"][i].reshape(1, CONV_DIM),
        dtb_row=lane_row(W["ssm_dt_bias"][i], DT_LANE0), dtb_col=W["ssm_dt_bias"][i].reshape(SSM_HEADS, 1),
        alog_row=lane_row(W["ssm_a_log"][i], DT_LANE0), alog_col=W["ssm_a_log"][i].reshape(SSM_HEADS, 1),
        alog_e=jnp.repeat(W["ssm_a_log"][i], SSM_HD).reshape(1, D_C),
        dvec=jnp.repeat(W["ssm_d"][i], SSM_HD).reshape(1, D_C), norm_g=W["ssm_norm_g"][i].reshape(1, D_C),
        ln1_g=W["ln1_g"][i].reshape(1, D_MODEL), ln1_b=W["ln1_b"][i].reshape(1, D_MODEL),
        ln2_g=W["ln2_g"][i].reshape(1, D_MODEL), ln2_b=W["ln2_b"][i].reshape(1, D_MODEL),
    )
    j = i // 2
    mats = dict(
        w_main=jnp.concatenate([w_xbc, w_z, w_u, w_v], axis=1), w_bgate=w_gate, w_k=w_k, w_va=w_va,
        w_small=jnp.concatenate([w_f, w_dt, pad], axis=1),
        w_qt=jnp.transpose(w_q), w_vt=jnp.transpose(w_va),
        w_a=W["w_branch_a"][i], w_b=W["w_branch_b"][i], w_c=W["w_branch_c"][i], w_o=W["w_out"][i],
        w_ple=W["ple_w_proj"][i], w_pleg=W["ple_w_gate"][i],
    )
    if i % 2 == 0:
        mats.update(w_gate=W["ffn_w_gate"][j], w_up=W["ffn_w_up"][j], w_down=W["ffn_w_down"][j])
    else:
        mats.update(w_gate=W["moe_w_gate"][j], w_up=W["moe_w_up"][j], w_down=W["moe_w_down"][j],
                    wr=jnp.pad(W["moe_w_router"][j], ((0, 0), (0, LANES - N_EXPERTS))))
        lw["br"] = lane_row(W["moe_b_router"][j])
    for name, w in mats.items():
        c = w * np.float32(2.0**16 + 1.0)
        hi_f = c - (c - w)
        lw[name] = hi_f.astype(BF16)
        lw[name + "_lo"] = (w - hi_f).astype(BF16)
    return lw


def _trunk_layer(x, xb, p, i, lw, bsz, t, fox_cache, ssm0, conv0, tm, precise):
    n = bsz * t
    act = _act_dtype(precise)
    xin = x if precise else xb
    tmm = min(tm, t)
    tbig = min(n, 2048)
    proj = _matmul_call(xin, lw, "w_main", tbig, MAIN_TN, precise, "in_proj")
    gates_sig = _matmul_call(xin, lw, "w_bgate", tbig, GATE_TN, precise, "in_proj_gates", gate=True)
    k_new, v_new, small, logf = _kvs_call(xin, lw, tm, precise)

    w_eff, bias_eff = _gmlp_weights(lw["gmlp_w"], lw["gmlp_bs"], t)
    y_a, v_n = _gmlp_call(proj, lw["gmlp_g"], lw["gmlp_b"], w_eff, bias_eff, min(n, 512), precise)

    k_new = k_new.reshape(bsz, t, D_B)
    v_new = v_new.reshape(bsz, t, FOX_HEADS, FOX_HD)
    logf = logf.reshape(bsz, t, FOX_HEADS)
    q_t, v_t = _qvt_call(xin, lw, tmm, bsz, t, precise)
    if fox_cache is None:
        k_all, logf_all, q_off = k_new, logf, 0
    else:
        k_c, v_c, logf_c = fox_cache
        q_off = k_c.shape[1]
        k_all = jnp.concatenate([k_c.reshape(bsz, q_off, D_B), k_new], axis=1)
        v_t = jnp.concatenate([jnp.transpose(v_c.reshape(bsz, q_off, D_B), (0, 2, 1)).astype(act), v_t], axis=2)
        logf_all = jnp.concatenate([logf_c, logf], axis=1)
    t_k = k_all.shape[1]
    k_a = _keys_call(k_all, logf_all, 512 if t_k % 512 == 0 else t_k, precise)
    tq, tk = (512, 512) if t % 512 == 0 else (t, t_k)
    y_bt = _fox_call(q_t, k_a, v_t, tq, tk, q_off, 4, precise)

    dt_t = jnp.transpose(small[:, DT_LANE0:DT_LANE0 + SSM_HEADS].reshape(bsz, t, SSM_HEADS), (0, 2, 1))
    y_c, h_fin, conv_new = _ssd_call(proj, small, dt_t, ssm0, conv0, lw, bsz, t, precise)

    x1, x1b = _merge_call(y_a, y_bt, y_c, gates_sig, x, lw, tmm, bsz, t, precise)
    if i % 2 == 0:
        x2, x2b = _ffn_call(x1b, x1, p, None, lw, tm, precise)
    else:
        comb, ids, wsel = _router_call(x1, lw, tm)
        if precise or n % MOE_TM or n < N_EXPERTS * MOE_ROWS:
            x2, x2b = _ffn_call(x1b, x1, p, comb, lw, tm, precise)
        else:
            src_tok, pos, tile_expert, n_used = _moe_plan(ids, n)
            y_sorted = _experts_call(x1, src_tok, tile_expert, n_used, lw)
            x2, x2b = _combine_call(x1b, x1, p, wsel, pos, y_sorted, lw)
    return x2, x2b, (k_new.reshape(bsz, t, FOX_HEADS, FOX_HD), v_new, logf, h_fin, conv_new,
                     v_n.reshape(bsz, t, D_AV))


def kernel(x_prompt, x_sample, p_prompt, p_sample, cache_fox_k, cache_fox_v, cache_fox_logf, state_ssm, state_conv, ln0_g, ln0_b, w_in, fox_b_forget, gmlp_ln_g, gmlp_ln_b, gmlp_w_spatial, gmlp_b_spatial, ssm_conv_w, ssm_conv_b, ssm_dt_bias, ssm_a_log, ssm_d, ssm_norm_g, w_branch_a, w_branch_b, w_branch_c, w_out, ln1_g, ln1_b, ln2_g, ln2_b, ffn_w_gate, ffn_w_up, ffn_w_down, moe_w_router, moe_b_router, moe_w_gate, moe_w_up, moe_w_down, ple_w_proj, ple_w_gate):
    W = dict(w_in=w_in, fox_b_forget=fox_b_forget, gmlp_ln_g=gmlp_ln_g, gmlp_ln_b=gmlp_ln_b,
             gmlp_w_spatial=gmlp_w_spatial, gmlp_b_spatial=gmlp_b_spatial, ssm_conv_w=ssm_conv_w,
             ssm_conv_b=ssm_conv_b, ssm_dt_bias=ssm_dt_bias, ssm_a_log=ssm_a_log, ssm_d=ssm_d,
             ssm_norm_g=ssm_norm_g, w_branch_a=w_branch_a, w_branch_b=w_branch_b,
             w_branch_c=w_branch_c, w_out=w_out, ln1_g=ln1_g, ln1_b=ln1_b, ln2_g=ln2_g, ln2_b=ln2_b,
             ffn_w_gate=ffn_w_gate, ffn_w_up=ffn_w_up, ffn_w_down=ffn_w_down,
             moe_w_router=moe_w_router, moe_b_router=moe_b_router, moe_w_gate=moe_w_gate, moe_w_up=moe_w_up,
             moe_w_down=moe_w_down, ple_w_proj=ple_w_proj, ple_w_gate=ple_w_gate)
    bp, tp, _ = x_prompt.shape
    bs, ts, _ = x_sample.shape
    tm_p = 512 if (bp * tp) % 512 == 0 else bp * tp
    tm_s = bs * ts
    xp, xpb = _layer_norm_call(x_prompt.reshape(bp * tp, D_MODEL), ln0_g, ln0_b, tm_p)
    xs, xsb = _layer_norm_call(x_sample.reshape(bs * ts, D_MODEL), ln0_g, ln0_b, tm_s)
    outs_p, outs_s = [], []
    for i in range(DEPTH):
        lw = _layer_weights(i, W)
        ssm0 = jnp.zeros((bp, SSM_HEADS, SSM_HD, SSM_N), F32)
        conv0 = jnp.zeros((bp, CONV_W - 1, CONV_DIM), F32)
        xp, xpb, st_p = _trunk_layer(xp, xpb, p_prompt[i].reshape(bp * tp, PLE_DIM), i, lw, bp, tp,
                                     None, ssm0, conv0, tm_p, False)
        xs, xsb, st_s = _trunk_layer(xs, xsb, p_sample[i].reshape(bs * ts, PLE_DIM), i, lw, bs, ts,
                                     (cache_fox_k[i], cache_fox_v[i], cache_fox_logf[i]),
                                     state_ssm[i], state_conv[i], tm_s, True)
        outs_p.append(st_p)
        outs_s.append(st_s)

    def stack(outs, k):
        return jnp.stack([o[k] for o in outs])

    return (xp.reshape(bp, tp, D_MODEL), xs.reshape(bs, ts, D_MODEL),
            stack(outs_p, 0), stack(outs_p, 1), stack(outs_p, 2), stack(outs_p, 3), stack(outs_p, 4),
            stack(outs_s, 0), stack(outs_s, 1), stack(outs_s, 2), stack(outs_s, 3), stack(outs_s, 4), stack(outs_s, 5))
```

```python
import functools
import math

import numpy as np
import jax
import jax.numpy as jnp
from jax import lax
from jax.experimental import pallas as pl
from jax.experimental.pallas import tpu as pltpu

F32 = jnp.float32
BF16 = jnp.bfloat16

D_MODEL = 1024
DEPTH = 2
CHUNK = 64
PLE_DIM = 256
GMLP_CHUNK = 128
GMLP_GROUPS = 4
D_AV = D_MODEL // 2
GMLP_GROUP_W = D_AV // GMLP_GROUPS
FOX_HD = 64
D_B = D_MODEL // 2
FOX_HEADS = D_B // FOX_HD
D_C = D_MODEL
SSM_HD = 64
SSM_HEADS = D_C // SSM_HD
SSM_N = 128
SSM_GROUPS = 2
CONV_W = 4
CONV_DIM = D_C + 2 * SSM_GROUPS * SSM_N
N_BRANCH = 3
D_FF = 11 * D_MODEL // 4
N_EXPERTS = 8
D_FF_EXPERT = D_FF // 2
ALPHA = (2.0 * DEPTH) ** 0.25
LN_EPS = 1e-5
RMS_EPS = 1e-5
NEG_INF = -1e30
LOG2E = math.log2(math.e)

LANES = 128
VMEM_LIMIT = 56 * 2**20
N_MAIN = CONV_DIM + D_C + 2 * D_AV
OFF_XBC, OFF_Z, OFF_U, OFF_V = 0, 1536, 2560, 3072
MAIN_TN = 512
GATE_TN = 1024
DT_LANE0 = FOX_HEADS
SSD_L = 128
CONV_PAD = 8
FOX_NBIAS = 3
FOX_DK = 128
FOX_DV = 80
FOX_QSUB = 256


def _cparams(sem):
    return pltpu.CompilerParams(dimension_semantics=sem, vmem_limit_bytes=VMEM_LIMIT)


def _sigmoid(x):
    return 1.0 / (1.0 + jnp.exp(-x))


def _softplus(x):
    return jnp.maximum(x, 0.0) + jnp.log1p(jnp.exp(-jnp.abs(x)))


def _gelu(x):
    c = np.float32(np.sqrt(2.0 / np.pi))
    return x * (0.5 * (1.0 + jnp.tanh(c * (x + 0.044715 * (x * x * x)))))


def _ln_rows(x, g, b):
    mu = jnp.mean(x, axis=-1, keepdims=True)
    xc = x - mu
    var = jnp.mean(xc * xc, axis=-1, keepdims=True)
    return xc * lax.rsqrt(var + LN_EPS) * g + b


def _bf16_part(x):
    u = lax.bitcast_convert_type(x, jnp.uint32) & jnp.uint32(0xFFFF0000)
    return lax.bitcast_convert_type(u, F32)


def _split2(x):
    hi = _bf16_part(x)
    return hi.astype(BF16), (x - hi).astype(BF16)


def _split3(x):
    hi = _bf16_part(x)
    r1 = x - hi
    mid = _bf16_part(r1)
    return hi.astype(BF16), mid.astype(BF16), (r1 - mid).astype(BF16)


def _dot(a, b):
    return jnp.dot(a, b, preferred_element_type=F32)


def _dot_nt(a, b):
    return lax.dot_general(a, b, (((1,), (1,)), ((), ())), preferred_element_type=F32)


def _dot_tn(a, b):
    return lax.dot_general(a, b, (((0,), (0,)), ((), ())), preferred_element_type=F32)


_DOTS = {"nn": _dot, "nt": _dot_nt, "tn": _dot_tn}


def _mm(a, b, precise, dims="nn"):
    dot = _DOTS[dims]
    if not precise:
        a = a[0] if isinstance(a, tuple) else a.astype(BF16)
        b = b[0] if isinstance(b, tuple) else b.astype(BF16)
        return dot(a, b)
    ah, al = a if isinstance(a, tuple) else _split2(a)
    bh, bl = b if isinstance(b, tuple) else _split2(b)
    return (dot(ah, bh) + dot(al, bh)) + dot(ah, bl)


def _take_w(it, precise):
    hi = next(it)
    return hi, (next(it) if precise else None)


def _ld(w, idx=...):
    return w[0][idx], (None if w[1] is None else w[1][idx])


def _wargs(lw, name, precise):
    return [lw[name], lw[name + "_lo"]] if precise else [lw[name]]


def _act_dtype(precise):
    return F32 if precise else BF16


def _ln_kernel(x_ref, g_ref, b_ref, o_ref, ob_ref):
    y = _ln_rows(x_ref[...], g_ref[...], b_ref[...])
    o_ref[...] = y
    ob_ref[...] = y.astype(BF16)


def _layer_norm_call(x, g, b, tm):
    n, d = x.shape
    row = pl.BlockSpec((tm, d), lambda i: (i, 0))
    vec = pl.BlockSpec((1, d), lambda i: (0, 0))
    return pl.pallas_call(
        _ln_kernel, grid=(n // tm,), in_specs=[row, vec, vec], out_specs=[row, row],
        out_shape=[jax.ShapeDtypeStruct((n, d), F32), jax.ShapeDtypeStruct((n, d), BF16)],
        compiler_params=_cparams(("parallel",)), name="ln_in")(x, g.reshape(1, d), b.reshape(1, d))


def _mm_kernel(*refs, precise, gate):
    it = iter(refs)
    x_ref = next(it)
    w = _take_w(it, precise)
    o_ref = next(it)
    y = _mm(x_ref[...], _ld(w), precise)
    o_ref[...] = (_sigmoid(y) if gate else y).astype(o_ref.dtype)


def _matmul_call(x, lw, wname, tm, tn, precise, name, gate=False):
    m, k = x.shape
    ws = _wargs(lw, wname, precise)
    n = ws[0].shape[1]
    wspec = pl.BlockSpec((k, tn), lambda i, j: (0, j))
    return pl.pallas_call(
        functools.partial(_mm_kernel, precise=precise, gate=gate), grid=(m // tm, n // tn),
        in_specs=[pl.BlockSpec((tm, k), lambda i, j: (i, 0))] + [wspec] * len(ws),
        out_specs=pl.BlockSpec((tm, tn), lambda i, j: (i, j)),
        out_shape=jax.ShapeDtypeStruct((m, n), _act_dtype(precise) if gate else F32),
        compiler_params=_cparams(("parallel", "parallel")), name=name)(x, *ws)


def _gmlp_kernel(*refs, rows, precise):
    it = iter(refs)
    u_ref, v_ref, g_ref, b_ref = next(it), next(it), next(it), next(it)
    w = _take_w(it, precise)
    bias_ref, ya_ref, vn_ref = next(it), next(it), next(it)
    u = _gelu(u_ref[...])
    vn = _ln_rows(_gelu(v_ref[...]), g_ref[...], b_ref[...])
    vn_ref[...] = vn
    vb = vn if precise else vn.astype(BF16)
    for c in range(rows // GMLP_CHUNK):
        r0 = c * GMLP_CHUNK
        for g in range(GMLP_GROUPS):
            c0 = g * GMLP_GROUP_W
            mixed = _mm(_ld(w, g), vb[r0:r0 + GMLP_CHUNK, c0:c0 + GMLP_GROUP_W], precise)
            mixed = mixed + bias_ref[:, c0:c0 + GMLP_GROUP_W]
            ya_ref[r0:r0 + GMLP_CHUNK, c0:c0 + GMLP_GROUP_W] = (
                u[r0:r0 + GMLP_CHUNK, c0:c0 + GMLP_GROUP_W] * mixed).astype(ya_ref.dtype)


def _gmlp_call(proj, ln_g, ln_b, w_eff, bias_eff, rows, precise):
    n = proj.shape[0]
    ublk, vblk = OFF_U // D_AV, OFF_V // D_AV
    vec = pl.BlockSpec((1, D_AV), lambda i: (0, 0))
    ws = list(_split2(w_eff)) if precise else [w_eff.astype(BF16)]
    wspec = pl.BlockSpec((GMLP_GROUPS, GMLP_CHUNK, GMLP_CHUNK), lambda i: (0, 0, 0))
    return pl.pallas_call(
        functools.partial(_gmlp_kernel, rows=rows, precise=precise), grid=(n // rows,),
        in_specs=[pl.BlockSpec((rows, D_AV), lambda i: (i, ublk)), pl.BlockSpec((rows, D_AV), lambda i: (i, vblk)),
                  vec, vec] + [wspec] * len(ws) + [pl.BlockSpec((GMLP_CHUNK, D_AV), lambda i: (0, 0))],
        out_specs=[pl.BlockSpec((rows, D_AV), lambda i: (i, 0)), pl.BlockSpec((rows, D_AV), lambda i: (i, 0))],
        out_shape=[jax.ShapeDtypeStruct((n, D_AV), _act_dtype(precise)), jax.ShapeDtypeStruct((n, D_AV), F32)],
        compiler_params=_cparams(("parallel",)), name="gmlp")(
            proj, proj, ln_g.reshape(1, D_AV), ln_b.reshape(1, D_AV), *ws, bias_eff)


def _gmlp_weights(w_s, b_s, t):
    l = min(GMLP_CHUNK, t)
    pos = np.arange(l)
    mask = (pos[None, :] // CHUNK) <= (pos[:, None] // CHUNK)
    w = jnp.where(mask[None], w_s[:, :l, :l], 0.0)
    bias = jnp.transpose(b_s[:, :l])
    reps = GMLP_CHUNK // l
    if reps > 1:
        eye = jnp.eye(reps, dtype=w.dtype)
        w = jnp.einsum("ab,gij->gaibj", eye, w).reshape(GMLP_GROUPS, GMLP_CHUNK, GMLP_CHUNK)
        bias = jnp.tile(bias, (reps, 1))
    return w, jnp.repeat(bias, GMLP_GROUP_W, axis=1)


def _kvs_kernel(*refs, precise):
    it = iter(refs)
    x_ref = next(it)
    wk, wv, ws = [_take_w(it, precise) for _ in range(3)]
    fb_ref, k_ref, v_ref, s_ref, lf_ref = [next(it) for _ in range(5)]
    x = x_ref[...]
    k_ref[...] = _mm(x, _ld(wk), precise)
    v_ref[...] = _mm(x, _ld(wv), precise)
    small = _mm(x, _ld(ws), precise)
    s_ref[...] = small
    lf_ref[...] = (-_softplus(-(small + fb_ref[...])))[:, :FOX_HEADS]


def _kvs_call(x, lw, tm, precise):
    n = x.shape[0]
    wargs, wspecs = [], []
    for name, width in (("w_k", D_B), ("w_va", D_B), ("w_small", LANES)):
        ws = _wargs(lw, name, precise)
        wargs += ws
        wspecs += [pl.BlockSpec((D_MODEL, width), lambda i: (0, 0))] * len(ws)

    def rows(w):
        return pl.BlockSpec((tm, w), lambda i: (i, 0))

    return pl.pallas_call(
        functools.partial(_kvs_kernel, precise=precise), grid=(n // tm,),
        in_specs=[rows(D_MODEL)] + wspecs + [pl.BlockSpec((1, LANES), lambda i: (0, 0))],
        out_specs=[rows(D_B), rows(D_B), rows(LANES), rows(FOX_HEADS)],
        out_shape=[jax.ShapeDtypeStruct((n, D_B), F32), jax.ShapeDtypeStruct((n, D_B), F32),
                   jax.ShapeDtypeStruct((n, LANES), F32), jax.ShapeDtypeStruct((n, FOX_HEADS), F32)],
        compiler_params=_cparams(("parallel",)), name="fox_kv")(x, *wargs, lw["fox_b"])


def _qvt_kernel(*refs, precise):
    it = iter(refs)
    x_ref = next(it)
    wq, wv = _take_w(it, precise), _take_w(it, precise)
    qt_ref, vt_ref = next(it), next(it)
    x = x_ref[...]
    qt_ref[...] = (_mm(_ld(wq), x, precise, "nt") * (FOX_HD ** -0.5 * LOG2E)).astype(qt_ref.dtype)
    vt_ref[...] = _mm(_ld(wv), x, precise, "nt").astype(vt_ref.dtype)


def _qvt_call(x, lw, tm, bsz, t, precise):
    n = x.shape[0]
    per_b = t // tm
    act = _act_dtype(precise)
    ws = _wargs(lw, "w_qt", precise) + _wargs(lw, "w_vt", precise)
    out = pl.BlockSpec((None, D_B, tm), lambda i: (i // per_b, 0, i % per_b))
    return pl.pallas_call(
        functools.partial(_qvt_kernel, precise=precise), grid=(n // tm,),
        in_specs=[pl.BlockSpec((tm, D_MODEL), lambda i: (i, 0))]
        + [pl.BlockSpec((D_B, D_MODEL), lambda i: (0, 0))] * len(ws),
        out_specs=[out, out],
        out_shape=[jax.ShapeDtypeStruct((bsz, D_B, t), act), jax.ShapeDtypeStruct((bsz, D_B, t), act)],
        compiler_params=_cparams(("parallel",)), name="fox_qv_t")(x, *ws)


def _place(x, sel, precise):
    if not precise:
        return _dot(x.astype(BF16), sel)
    hi, mid, lo = _split3(x)
    return (_dot(hi, sel) + _dot(mid, sel)) + _dot(lo, sel)


def _keys_kernel(k_ref, lf_ref, tril_ref, selk_ref, selg_ref, o_ref, carry_ref, *, precise):
    @pl.when(pl.program_id(1) == 0)
    def _():
        carry_ref[...] = jnp.zeros_like(carry_ref)

    tril = tril_ref[...]
    hi, mid, lo = _split3(lf_ref[...])
    f = (_dot(tril, hi) + _dot(tril, mid)) + _dot(tril, lo) + carry_ref[0:1, :]
    tc = f.shape[0]
    carry_ref[...] = jnp.broadcast_to(f[tc - 1:tc, :], carry_ref.shape)
    g = f * (-LOG2E)
    g1 = _bf16_part(g)
    r = g - g1
    g2 = _bf16_part(r)
    gcat = jnp.concatenate([g1, g2, r - g2], axis=1)
    ka = _place(k_ref[...], selk_ref[...], precise) + _place(gcat, selg_ref[...], precise)
    o_ref[...] = ka.astype(o_ref.dtype)


def _keys_call(k_all, logf_all, tc, precise):
    bsz, t_k, _ = k_all.shape
    sel_k = np.zeros((D_B, FOX_HEADS * FOX_DK), np.float32)
    sel_g = np.zeros((FOX_NBIAS * FOX_HEADS, FOX_HEADS * FOX_DK), np.float32)
    for h in range(FOX_HEADS):
        sel_k[h * FOX_HD + np.arange(FOX_HD), h * FOX_DK + np.arange(FOX_HD)] = 1.0
        for j in range(FOX_NBIAS):
            sel_g[j * FOX_HEADS + h, h * FOX_DK + FOX_HD + j] = 1.0
    tril = jnp.tril(jnp.ones((tc, tc), F32)).astype(BF16)
    return pl.pallas_call(
        functools.partial(_keys_kernel, precise=precise), grid=(bsz, t_k // tc),
        in_specs=[pl.BlockSpec((None, tc, D_B), lambda b, j: (b, j, 0)),
                  pl.BlockSpec((None, tc, FOX_HEADS), lambda b, j: (b, j, 0)),
                  pl.BlockSpec((tc, tc), lambda b, j: (0, 0)),
                  pl.BlockSpec(sel_k.shape, lambda b, j: (0, 0)), pl.BlockSpec(sel_g.shape, lambda b, j: (0, 0))],
        out_specs=pl.BlockSpec((None, tc, FOX_HEADS * FOX_DK), lambda b, j: (b, j, 0)),
        out_shape=jax.ShapeDtypeStruct((bsz, t_k, FOX_HEADS * FOX_DK), _act_dtype(precise)),
        scratch_shapes=[pltpu.VMEM((8, FOX_HEADS), F32)],
        compiler_params=_cparams(("parallel", "arbitrary")), name="fox_keys")(
            k_all, logf_all, tril, jnp.asarray(sel_k, BF16), jnp.asarray(sel_g, BF16))


def _fox_kernel(qt_ref, ka_ref, vt_ref, o_ref, qa_sc, m_sc, acc_sc, s_sc, mb_sc, p_sc, al_sc,
                *, tq, tk, qs, q_off, precise):
    i = pl.program_id(2)
    act = qa_sc.dtype
    nh = qa_sc.shape[0]
    rows = lax.broadcasted_iota(jnp.int32, (FOX_DK - FOX_HD, tq), 0)
    ones_rows = jnp.where(rows < FOX_NBIAS, 1.0, 0.0).astype(act)
    for h in range(nh):
        qa_sc[h] = jnp.concatenate([qt_ref[h * FOX_HD:(h + 1) * FOX_HD, :], ones_rows], axis=0)
    vrows = lax.broadcasted_iota(jnp.int32, (FOX_DV - FOX_HD, tk), 0)
    v_extra = jnp.where(vrows < 1, 1.0, 0.0).astype(act)
    m_sc[...] = jnp.full_like(m_sc, -jnp.inf)
    acc_sc[...] = jnp.zeros_like(acc_sc)
    first_q = q_off + i * tq
    subs = [(h, c, slice(c * qs, (c + 1) * qs)) for h in range(nh) for c in range(tq // qs)]

    def scores(k0):
        for h, _, sl in subs:
            ka = ka_ref[pl.ds(k0, tk), h * FOX_DK:(h + 1) * FOX_DK]
            s = _mm(ka, qa_sc[h, :, sl], precise)
            s_sc[h, :, sl] = s
            mb_sc[h, :, sl] = jnp.max(s, axis=0, keepdims=True)

    def consume(k0, masked):
        for h, c, sl in subs:
            vt = jnp.concatenate([vt_ref[h * FOX_HD:(h + 1) * FOX_HD, pl.ds(k0, tk)], v_extra], axis=0)
            s = s_sc[h, :, sl]
            if masked:
                kpos = k0 + lax.broadcasted_iota(jnp.int32, (tk, qs), 0)
                qpos = first_q + c * qs + lax.broadcasted_iota(jnp.int32, (tk, qs), 1)
                s = jnp.where(kpos <= qpos, s, NEG_INF)
                mb = jnp.max(s, axis=0, keepdims=True)
            else:
                mb = mb_sc[h, :, sl]
            m_prev = m_sc[h, :, sl]
            m_new = jnp.maximum(m_prev, mb)
            alpha = jnp.exp2(m_prev - m_new)
            p = jnp.exp2(s - m_new)
            acc_sc[h, :, sl] = alpha * acc_sc[h, :, sl] + _mm(vt, p, precise)
            m_sc[h, :, sl] = m_new

    def softmax_only(_k0):
        for h, _, sl in subs:
            m_prev = m_sc[h, :, sl]
            m_new = jnp.maximum(m_prev, mb_sc[h, :, sl])
            al_sc[h, :, sl] = jnp.exp2(m_prev - m_new)
            p_sc[h, :, sl] = jnp.exp2(s_sc[h, :, sl] - m_new).astype(p_sc.dtype)
            m_sc[h, :, sl] = m_new

    def values(k0):
        for h, _, sl in subs:
            vt = jnp.concatenate([vt_ref[h * FOX_HD:(h + 1) * FOX_HD, pl.ds(k0, tk)], v_extra], axis=0)
            acc_sc[h, :, sl] = al_sc[h, :, sl] * acc_sc[h, :, sl] + _mm(vt, p_sc[h, :, sl], precise)

    scores(0)
    if ka_ref.shape[0] == tk:
        consume(0, True)
    else:
        assert tq == tk and q_off == 0
        @pl.when(i > 0)
        def _():
            softmax_only(0)
            scores(tk)

        def body(j, c):
            k0 = pl.multiple_of(j * tk, tk)
            values(pl.multiple_of(k0 - tk, tk))
            softmax_only(k0)
            scores(pl.multiple_of(k0 + tk, tk))
            return c

        lax.fori_loop(1, i, body, 0)

        @pl.when(i > 0)
        def _():
            values(pl.multiple_of(i * tk - tk, tk))

        consume(pl.multiple_of(i * tk, tk), True)
    for h in range(nh):
        acc = acc_sc[h]
        o_ref[h * FOX_HD:(h + 1) * FOX_HD, :] = (acc[:FOX_HD] / acc[FOX_HD:FOX_HD + 1]).astype(o_ref.dtype)


def _fox_call(q_t, k_a, v_t, tq, tk, q_off, nh, precise):
    bsz, _, t_q = q_t.shape
    t_k = k_a.shape[1]
    qs = min(FOX_QSUB, tq)
    act = _act_dtype(precise)
    qo = pl.BlockSpec((None, nh * FOX_HD, tq), lambda b, hh, i: (b, hh, i))
    return pl.pallas_call(
        functools.partial(_fox_kernel, tq=tq, tk=tk, qs=qs, q_off=q_off, precise=precise),
        grid=(bsz, FOX_HEADS // nh, t_q // tq),
        in_specs=[qo, pl.BlockSpec((None, t_k, nh * FOX_DK), lambda b, hh, i: (b, 0, hh)),
                  pl.BlockSpec((None, nh * FOX_HD, t_k), lambda b, hh, i: (b, hh, 0))],
        out_specs=qo, out_shape=jax.ShapeDtypeStruct((bsz, D_B, t_q), act),
        scratch_shapes=[pltpu.VMEM((nh, FOX_DK, tq), act), pltpu.VMEM((nh, 1, tq), F32),
                        pltpu.VMEM((nh, FOX_DV, tq), F32), pltpu.VMEM((nh, tk, tq), F32),
                        pltpu.VMEM((nh, 1, tq), F32), pltpu.VMEM((nh, tk, tq), act), pltpu.VMEM((nh, 1, tq), F32)],
        compiler_params=_cparams(("parallel", "parallel", "arbitrary")), name="fox_attn")(q_t, k_a, v_t)


def _ssd_kernel(xbc_ref, z0_ref, z1_ref, dtp_ref, dtt_ref, h0_ref, c0_ref, cw_ref, cb_ref, dtb_row_ref, dtb_col_ref,
                alog_row_ref, alog_col_ref, aloge_ref, dvec_ref, ng_ref, tril_ref, triu_ref, expand_ref,
                yc_ref, hfin_ref, cnew_ref, h_sc, pad_sc, y_sc, *, L, precise):
    c = pl.program_id(1)

    @pl.when(c == 0)
    def _():
        h_sc[...] = h0_ref[...]
        pad_sc[CONV_PAD - (CONV_W - 1):CONV_PAD, :] = c0_ref[...]

    xin = xbc_ref[...]
    pad_sc[CONV_PAD:CONV_PAD + L, :] = xin
    y = cb_ref[...]
    for j in range(CONV_W - 1):
        r0 = CONV_PAD - (CONV_W - 1) + j
        y = y + pad_sc[r0:r0 + L, :] * cw_ref[j:j + 1, :]
    y = y + xin * cw_ref[CONV_W - 1:CONV_W, :]
    tail = pad_sc[CONV_PAD + L - (CONV_W - 1):CONV_PAD + L, :]
    pad_sc[CONV_PAD - (CONV_W - 1):CONV_PAD, :] = tail
    cnew_ref[...] = tail
    act = y * _sigmoid(y)

    dt_c = _softplus(dtp_ref[...] + dtb_row_ref[...])
    a_c = dt_c * (-jnp.exp(alog_row_ref[...]))
    tril = tril_ref[...]
    hi, mid, lo = _split3(a_c)
    acum_c = (_dot(tril, hi) + _dot(tril, mid)) + _dot(tril, lo)
    dt_r = _softplus(dtt_ref[...] + dtb_col_ref[...])
    a_r = dt_r * (-jnp.exp(alog_col_ref[...]))
    triu = triu_ref[...]
    hi, mid, lo = _split3(a_r)
    acum_r = (_dot(hi, triu) + _dot(mid, triu)) + _dot(lo, triu)

    dt_e = _place(dt_c, expand_ref[...], True)
    hi, mid, lo = _split3(dt_e * (-jnp.exp(aloge_ref[...])))
    acum_e = (_dot(tril, hi) + _dot(tril, mid)) + _dot(tril, lo)
    xs = act[:, :D_C]
    e_in = jnp.exp(acum_e)
    xw = xs * (jnp.exp(acum_e[L - 1:L, :] - acum_e) * dt_e)
    dx = dvec_ref[...] * xs

    row = lax.broadcasted_iota(jnp.int32, (L, L), 0)
    col = lax.broadcasted_iota(jnp.int32, (L, L), 1)
    causal = col <= row
    first_of_pair = lax.broadcasted_iota(jnp.int32, (L, 2 * SSM_HD), 1) < SSM_HD
    rep = SSM_HEADS // SSM_GROUPS
    gw = rep * SSM_HD
    for g in range(SSM_GROUPS):
        b0 = D_C + g * SSM_N
        c0 = D_C + SSM_GROUPS * SSM_N + g * SSM_N
        b_g = act[:, b0:b0 + SSM_N]
        c_g = act[:, c0:c0 + SSM_N]
        b_gs = _split2(b_g) if precise else (b_g.astype(BF16), None)
        cb = _mm(c_g, b_gs, precise, "nt")
        h_prev = h_sc[g * rep:(g + 1) * rep].reshape(gw, SSM_N)
        y_off = _mm(c_g, h_prev, precise, "nt") * e_in[:, g * gw:(g + 1) * gw]
        s_g = _mm(xw[:, g * gw:(g + 1) * gw], b_gs, precise, "tn")
        for hh in range(rep):
            h = g * rep + hh
            alast = acum_c[L - 1:L, DT_LANE0 + h:DT_LANE0 + h + 1]
            h_sc[h] = jnp.exp(alast) * h_sc[h] + s_g[hh * SSM_HD:(hh + 1) * SSM_HD, :]
        for pr in range(rep // 2):
            ch0 = g * gw + pr * 2 * SSM_HD
            x_pair = xs[:, ch0:ch0 + 2 * SSM_HD]
            x_pair = _split2(x_pair) if precise else (x_pair.astype(BF16), None)
            y_pair = []
            for hh in (2 * pr, 2 * pr + 1):
                h = g * rep + hh
                lane = DT_LANE0 + h
                decay = jnp.exp(jnp.where(causal, acum_c[:, lane:lane + 1] - acum_r[h:h + 1, :], -jnp.inf))
                y_pair.append(_mm((cb * decay) * dt_r[h:h + 1, :], x_pair, precise))
            y_diag = jnp.where(first_of_pair, y_pair[0], y_pair[1])
            y_sc[:, ch0:ch0 + 2 * SSM_HD] = (y_diag + y_off[:, ch0 - g * gw:ch0 - g * gw + 2 * SSM_HD]) + dx[:, ch0:ch0 + 2 * SSM_HD]

    @pl.when(c == pl.num_programs(1) - 1)
    def _():
        hfin_ref[...] = h_sc[...]

    z = jnp.concatenate([z0_ref[...], z1_ref[...]], axis=1)
    yg =y_sc[...] * (z * _sigmoid(z))
    ms = jnp.mean(yg * yg, axis=-1, keepdims=True)
    yc_ref[...] = ((yg * lax.rsqrt(ms + RMS_EPS)) * ng_ref[...]).astype(yc_ref.dtype)


def _ssd_call(proj, small, dt_t, h0, conv0, lw, bsz, t, precise):
    L = min(SSD_L, t)
    nc = t // L
    zhalf = D_C // 2
    zblk = OFF_Z // zhalf

    def const(shape):
        return pl.BlockSpec(shape, lambda b, c: (0,) * len(shape))

    in_specs = [
        pl.BlockSpec((L, CONV_DIM), lambda b, c: (b * nc + c, OFF_XBC // CONV_DIM)),
        pl.BlockSpec((L, zhalf), lambda b, c: (b * nc + c, zblk)),
        pl.BlockSpec((L, zhalf), lambda b, c: (b * nc + c, zblk + 1)),
        pl.BlockSpec((L, LANES), lambda b, c: (b * nc + c, 0)),
        pl.BlockSpec((None, SSM_HEADS, L), lambda b, c: (b, 0, c)),
        pl.BlockSpec((None, SSM_HEADS, SSM_HD, SSM_N), lambda b, c: (b, 0, 0, 0)),
        pl.BlockSpec((None, CONV_W - 1, CONV_DIM), lambda b, c: (b, 0, 0)),
        const((CONV_W, CONV_DIM)), const((1, CONV_DIM)), const((1, LANES)), const((SSM_HEADS, 1)),
        const((1, LANES)), const((SSM_HEADS, 1)), const((1, D_C)), const((1, D_C)), const((1, D_C)),
        const((L, L)), const((L, L)), const((LANES, D_C)),
    ]
    expand = np.zeros((LANES, D_C), np.float32)
    for h in range(SSM_HEADS):
        expand[DT_LANE0 + h, h * SSM_HD:(h + 1) * SSM_HD] = 1.0
    out_specs = [
        pl.BlockSpec((L, D_C), lambda b, c: (b * nc + c, 0)),
        pl.BlockSpec((None, SSM_HEADS, SSM_HD, SSM_N), lambda b, c: (b, 0, 0, 0)),
        pl.BlockSpec((None, CONV_W - 1, CONV_DIM), lambda b, c: (b, 0, 0)),
    ]
    out_shape = [jax.ShapeDtypeStruct((bsz * t, D_C), _act_dtype(precise)),
                 jax.ShapeDtypeStruct((bsz, SSM_HEADS, SSM_HD, SSM_N), F32),
                 jax.ShapeDtypeStruct((bsz, CONV_W - 1, CONV_DIM), F32)]
    ones = jnp.ones((L, L), F32)
    return pl.pallas_call(
        functools.partial(_ssd_kernel, L=L, precise=precise), grid=(bsz, nc), in_specs=in_specs,
        out_specs=out_specs, out_shape=out_shape,
        scratch_shapes=[pltpu.VMEM((SSM_HEADS, SSM_HD, SSM_N), F32), pltpu.VMEM((CONV_PAD + L, CONV_DIM), F32),
                        pltpu.VMEM((L, D_C), F32)],
        compiler_params=_cparams(("parallel", "arbitrary")), name="ssd")(
            proj, proj, proj, small, dt_t, h0, conv0, lw["conv_w"], lw["conv_b"], lw["dtb_row"], lw["dtb_col"],
            lw["alog_row"], lw["alog_col"], lw["alog_e"], lw["dvec"], lw["norm_g"],
            jnp.tril(ones).astype(BF16), jnp.triu(ones).astype(BF16), jnp.asarray(expand, BF16))


def _merge_kernel(*refs, precise):
    it = iter(refs)
    ya_ref, ybt_ref, yc_ref, g0_ref, g1_ref, g2_ref, x_ref = [next(it) for _ in range(7)]
    wa, wb, wc, wo = [_take_w(it, precise) for _ in range(4)]
    lg_ref, lb_ref, o_ref, ob_ref = next(it), next(it), next(it), next(it)
    merged = g0_ref[...].astype(F32) * _mm(ya_ref[...], _ld(wa), precise)
    merged = merged + g1_ref[...].astype(F32) * _mm(ybt_ref[...], _ld(wb), precise, "tn")
    merged = merged + g2_ref[...].astype(F32) * _mm(yc_ref[...], _ld(wc), precise)
    mix = _mm(merged, _ld(wo), precise)
    y = _ln_rows(ALPHA * x_ref[...] + mix, lg_ref[...], lb_ref[...])
    o_ref[...] = y
    ob_ref[...] = y.astype(BF16)


def _merge_call(ya, yb_t, yc, gates_sig, x, lw, tm, bsz, t, precise):
    n = x.shape[0]
    per_b = t // tm

    def rows(w):
        return pl.BlockSpec((tm, w), lambda i: (i, 0))

    def const(shape):
        return pl.BlockSpec(shape, lambda i: (0, 0))

    gates = [pl.BlockSpec((tm, D_MODEL), lambda i, j=j: (i, j)) for j in range(N_BRANCH)]
    wspecs, wargs = [], []
    for name, k in (("w_a", D_AV), ("w_b", D_B), ("w_c", D_C), ("w_o", D_MODEL)):
        ws = _wargs(lw, name, precise)
        wargs += ws
        wspecs += [const((k, D_MODEL))] * len(ws)
    return pl.pallas_call(
        functools.partial(_merge_kernel, precise=precise), grid=(n // tm,),
        in_specs=[rows(D_AV), pl.BlockSpec((None, D_B, tm), lambda i: (i // per_b, 0, i % per_b)), rows(D_C)]
        + gates + [rows(D_MODEL)] + wspecs + [const((1, D_MODEL)), const((1, D_MODEL))],
        out_specs=[rows(D_MODEL), rows(D_MODEL)],
        out_shape=[jax.ShapeDtypeStruct((n, D_MODEL), F32), jax.ShapeDtypeStruct((n, D_MODEL), BF16)],
        compiler_params=_cparams(("parallel",)), name="merge")(
            ya, yb_t, yc, gates_sig, gates_sig, gates_sig, x, *wargs, lw["ln1_g"], lw["ln1_b"])


def _router_kernel(h_ref, whi_ref, wlo_ref, br_ref, o_ref, id_ref, w_ref):
    logits = _mm(h_ref[...], (whi_ref[...], wlo_ref[...]), True) + br_ref[...]
    lane = lax.broadcasted_iota(jnp.int32, logits.shape, 1)
    lg = jnp.where(lane < N_EXPERTS, logits, -jnp.inf)
    m1 = jnp.max(lg, axis=-1, keepdims=True)
    i1 = jnp.min(jnp.where(lg == m1, lane, LANES), axis=-1, keepdims=True)
    lg2 = jnp.where(lane == i1, -jnp.inf, lg)
    m2 = jnp.max(lg2, axis=-1, keepdims=True)
    i2 = jnp.min(jnp.where(lg2 == m2, lane, LANES), axis=-1, keepdims=True)
    e = jnp.exp(m2 - m1)
    den = 1.0 + e
    w1, w2 = 1.0 / den, e / den
    o_ref[...] = jnp.where(lane == i1, w1, 0.0) + jnp.where(lane == i2, w2, 0.0)
    id_ref[...] = jnp.where(lane == 0, i1, jnp.where(lane == 1, i2, 0))
    w_ref[...] = jnp.where(lane == 0, w1, jnp.where(lane == 1, w2, 0.0))


def _router_call(h, lw, tm):
    n = h.shape[0]
    out = pl.BlockSpec((tm, LANES), lambda i: (i, 0))
    return pl.pallas_call(
        _router_kernel, grid=(n // tm,),
        in_specs=[pl.BlockSpec((tm, D_MODEL), lambda i: (i, 0)), pl.BlockSpec((D_MODEL, LANES), lambda i: (0, 0)),
                  pl.BlockSpec((D_MODEL, LANES), lambda i: (0, 0)), pl.BlockSpec((1, LANES), lambda i: (0, 0))],
        out_specs=[out, out, out],
        out_shape=[jax.ShapeDtypeStruct((n, LANES), F32), jax.ShapeDtypeStruct((n, LANES), jnp.int32),
                   jax.ShapeDtypeStruct((n, LANES), F32)],
        compiler_params=_cparams(("parallel",)), name="router")(h, lw["wr"], lw["wr_lo"], lw["br"])


MOE_ROWS = 256
MOE_TM = 256
TOP_K = 2


def _moe_plan(ids, n):
    r = MOE_ROWS
    e = ids[:, :TOP_K].reshape(-1)
    na = n * TOP_K
    n_tiles = na // r + N_EXPERTS
    order = jnp.argsort(e, stable=True).astype(jnp.int32)
    inv = jnp.argsort(order).astype(jnp.int32)
    onehot = (e[:, None] == jnp.arange(N_EXPERTS, dtype=jnp.int32)[None, :]).astype(jnp.int32)
    counts = jnp.sum(onehot, axis=0)
    padded = ((counts + r - 1) // r) * r
    ends = jnp.cumsum(padded)
    starts = ends - padded
    first = jnp.cumsum(counts) - counts
    shift = starts - first
    pos = inv + jnp.sum(onehot * shift[None, :], axis=1)
    tile_row0 = jnp.arange(n_tiles, dtype=jnp.int32) * r
    tile_expert = jnp.minimum(jnp.sum(ends[None, :] <= tile_row0[:, None], axis=1), N_EXPERTS - 1).astype(jnp.int32)
    last = (first + counts - 1)[tile_expert]
    srt = jnp.minimum(tile_row0[:, None] - shift[tile_expert][:, None] + jnp.arange(r, dtype=jnp.int32)[None, :],
                      last[:, None])
    src_tok = order[jnp.clip(srt, 0, na - 1)] // TOP_K
    n_used = (ends[-1] // r).astype(jnp.int32).reshape(1)
    return src_tok.reshape(n_tiles, 1, r), pos.reshape(n // MOE_TM, 1, MOE_TM * TOP_K), tile_expert, n_used


def _row_copy(src_hbm, dst_vmem, sem, src_row, dst_row):
    return pltpu.make_async_copy(src_hbm.at[pl.ds(src_row, 1), :], dst_vmem.at[pl.ds(dst_row, 1), :], sem)


def _experts_kernel(te_ref, nu_ref, src_ref, h_hbm, wg_ref, wu_ref, wd_ref, y_ref, x_buf, sem):
    i = pl.program_id(0)
    r = x_buf.shape[0]

    @pl.when(i < nu_ref[0])
    def _():
        def start(k, c):
            _row_copy(h_hbm, x_buf, sem.at[0], src_ref[0, k], k).start()
            return c

        def wait(k, c):
            _row_copy(h_hbm, x_buf, sem.at[0], 0, k).wait()
            return c

        lax.fori_loop(0, r, start, 0, unroll=8)
        lax.fori_loop(0, r, wait, 0, unroll=8)
        xb = x_buf[...].astype(BF16)
        gate = _dot(xb, wg_ref[...])
        up = _dot(xb, wu_ref[...])
        y_ref[...] = _dot(((gate * _sigmoid(gate)) * up).astype(BF16), wd_ref[...])

    @pl.when(i >= nu_ref[0])
    def _():
        y_ref[...] = jnp.zeros_like(y_ref)


def _experts_call(h, src_tok, tile_expert, n_used, lw):
    n_tiles, _, r = src_tok.shape
    wg_spec = pl.BlockSpec((None, D_MODEL, D_FF_EXPERT), lambda i, te, nu: (te[i], 0, 0))
    wd_spec = pl.BlockSpec((None, D_FF_EXPERT, D_MODEL), lambda i, te, nu: (te[i], 0, 0))
    grid_spec = pltpu.PrefetchScalarGridSpec(
        num_scalar_prefetch=2, grid=(n_tiles,),
        in_specs=[pl.BlockSpec((None, 1, r), lambda i, te, nu: (i, 0, 0), memory_space=pltpu.SMEM),
                  pl.BlockSpec(memory_space=pl.ANY), wg_spec, wg_spec, wd_spec],
        out_specs=pl.BlockSpec((r, D_MODEL), lambda i, te, nu: (i, 0)),
        scratch_shapes=[pltpu.VMEM((r, D_MODEL), F32), pltpu.SemaphoreType.DMA((1,))])
    return pl.pallas_call(
        _experts_kernel, grid_spec=grid_spec,
        out_shape=jax.ShapeDtypeStruct((n_tiles * r, D_MODEL), F32),
        compiler_params=_cparams(("arbitrary",)), name="moe_experts")(
            tile_expert, n_used, src_tok, h, lw["w_gate"], lw["w_up"], lw["w_down"])


def _combine_kernel(pos_ref, hb_ref, h_ref, p_ref, w_ref, y_hbm, wp_ref, wpg_ref, lg_ref, lb_ref,
                    o_ref, ob_ref, y_buf, sem):
    tm = h_ref.shape[0]

    def start(k, c):
        for j in range(TOP_K):
            _row_copy(y_hbm, y_buf.at[j], sem.at[0], pos_ref[0, TOP_K * k + j], k).start()
        return c

    def wait(k, c):
        for j in range(TOP_K):
            _row_copy(y_hbm, y_buf.at[j], sem.at[0], 0, k).wait()
        return c

    lax.fori_loop(0, tm, start, 0, unroll=8)
    hb = hb_ref[...]
    ple = _dot(p_ref[...].astype(BF16), wp_ref[...]) * _sigmoid(_dot(hb, wpg_ref[...]))
    acc = ALPHA * h_ref[...] + ple
    lax.fori_loop(0, tm, wait, 0, unroll=8)
    w = w_ref[...]
    for j in range(TOP_K):
        acc = acc + w[:, j:j + 1] * y_buf[j]
    y = _ln_rows(acc, lg_ref[...], lb_ref[...])
    o_ref[...] = y
    ob_ref[...] = y.astype(BF16)


def _combine_call(hb, h, p, wsel, pos, y_sorted, lw):
    n = h.shape[0]
    tm = MOE_TM

    def rows(w):
        return pl.BlockSpec((tm, w), lambda i: (i, 0))

    def const(shape):
        return pl.BlockSpec(shape, lambda i: (0, 0))

    return pl.pallas_call(
        _combine_kernel, grid=(n // tm,),
        in_specs=[pl.BlockSpec((None, 1, tm * TOP_K), lambda i: (i, 0, 0), memory_space=pltpu.SMEM),
                  rows(D_MODEL), rows(D_MODEL), rows(PLE_DIM), rows(LANES), pl.BlockSpec(memory_space=pl.ANY),
                  const((PLE_DIM, D_MODEL)), const((D_MODEL, D_MODEL)), const((1, D_MODEL)), const((1, D_MODEL))],
        out_specs=[rows(D_MODEL), rows(D_MODEL)],
        out_shape=[jax.ShapeDtypeStruct((n, D_MODEL), F32), jax.ShapeDtypeStruct((n, D_MODEL), BF16)],
        scratch_shapes=[pltpu.VMEM((TOP_K, tm, D_MODEL), F32), pltpu.SemaphoreType.DMA((1,))],
        compiler_params=_cparams(("arbitrary",)), name="moe_combine")(
            pos, hb, h, p, wsel, y_sorted, lw["w_ple"], lw["w_pleg"], lw["ln2_g"], lw["ln2_b"])


def _ffn_kernel(*refs, weighted, precise):
    it = iter(refs)
    hb_ref, h_ref, p_ref, comb_ref = next(it), next(it), next(it), next(it)
    wg, wu, wd, wp, wpg = [_take_w(it, precise) for _ in range(5)]
    lg_ref, lb_ref, o_ref, ob_ref, acc_sc = next(it), next(it), next(it), next(it), next(it)
    j = pl.program_id(1)
    hb = h_ref[...] if precise else hb_ref[...]

    @pl.when(j == 0)
    def _():
        ple = _mm(p_ref[...], _ld(wp), precise) * _sigmoid(_mm(hb, _ld(wpg), precise))
        acc_sc[...] = ALPHA * h_ref[...] + ple

    gate = _mm(hb, _ld(wg), precise)
    up = _mm(hb, _ld(wu), precise)
    out = _mm((gate * _sigmoid(gate)) * up, _ld(wd), precise)
    if weighted:
        comb = comb_ref[...]
        lane = lax.broadcasted_iota(jnp.int32, comb.shape, 1)
        out = jnp.sum(jnp.where(lane == j, comb, 0.0), axis=-1, keepdims=True) * out
    acc_sc[...] += out

    @pl.when(j == pl.num_programs(1) - 1)
    def _():
        y = _ln_rows(acc_sc[...], lg_ref[...], lb_ref[...])
        o_ref[...] = y
        ob_ref[...] = y.astype(BF16)


def _ffn_call(hb, h, p, comb, lw, tm, precise):
    n = h.shape[0]
    weighted = comb is not None
    if weighted:
        n_e = N_EXPERTS
        wg_spec = pl.BlockSpec((None, D_MODEL, D_FF_EXPERT), lambda i, j: (j, 0, 0))
        wd_spec = pl.BlockSpec((None, D_FF_EXPERT, D_MODEL), lambda i, j: (j, 0, 0))
    else:
        n_e = D_FF // D_FF_EXPERT
        comb = jnp.zeros((n, LANES), F32)
        wg_spec = pl.BlockSpec((D_MODEL, D_FF_EXPERT), lambda i, j: (0, j))
        wd_spec = pl.BlockSpec((D_FF_EXPERT, D_MODEL), lambda i, j: (j, 0))

    def rows(w):
        return pl.BlockSpec((tm, w), lambda i, j: (i, 0))

    def const(shape):
        return pl.BlockSpec(shape, lambda i, j: (0, 0))

    wspecs, wargs = [], []
    for name, spec in (("w_gate", wg_spec), ("w_up", wg_spec), ("w_down", wd_spec),
                       ("w_ple", const((PLE_DIM, D_MODEL))), ("w_pleg", const((D_MODEL, D_MODEL)))):
        ws = _wargs(lw, name, precise)
        wargs += ws
        wspecs += [spec] * len(ws)
    return pl.pallas_call(
        functools.partial(_ffn_kernel, weighted=weighted, precise=precise), grid=(n // tm, n_e),
        in_specs=[rows(D_MODEL), rows(D_MODEL), rows(PLE_DIM), rows(LANES)] + wspecs
        + [const((1, D_MODEL)), const((1, D_MODEL))],
        out_specs=[rows(D_MODEL), rows(D_MODEL)],
        out_shape=[jax.ShapeDtypeStruct((n, D_MODEL), F32), jax.ShapeDtypeStruct((n, D_MODEL), BF16)],
        scratch_shapes=[pltpu.VMEM((tm, D_MODEL), F32)],
        compiler_params=_cparams(("parallel", "arbitrary")), name="ffn")(
            hb, h, p, comb, *wargs, lw["ln2_g"], lw["ln2_b"])


def _layer_weights(i, W):
    w_in = W["w_in"][i]
    sizes = [D_AV, D_AV, D_B, D_B, D_B, FOX_HEADS, D_C, CONV_DIM, SSM_HEADS, N_BRANCH * D_MODEL]
    o = [0] + [int(s) for s in np.cumsum(sizes)]
    seg = [w_in[:, o[k]:o[k + 1]] for k in range(len(sizes))]
    w_u, w_v, w_q, w_k, w_va, w_f, w_z, w_xbc, w_dt, w_gate = seg
    pad = jnp.zeros((D_MODEL, LANES - SSM_HEADS - FOX_HEADS), F32)

    def lane_row(v, lane0=0):
        return jnp.pad(v.astype(F32), (lane0, LANES - lane0 - v.shape[0])).reshape(1, LANES)

    lw = dict(
        fox_b=lane_row(W["fox_b_forget"][i]),
        gmlp_g=W["gmlp_ln_g"][i], gmlp_b=W["gmlp_ln_b"][i], gmlp_w=W["gmlp_w_spatial"][i], gmlp_bs=W["gmlp_b_spatial"][i],
        conv_w=W["ssm_conv_w"][i], conv_b=W["ssm_conv_b"][i].reshape(1, CONV_DIM),
        dtb_row=lane_row(W["ssm_dt_bias"][i], DT_LANE0), dtb_col=W["ssm_dt_bias"][i].reshape(SSM_HEADS, 1),
        alog_row=lane_row(W["ssm_a_log"][i], DT_LANE0), alog_col=W["ssm_a_log"][i].reshape(SSM_HEADS, 1),
        alog_e=jnp.repeat(W["ssm_a_log"][i], SSM_HD).reshape(1, D_C),
        dvec=jnp.repeat(W["ssm_d"][i], SSM_HD).reshape(1, D_C), norm_g=W["ssm_norm_g"][i].reshape(1, D_C),
        ln1_g=W["ln1_g"][i].reshape(1, D_MODEL), ln1_b=W["ln1_b"][i].reshape(1, D_MODEL),
        ln2_g=W["ln2_g"][i].reshape(1, D_MODEL), ln2_b=W["ln2_b"][i].reshape(1, D_MODEL),
    )
    j = i // 2
    mats = dict(
        w_main=jnp.concatenate([w_xbc, w_z, w_u, w_v], axis=1), w_bgate=w_gate, w_k=w_k, w_va=w_va,
        w_small=jnp.concatenate([w_f, w_dt, pad], axis=1),
        w_qt=jnp.transpose(w_q), w_vt=jnp.transpose(w_va),
        w_a=W["w_branch_a"][i], w_b=W["w_branch_b"][i], w_c=W["w_branch_c"][i], w_o=W["w_out"][i],
        w_ple=W["ple_w_proj"][i], w_pleg=W["ple_w_gate"][i],
    )
    if i % 2 == 0:
        mats.update(w_gate=W["ffn_w_gate"][j], w_up=W["ffn_w_up"][j], w_down=W["ffn_w_down"][j])
    else:
        mats.update(w_gate=W["moe_w_gate"][j], w_up=W["moe_w_up"][j], w_down=W["moe_w_down"][j],
                    wr=jnp.pad(W["moe_w_router"][j], ((0, 0), (0, LANES - N_EXPERTS))))
        lw["br"] = lane_row(W["moe_b_router"][j])
    for name, w in mats.items():
        c = w * np.float32(2.0**16 + 1.0)
        hi_f = c - (c - w)
        lw[name] = hi_f.astype(BF16)
        lw[name + "_lo"] = (w - hi_f).astype(BF16)
    return lw


def _trunk_layer(x, xb, p, i, lw, bsz, t, fox_cache, ssm0, conv0, tm, precise):
    n = bsz * t
    act = _act_dtype(precise)
    xin = x if precise else xb
    tmm = min(tm, t)
    tbig = min(n, 2048)
    proj = _matmul_call(xin, lw, "w_main", tbig, MAIN_TN, precise, "in_proj")
    gates_sig = _matmul_call(xin, lw, "w_bgate", tbig, GATE_TN, precise, "in_proj_gates", gate=True)
    k_new, v_new, small, logf = _kvs_call(xin, lw, tm, precise)

    w_eff, bias_eff = _gmlp_weights(lw["gmlp_w"], lw["gmlp_bs"], t)
    y_a, v_n = _gmlp_call(proj, lw["gmlp_g"], lw["gmlp_b"], w_eff, bias_eff, min(n, 512), precise)

    k_new = k_new.reshape(bsz, t, D_B)
    v_new = v_new.reshape(bsz, t, FOX_HEADS, FOX_HD)
    logf = logf.reshape(bsz, t, FOX_HEADS)
    q_t, v_t = _qvt_call(xin, lw, tmm, bsz, t, precise)
    if fox_cache is None:
        k_all, logf_all, q_off = k_new, logf, 0
    else:
        k_c, v_c, logf_c = fox_cache
        q_off = k_c.shape[1]
        k_all = jnp.concatenate([k_c.reshape(bsz, q_off, D_B), k_new], axis=1)
        v_t = jnp.concatenate([jnp.transpose(v_c.reshape(bsz, q_off, D_B), (0, 2, 1)).astype(act), v_t], axis=2)
        logf_all = jnp.concatenate([logf_c, logf], axis=1)
    t_k = k_all.shape[1]
    k_a = _keys_call(k_all, logf_all, 512 if t_k % 512 == 0 else t_k, precise)
    tq, tk = (512, 512) if t % 512 == 0 else (t, t_k)
    y_bt = _fox_call(q_t, k_a, v_t, tq, tk, q_off, 4, precise)

    dt_t = jnp.transpose(small[:, DT_LANE0:DT_LANE0 + SSM_HEADS].reshape(bsz, t, SSM_HEADS), (0, 2, 1))
    y_c, h_fin, conv_new = _ssd_call(proj, small, dt_t, ssm0, conv0, lw, bsz, t, precise)

    x1, x1b = _merge_call(y_a, y_bt, y_c, gates_sig, x, lw, tmm, bsz, t, precise)
    if i % 2 == 0:
        x2, x2b = _ffn_call(x1b, x1, p, None, lw, tm, precise)
    else:
        comb, ids, wsel = _router_call(x1, lw, tm)
        if precise or n % MOE_TM or n < N_EXPERTS * MOE_ROWS:
            x2, x2b = _ffn_call(x1b, x1, p, comb, lw, tm, precise)
        else:
            src_tok, pos, tile_expert, n_used = _moe_plan(ids, n)
            y_sorted = _experts_call(x1, src_tok, tile_expert, n_used, lw)
            x2, x2b = _combine_call(x1b, x1, p, wsel, pos, y_sorted, lw)
    return x2, x2b, (k_new.reshape(bsz, t, FOX_HEADS, FOX_HD), v_new, logf, h_fin, conv_new,
                     v_n.reshape(bsz, t, D_AV))


def kernel(x_prompt, x_sample, p_prompt, p_sample, cache_fox_k, cache_fox_v, cache_fox_logf, state_ssm, state_conv, ln0_g, ln0_b, w_in, fox_b_forget, gmlp_ln_g, gmlp_ln_b, gmlp_w_spatial, gmlp_b_spatial, ssm_conv_w, ssm_conv_b, ssm_dt_bias, ssm_a_log, ssm_d, ssm_norm_g, w_branch_a, w_branch_b, w_branch_c, w_out, ln1_g, ln1_b, ln2_g, ln2_b, ffn_w_gate, ffn_w_up, ffn_w_down, moe_w_router, moe_b_router, moe_w_gate, moe_w_up, moe_w_down, ple_w_proj, ple_w_gate):
    W = dict(w_in=w_in, fox_b_forget=fox_b_forget, gmlp_ln_g=gmlp_ln_g, gmlp_ln_b=gmlp_ln_b,
             gmlp_w_spatial=gmlp_w_spatial, gmlp_b_spatial=gmlp_b_spatial, ssm_conv_w=ssm_conv_w,
             ssm_conv_b=ssm_conv_b, ssm_dt_bias=ssm_dt_bias, ssm_a_log=ssm_a_log, ssm_d=ssm_d,
             ssm_norm_g=ssm_norm_g, w_branch_a=w_branch_a, w_branch_b=w_branch_b,
             w_branch_c=w_branch_c, w_out=w_out, ln1_g=ln1_g, ln1_b=ln1_b, ln2_g=ln2_g, ln2_b=ln2_b,
             ffn_w_gate=ffn_w_gate, ffn_w_up=ffn_w_up, ffn_w_down=ffn_w_down,
             moe_w_router=moe_w_router, moe_b_router=moe_b_router, moe_w_gate=moe_w_gate, moe_w_up=moe_w_up,
             moe_w_down=moe_w_down, ple_w_proj=ple_w_proj, ple_w_gate=ple_w_gate)
    bp, tp, _ = x_prompt.shape
    bs, ts, _ = x_sample.shape
    tm_p = 512 if (bp * tp) % 512 == 0 else bp * tp
    tm_s = bs * ts
    xp, xpb = _layer_norm_call(x_prompt.reshape(bp * tp, D_MODEL), ln0_g, ln0_b, tm_p)
    xs, xsb = _layer_norm_call(x_sample.reshape(bs * ts, D_MODEL), ln0_g, ln0_b, tm_s)
    outs_p, outs_s = [], []
    for i in range(DEPTH):
        lw = _layer_weights(i, W)
        ssm0 = jnp.zeros((bp, SSM_HEADS, SSM_HD, SSM_N), F32)
        conv0 = jnp.zeros((bp, CONV_W - 1, CONV_DIM), F32)
        xp, xpb, st_p = _trunk_layer(xp, xpb, p_prompt[i].reshape(bp * tp, PLE_DIM), i, lw, bp, tp,
                                     None, ssm0, conv0, tm_p, False)
        xs, xsb, st_s = _trunk_layer(xs, xsb, p_sample[i].reshape(bs * ts, PLE_DIM), i, lw, bs, ts,
                                     (cache_fox_k[i], cache_fox_v[i], cache_fox_logf[i]),
                                     state_ssm[i], state_conv[i], tm_s, True)
        outs_p.append(st_p)
        outs_s.append(st_s)

    def stack(outs, k):
        return jnp.stack([o[k] for o in outs])

    return (xp.reshape(bp, tp, D_MODEL), xs.reshape(bs, ts, D_MODEL),
            stack(outs_p, 0), stack(outs_p, 1), stack(outs_p, 2), stack(outs_p, 3), stack(outs_p, 4),
            stack(outs_s, 0), stack(outs_s, 1), stack(outs_s, 2), stack(outs_s, 3), stack(outs_s, 4), stack(outs_s, 5))
```

```python
import functools
import math

import numpy as np
import jax
import jax.numpy as jnp
from jax import lax
from jax.experimental import pallas as pl
from jax.experimental.pallas import tpu as pltpu

F32 = jnp.float32
BF16 = jnp.bfloat16

D_MODEL = 1024
DEPTH = 2
CHUNK = 64
PLE_DIM = 256
GMLP_CHUNK = 128
GMLP_GROUPS = 4
D_AV = D_MODEL // 2
GMLP_GROUP_W = D_AV // GMLP_GROUPS
FOX_HD = 64
D_B = D_MODEL // 2
FOX_HEADS = D_B // FOX_HD
D_C = D_MODEL
SSM_HD = 64
SSM_HEADS = D_C // SSM_HD
SSM_N = 128
SSM_GROUPS = 2
CONV_W = 4
CONV_DIM = D_C + 2 * SSM_GROUPS * SSM_N
N_BRANCH = 3
D_FF = 11 * D_MODEL // 4
N_EXPERTS = 8
D_FF_EXPERT = D_FF // 2
ALPHA = (2.0 * DEPTH) ** 0.25
LN_EPS = 1e-5
RMS_EPS = 1e-5
NEG_INF = -1e30
LOG2E = math.log2(math.e)

LANES = 128
VMEM_LIMIT = 56 * 2**20
N_MAIN = CONV_DIM + D_C + 2 * D_AV
OFF_XBC, OFF_Z, OFF_U, OFF_V = 0, 1536, 2560, 3072
MAIN_TN = 512
GATE_TN = 1024
DT_LANE0 = FOX_HEADS
SSD_L = 128
CONV_PAD = 8
FOX_NBIAS = 3
FOX_DK = 128
FOX_DV = 80
FOX_QSUB = 256
FOX_TILE = 512
FOX_NH = 4
TOKEN_TM = 512
PROJ_TM = 2048


def _cparams(sem):
    return pltpu.CompilerParams(dimension_semantics=sem, vmem_limit_bytes=VMEM_LIMIT)


def _sigmoid(x):
    return 1.0 / (1.0 + jnp.exp(-x))


def _softplus(x):
    return jnp.maximum(x, 0.0) + jnp.log1p(jnp.exp(-jnp.abs(x)))


def _gelu(x):
    c = np.float32(np.sqrt(2.0 / np.pi))
    return x * (0.5 * (1.0 + jnp.tanh(c * (x + 0.044715 * (x * x * x)))))


def _ln_rows(x, g, b):
    mu = jnp.mean(x, axis=-1, keepdims=True)
    xc = x - mu
    var = jnp.mean(xc * xc, axis=-1, keepdims=True)
    return xc * lax.rsqrt(var + LN_EPS) * g + b


def _bf16_part(x):
    u = lax.bitcast_convert_type(x, jnp.uint32) & jnp.uint32(0xFFFF0000)
    return lax.bitcast_convert_type(u, F32)


def _split2(x):
    hi = _bf16_part(x)
    return hi.astype(BF16), (x - hi).astype(BF16)


def _split3(x):
    hi = _bf16_part(x)
    r1 = x - hi
    mid = _bf16_part(r1)
    return hi.astype(BF16), mid.astype(BF16), (r1 - mid).astype(BF16)


def _dot(a, b):
    return jnp.dot(a, b, preferred_element_type=F32)


def _dot_nt(a, b):
    return lax.dot_general(a, b, (((1,), (1,)), ((), ())), preferred_element_type=F32)


def _dot_tn(a, b):
    return lax.dot_general(a, b, (((0,), (0,)), ((), ())), preferred_element_type=F32)


_DOTS = {"nn": _dot, "nt": _dot_nt, "tn": _dot_tn}


def _mm(a, b, precise, dims="nn"):
    dot = _DOTS[dims]
    if not precise:
        a = a[0] if isinstance(a, tuple) else a.astype(BF16)
        b = b[0] if isinstance(b, tuple) else b.astype(BF16)
        return dot(a, b)
    ah, al = a if isinstance(a, tuple) else _split2(a)
    bh, bl = b if isinstance(b, tuple) else _split2(b)
    return (dot(ah, bh) + dot(al, bh)) + dot(ah, bl)


def _take_w(it, precise):
    hi = next(it)
    return hi, (next(it) if precise else None)


def _ld(w, idx=...):
    return w[0][idx], (None if w[1] is None else w[1][idx])


def _wargs(lw, name, precise):
    return [lw[name], lw[name + "_lo"]] if precise else [lw[name]]


def _act_dtype(precise):
    return F32 if precise else BF16


def _ln_kernel(x_ref, g_ref, b_ref, o_ref, ob_ref):
    y = _ln_rows(x_ref[...], g_ref[...], b_ref[...])
    o_ref[...] = y
    ob_ref[...] = y.astype(BF16)


def _layer_norm_call(x, g, b, tm):
    n, d = x.shape
    row = pl.BlockSpec((tm, d), lambda i: (i, 0))
    vec = pl.BlockSpec((1, d), lambda i: (0, 0))
    return pl.pallas_call(
        _ln_kernel, grid=(n // tm,), in_specs=[row, vec, vec], out_specs=[row, row],
        out_shape=[jax.ShapeDtypeStruct((n, d), F32), jax.ShapeDtypeStruct((n, d), BF16)],
        compiler_params=_cparams(("parallel",)), name="ln_in")(x, g.reshape(1, d), b.reshape(1, d))


def _mm_kernel(*refs, precise, gate):
    it = iter(refs)
    x_ref = next(it)
    w = _take_w(it, precise)
    o_ref = next(it)
    y = _mm(x_ref[...], _ld(w), precise)
    o_ref[...] = (_sigmoid(y) if gate else y).astype(o_ref.dtype)


def _matmul_call(x, lw, wname, tm, tn, precise, name, gate=False):
    m, k = x.shape
    ws = _wargs(lw, wname, precise)
    n = ws[0].shape[1]
    wspec = pl.BlockSpec((k, tn), lambda i, j: (0, j))
    return pl.pallas_call(
        functools.partial(_mm_kernel, precise=precise, gate=gate), grid=(m // tm, n // tn),
        in_specs=[pl.BlockSpec((tm, k), lambda i, j: (i, 0))] + [wspec] * len(ws),
        out_specs=pl.BlockSpec((tm, tn), lambda i, j: (i, j)),
        out_shape=jax.ShapeDtypeStruct((m, n), _act_dtype(precise) if gate else F32),
        compiler_params=_cparams(("parallel", "parallel")), name=name)(x, *ws)


def _gmlp_kernel(*refs, rows, precise):
    it = iter(refs)
    u_ref, v_ref, g_ref, b_ref = next(it), next(it), next(it), next(it)
    w = _take_w(it, precise)
    bias_ref, ya_ref, vn_ref = next(it), next(it), next(it)
    u = _gelu(u_ref[...])
    vn = _ln_rows(_gelu(v_ref[...]), g_ref[...], b_ref[...])
    vn_ref[...] = vn
    vb = vn if precise else vn.astype(BF16)
    for c in range(rows // GMLP_CHUNK):
        r0 = c * GMLP_CHUNK
        for g in range(GMLP_GROUPS):
            c0 = g * GMLP_GROUP_W
            mixed = _mm(_ld(w, g), vb[r0:r0 + GMLP_CHUNK, c0:c0 + GMLP_GROUP_W], precise)
            mixed = mixed + bias_ref[:, c0:c0 + GMLP_GROUP_W]
            ya_ref[r0:r0 + GMLP_CHUNK, c0:c0 + GMLP_GROUP_W] = (
                u[r0:r0 + GMLP_CHUNK, c0:c0 + GMLP_GROUP_W] * mixed).astype(ya_ref.dtype)


def _gmlp_call(proj, ln_g, ln_b, w_eff, bias_eff, rows, precise):
    n = proj.shape[0]
    ublk, vblk = OFF_U // D_AV, OFF_V // D_AV
    vec = pl.BlockSpec((1, D_AV), lambda i: (0, 0))
    ws = list(_split2(w_eff)) if precise else [w_eff.astype(BF16)]
    wspec = pl.BlockSpec((GMLP_GROUPS, GMLP_CHUNK, GMLP_CHUNK), lambda i: (0, 0, 0))
    return pl.pallas_call(
        functools.partial(_gmlp_kernel, rows=rows, precise=precise), grid=(n // rows,),
        in_specs=[pl.BlockSpec((rows, D_AV), lambda i: (i, ublk)), pl.BlockSpec((rows, D_AV), lambda i: (i, vblk)),
                  vec, vec] + [wspec] * len(ws) + [pl.BlockSpec((GMLP_CHUNK, D_AV), lambda i: (0, 0))],
        out_specs=[pl.BlockSpec((rows, D_AV), lambda i: (i, 0)), pl.BlockSpec((rows, D_AV), lambda i: (i, 0))],
        out_shape=[jax.ShapeDtypeStruct((n, D_AV), _act_dtype(precise)), jax.ShapeDtypeStruct((n, D_AV), F32)],
        compiler_params=_cparams(("parallel",)), name="gmlp")(
            proj, proj, ln_g.reshape(1, D_AV), ln_b.reshape(1, D_AV), *ws, bias_eff)


def _gmlp_weights(w_s, b_s, t):
    l = min(GMLP_CHUNK, t)
    pos = np.arange(l)
    mask = (pos[None, :] // CHUNK) <= (pos[:, None] // CHUNK)
    w = jnp.where(mask[None], w_s[:, :l, :l], 0.0)
    bias = jnp.transpose(b_s[:, :l])
    reps = GMLP_CHUNK // l
    if reps > 1:
        eye = jnp.eye(reps, dtype=w.dtype)
        w = jnp.einsum("ab,gij->gaibj", eye, w).reshape(GMLP_GROUPS, GMLP_CHUNK, GMLP_CHUNK)
        bias = jnp.tile(bias, (reps, 1))
    return w, jnp.repeat(bias, GMLP_GROUP_W, axis=1)


def _kvs_kernel(*refs, precise):
    it = iter(refs)
    x_ref = next(it)
    wk, wv, ws = [_take_w(it, precise) for _ in range(3)]
    fb_ref, k_ref, v_ref, s_ref, lf_ref = [next(it) for _ in range(5)]
    x = x_ref[...]
    k_ref[...] = _mm(x, _ld(wk), precise)
    v_ref[...] = _mm(x, _ld(wv), precise)
    small = _mm(x, _ld(ws), precise)
    s_ref[...] = small
    lf_ref[...] = (-_softplus(-(small + fb_ref[...])))[:, :FOX_HEADS]


def _kvs_call(x, lw, tm, precise):
    n = x.shape[0]
    wargs, wspecs = [], []
    for name, width in (("w_k", D_B), ("w_va", D_B), ("w_small", LANES)):
        ws = _wargs(lw, name, precise)
        wargs += ws
        wspecs += [pl.BlockSpec((D_MODEL, width), lambda i: (0, 0))] * len(ws)

    def rows(w):
        return pl.BlockSpec((tm, w), lambda i: (i, 0))

    return pl.pallas_call(
        functools.partial(_kvs_kernel, precise=precise), grid=(n // tm,),
        in_specs=[rows(D_MODEL)] + wspecs + [pl.BlockSpec((1, LANES), lambda i: (0, 0))],
        out_specs=[rows(D_B), rows(D_B), rows(LANES), rows(FOX_HEADS)],
        out_shape=[jax.ShapeDtypeStruct((n, D_B), F32), jax.ShapeDtypeStruct((n, D_B), F32),
                   jax.ShapeDtypeStruct((n, LANES), F32), jax.ShapeDtypeStruct((n, FOX_HEADS), F32)],
        compiler_params=_cparams(("parallel",)), name="fox_kv")(x, *wargs, lw["fox_b"])


def _qvt_kernel(*refs, precise):
    it = iter(refs)
    x_ref = next(it)
    wq, wv = _take_w(it, precise), _take_w(it, precise)
    qt_ref, vt_ref = next(it), next(it)
    x = x_ref[...]
    qt_ref[...] = (_mm(_ld(wq), x, precise, "nt") * (FOX_HD ** -0.5 * LOG2E)).astype(qt_ref.dtype)
    vt_ref[...] = _mm(_ld(wv), x, precise, "nt").astype(vt_ref.dtype)


def _qvt_call(x, lw, tm, bsz, t, precise):
    n = x.shape[0]
    per_b = t // tm
    act = _act_dtype(precise)
    ws = _wargs(lw, "w_qt", precise) + _wargs(lw, "w_vt", precise)
    out = pl.BlockSpec((None, D_B, tm), lambda i: (i // per_b, 0, i % per_b))
    return pl.pallas_call(
        functools.partial(_qvt_kernel, precise=precise), grid=(n // tm,),
        in_specs=[pl.BlockSpec((tm, D_MODEL), lambda i: (i, 0))]
        + [pl.BlockSpec((D_B, D_MODEL), lambda i: (0, 0))] * len(ws),
        out_specs=[out, out],
        out_shape=[jax.ShapeDtypeStruct((bsz, D_B, t), act), jax.ShapeDtypeStruct((bsz, D_B, t), act)],
        compiler_params=_cparams(("parallel",)), name="fox_qv_t")(x, *ws)


def _place(x, sel, precise):
    if not precise:
        return _dot(x.astype(BF16), sel)
    hi, mid, lo = _split3(x)
    return (_dot(hi, sel) + _dot(mid, sel)) + _dot(lo, sel)


def _keys_kernel(k_ref, lf_ref, tril_ref, selk_ref, selg_ref, o_ref, carry_ref, *, precise):
    @pl.when(pl.program_id(1) == 0)
    def _():
        carry_ref[...] = jnp.zeros_like(carry_ref)

    tril = tril_ref[...]
    hi, mid, lo = _split3(lf_ref[...])
    f = (_dot(tril, hi) + _dot(tril, mid)) + _dot(tril, lo) + carry_ref[0:1, :]
    tc = f.shape[0]
    carry_ref[...] = jnp.broadcast_to(f[tc - 1:tc, :], carry_ref.shape)
    g = f * (-LOG2E)
    g1 = _bf16_part(g)
    r = g - g1
    g2 = _bf16_part(r)
    gcat = jnp.concatenate([g1, g2, r - g2], axis=1)
    ka = _place(k_ref[...], selk_ref[...], precise) + _place(gcat, selg_ref[...], precise)
    o_ref[...] = ka.astype(o_ref.dtype)


def _keys_call(k_all, logf_all, tc, precise):
    bsz, t_k, _ = k_all.shape
    sel_k = np.zeros((D_B, FOX_HEADS * FOX_DK), np.float32)
    sel_g = np.zeros((FOX_NBIAS * FOX_HEADS, FOX_HEADS * FOX_DK), np.float32)
    for h in range(FOX_HEADS):
        sel_k[h * FOX_HD + np.arange(FOX_HD), h * FOX_DK + np.arange(FOX_HD)] = 1.0
        for j in range(FOX_NBIAS):
            sel_g[j * FOX_HEADS + h, h * FOX_DK + FOX_HD + j] = 1.0
    tril = jnp.tril(jnp.ones((tc, tc), F32)).astype(BF16)
    return pl.pallas_call(
        functools.partial(_keys_kernel, precise=precise), grid=(bsz, t_k // tc),
        in_specs=[pl.BlockSpec((None, tc, D_B), lambda b, j: (b, j, 0)),
                  pl.BlockSpec((None, tc, FOX_HEADS), lambda b, j: (b, j, 0)),
                  pl.BlockSpec((tc, tc), lambda b, j: (0, 0)),
                  pl.BlockSpec(sel_k.shape, lambda b, j: (0, 0)), pl.BlockSpec(sel_g.shape, lambda b, j: (0, 0))],
        out_specs=pl.BlockSpec((None, tc, FOX_HEADS * FOX_DK), lambda b, j: (b, j, 0)),
        out_shape=jax.ShapeDtypeStruct((bsz, t_k, FOX_HEADS * FOX_DK), _act_dtype(precise)),
        scratch_shapes=[pltpu.VMEM((8, FOX_HEADS), F32)],
        compiler_params=_cparams(("parallel", "arbitrary")), name="fox_keys")(
            k_all, logf_all, tril, jnp.asarray(sel_k, BF16), jnp.asarray(sel_g, BF16))


def _fox_kernel(qt_ref, ka_ref, vt_ref, o_ref, qa_sc, m_sc, acc_sc, s_sc, mb_sc, p_sc, al_sc,
                *, tq, tk, qs, q_off, precise):
    i = pl.program_id(2)
    act = qa_sc.dtype
    nh = qa_sc.shape[0]
    rows = lax.broadcasted_iota(jnp.int32, (FOX_DK - FOX_HD, tq), 0)
    ones_rows = jnp.where(rows < FOX_NBIAS, 1.0, 0.0).astype(act)
    for h in range(nh):
        qa_sc[h] = jnp.concatenate([qt_ref[h * FOX_HD:(h + 1) * FOX_HD, :], ones_rows], axis=0)
    vrows = lax.broadcasted_iota(jnp.int32, (FOX_DV - FOX_HD, tk), 0)
    v_extra = jnp.where(vrows < 1, 1.0, 0.0).astype(act)
    m_sc[...] = jnp.full_like(m_sc, -jnp.inf)
    acc_sc[...] = jnp.zeros_like(acc_sc)
    first_q = q_off + i * tq
    subs = [(h, c, slice(c * qs, (c + 1) * qs)) for h in range(nh) for c in range(tq // qs)]

    def scores(k0):
        for h, _, sl in subs:
            ka = ka_ref[pl.ds(k0, tk), h * FOX_DK:(h + 1) * FOX_DK]
            s = _mm(ka, qa_sc[h, :, sl], precise)
            s_sc[h, :, sl] = s
            mb_sc[h, :, sl] = jnp.max(s, axis=0, keepdims=True)

    def consume(k0, masked):
        for h, c, sl in subs:
            vt = jnp.concatenate([vt_ref[h * FOX_HD:(h + 1) * FOX_HD, pl.ds(k0, tk)], v_extra], axis=0)
            s = s_sc[h, :, sl]
            if masked:
                kpos = k0 + lax.broadcasted_iota(jnp.int32, (tk, qs), 0)
                qpos = first_q + c * qs + lax.broadcasted_iota(jnp.int32, (tk, qs), 1)
                s = jnp.where(kpos <= qpos, s, NEG_INF)
                mb = jnp.max(s, axis=0, keepdims=True)
            else:
                mb = mb_sc[h, :, sl]
            m_prev = m_sc[h, :, sl]
            m_new = jnp.maximum(m_prev, mb)
            alpha = jnp.exp2(m_prev - m_new)
            p = jnp.exp2(s - m_new)
            acc_sc[h, :, sl] = alpha * acc_sc[h, :, sl] + _mm(vt, p, precise)
            m_sc[h, :, sl] = m_new

    def softmax_only(_k0):
        for h, _, sl in subs:
            m_prev = m_sc[h, :, sl]
            m_new = jnp.maximum(m_prev, mb_sc[h, :, sl])
            al_sc[h, :, sl] = jnp.exp2(m_prev - m_new)
            p_sc[h, :, sl] = jnp.exp2(s_sc[h, :, sl] - m_new).astype(p_sc.dtype)
            m_sc[h, :, sl] = m_new

    def values(k0):
        for h, _, sl in subs:
            vt = jnp.concatenate([vt_ref[h * FOX_HD:(h + 1) * FOX_HD, pl.ds(k0, tk)], v_extra], axis=0)
            acc_sc[h, :, sl] = al_sc[h, :, sl] * acc_sc[h, :, sl] + _mm(vt, p_sc[h, :, sl], precise)

    scores(0)
    if ka_ref.shape[0] == tk:
        consume(0, True)
    else:
        assert tq == tk and q_off == 0
        @pl.when(i > 0)
        def _():
            softmax_only(0)
            scores(tk)

        def body(j, c):
            k0 = pl.multiple_of(j * tk, tk)
            values(pl.multiple_of(k0 - tk, tk))
            softmax_only(k0)
            scores(pl.multiple_of(k0 + tk, tk))
            return c

        lax.fori_loop(1, i, body, 0)

        @pl.when(i > 0)
        def _():
            values(pl.multiple_of(i * tk - tk, tk))

        consume(pl.multiple_of(i * tk, tk), True)
    for h in range(nh):
        acc = acc_sc[h]
        o_ref[h * FOX_HD:(h + 1) * FOX_HD, :] = (acc[:FOX_HD] / acc[FOX_HD:FOX_HD + 1]).astype(o_ref.dtype)


def _fox_call(q_t, k_a, v_t, tq, tk, q_off, nh, precise):
    bsz, _, t_q = q_t.shape
    t_k = k_a.shape[1]
    qs = min(FOX_QSUB, tq)
    act = _act_dtype(precise)
    qo = pl.BlockSpec((None, nh * FOX_HD, tq), lambda b, hh, i: (b, hh, i))
    return pl.pallas_call(
        functools.partial(_fox_kernel, tq=tq, tk=tk, qs=qs, q_off=q_off, precise=precise),
        grid=(bsz, FOX_HEADS // nh, t_q // tq),
        in_specs=[qo, pl.BlockSpec((None, t_k, nh * FOX_DK), lambda b, hh, i: (b, 0, hh)),
                  pl.BlockSpec((None, nh * FOX_HD, t_k), lambda b, hh, i: (b, hh, 0))],
        out_specs=qo, out_shape=jax.ShapeDtypeStruct((bsz, D_B, t_q), act),
        scratch_shapes=[pltpu.VMEM((nh, FOX_DK, tq), act), pltpu.VMEM((nh, 1, tq), F32),
                        pltpu.VMEM((nh, FOX_DV, tq), F32), pltpu.VMEM((nh, tk, tq), F32),
                        pltpu.VMEM((nh, 1, tq), F32), pltpu.VMEM((nh, tk, tq), act), pltpu.VMEM((nh, 1, tq), F32)],
        compiler_params=_cparams(("parallel", "parallel", "arbitrary")), name="fox_attn")(q_t, k_a, v_t)


def _ssd_kernel(xbc_ref, z0_ref, z1_ref, dtp_ref, dtt_ref, h0_ref, c0_ref, cw_ref, cb_ref, dtb_row_ref, dtb_col_ref,
                alog_row_ref, alog_col_ref, aloge_ref, dvec_ref, ng_ref, tril_ref, triu_ref, expand_ref,
                yc_ref, hfin_ref, cnew_ref, h_sc, pad_sc, y_sc, *, L, precise):
    c = pl.program_id(1)

    @pl.when(c == 0)
    def _():
        h_sc[...] = h0_ref[...]
        pad_sc[CONV_PAD - (CONV_W - 1):CONV_PAD, :] = c0_ref[...]

    xin = xbc_ref[...]
    pad_sc[CONV_PAD:CONV_PAD + L, :] = xin
    y = cb_ref[...]
    for j in range(CONV_W - 1):
        r0 = CONV_PAD - (CONV_W - 1) + j
        y = y + pad_sc[r0:r0 + L, :] * cw_ref[j:j + 1, :]
    y = y + xin * cw_ref[CONV_W - 1:CONV_W, :]
    tail = pad_sc[CONV_PAD + L - (CONV_W - 1):CONV_PAD + L, :]
    pad_sc[CONV_PAD - (CONV_W - 1):CONV_PAD, :] = tail
    cnew_ref[...] = tail
    act = y * _sigmoid(y)

    dt_c = _softplus(dtp_ref[...] + dtb_row_ref[...])
    a_c = dt_c * (-jnp.exp(alog_row_ref[...]))
    tril = tril_ref[...]
    hi, mid, lo = _split3(a_c)
    acum_c = (_dot(tril, hi) + _dot(tril, mid)) + _dot(tril, lo)
    dt_r = _softplus(dtt_ref[...] + dtb_col_ref[...])
    a_r = dt_r * (-jnp.exp(alog_col_ref[...]))
    triu = triu_ref[...]
    hi, mid, lo = _split3(a_r)
    acum_r = (_dot(hi, triu) + _dot(mid, triu)) + _dot(lo, triu)

    dt_e = _place(dt_c, expand_ref[...], True)
    hi, mid, lo = _split3(dt_e * (-jnp.exp(aloge_ref[...])))
    acum_e = (_dot(tril, hi) + _dot(tril, mid)) + _dot(tril, lo)
    xs = act[:, :D_C]
    e_in = jnp.exp(acum_e)
    xw = xs * (jnp.exp(acum_e[L - 1:L, :] - acum_e) * dt_e)
    dx = dvec_ref[...] * xs

    row = lax.broadcasted_iota(jnp.int32, (L, L), 0)
    col = lax.broadcasted_iota(jnp.int32, (L, L), 1)
    causal = col <= row
    first_of_pair = lax.broadcasted_iota(jnp.int32, (L, 2 * SSM_HD), 1) < SSM_HD
    rep = SSM_HEADS // SSM_GROUPS
    gw = rep * SSM_HD
    for g in range(SSM_GROUPS):
        b0 = D_C + g * SSM_N
        c0 = D_C + SSM_GROUPS * SSM_N + g * SSM_N
        b_g = act[:, b0:b0 + SSM_N]
        c_g = act[:, c0:c0 + SSM_N]
        b_gs = _split2(b_g) if precise else (b_g.astype(BF16), None)
        cb = _mm(c_g, b_gs, precise, "nt")
        h_prev = h_sc[g * rep:(g + 1) * rep].reshape(gw, SSM_N)
        y_off = _mm(c_g, h_prev, precise, "nt") * e_in[:, g * gw:(g + 1) * gw]
        s_g = _mm(xw[:, g * gw:(g + 1) * gw], b_gs, precise, "tn")
        for hh in range(rep):
            h = g * rep + hh
            alast = acum_c[L - 1:L, DT_LANE0 + h:DT_LANE0 + h + 1]
            h_sc[h] = jnp.exp(alast) * h_sc[h] + s_g[hh * SSM_HD:(hh + 1) * SSM_HD, :]
        for pr in range(rep // 2):
            ch0 = g * gw + pr * 2 * SSM_HD
            x_pair = xs[:, ch0:ch0 + 2 * SSM_HD]
            x_pair = _split2(x_pair) if precise else (x_pair.astype(BF16), None)
            y_pair = []
            for hh in (2 * pr, 2 * pr + 1):
                h = g * rep + hh
                lane = DT_LANE0 + h
                decay = jnp.exp(jnp.where(causal, acum_c[:, lane:lane + 1] - acum_r[h:h + 1, :], -jnp.inf))
                y_pair.append(_mm((cb * decay) * dt_r[h:h + 1, :], x_pair, precise))
            y_diag = jnp.where(first_of_pair, y_pair[0], y_pair[1])
            y_sc[:, ch0:ch0 + 2 * SSM_HD] = (y_diag + y_off[:, ch0 - g * gw:ch0 - g * gw + 2 * SSM_HD]) + dx[:, ch0:ch0 + 2 * SSM_HD]

    @pl.when(c == pl.num_programs(1) - 1)
    def _():
        hfin_ref[...] = h_sc[...]

    z = jnp.concatenate([z0_ref[...], z1_ref[...]], axis=1)
    yg =y_sc[...] * (z * _sigmoid(z))
    ms = jnp.mean(yg * yg, axis=-1, keepdims=True)
    yc_ref[...] = ((yg * lax.rsqrt(ms + RMS_EPS)) * ng_ref[...]).astype(yc_ref.dtype)


def _ssd_call(proj, small, dt_t, h0, conv0, lw, bsz, t, precise):
    L = min(SSD_L, t)
    nc = t // L
    zhalf = D_C // 2
    zblk = OFF_Z // zhalf

    def const(shape):
        return pl.BlockSpec(shape, lambda b, c: (0,) * len(shape))

    in_specs = [
        pl.BlockSpec((L, CONV_DIM), lambda b, c: (b * nc + c, OFF_XBC // CONV_DIM)),
        pl.BlockSpec((L, zhalf), lambda b, c: (b * nc + c, zblk)),
        pl.BlockSpec((L, zhalf), lambda b, c: (b * nc + c, zblk + 1)),
        pl.BlockSpec((L, LANES), lambda b, c: (b * nc + c, 0)),
        pl.BlockSpec((None, SSM_HEADS, L), lambda b, c: (b, 0, c)),
        pl.BlockSpec((None, SSM_HEADS, SSM_HD, SSM_N), lambda b, c: (b, 0, 0, 0)),
        pl.BlockSpec((None, CONV_W - 1, CONV_DIM), lambda b, c: (b, 0, 0)),
        const((CONV_W, CONV_DIM)), const((1, CONV_DIM)), const((1, LANES)), const((SSM_HEADS, 1)),
        const((1, LANES)), const((SSM_HEADS, 1)), const((1, D_C)), const((1, D_C)), const((1, D_C)),
        const((L, L)), const((L, L)), const((LANES, D_C)),
    ]
    expand = np.zeros((LANES, D_C), np.float32)
    for h in range(SSM_HEADS):
        expand[DT_LANE0 + h, h * SSM_HD:(h + 1) * SSM_HD] = 1.0
    out_specs = [
        pl.BlockSpec((L, D_C), lambda b, c: (b * nc + c, 0)),
        pl.BlockSpec((None, SSM_HEADS, SSM_HD, SSM_N), lambda b, c: (b, 0, 0, 0)),
        pl.BlockSpec((None, CONV_W - 1, CONV_DIM), lambda b, c: (b, 0, 0)),
    ]
    out_shape = [jax.ShapeDtypeStruct((bsz * t, D_C), _act_dtype(precise)),
                 jax.ShapeDtypeStruct((bsz, SSM_HEADS, SSM_HD, SSM_N), F32),
                 jax.ShapeDtypeStruct((bsz, CONV_W - 1, CONV_DIM), F32)]
    ones = jnp.ones((L, L), F32)
    return pl.pallas_call(
        functools.partial(_ssd_kernel, L=L, precise=precise), grid=(bsz, nc), in_specs=in_specs,
        out_specs=out_specs, out_shape=out_shape,
        scratch_shapes=[pltpu.VMEM((SSM_HEADS, SSM_HD, SSM_N), F32), pltpu.VMEM((CONV_PAD + L, CONV_DIM), F32),
                        pltpu.VMEM((L, D_C), F32)],
        compiler_params=_cparams(("parallel", "arbitrary")), name="ssd")(
            proj, proj, proj, small, dt_t, h0, conv0, lw["conv_w"], lw["conv_b"], lw["dtb_row"], lw["dtb_col"],
            lw["alog_row"], lw["alog_col"], lw["alog_e"], lw["dvec"], lw["norm_g"],
            jnp.tril(ones).astype(BF16), jnp.triu(ones).astype(BF16), jnp.asarray(expand, BF16))


def _merge_kernel(*refs, precise):
    it = iter(refs)
    ya_ref, ybt_ref, yc_ref, g0_ref, g1_ref, g2_ref, x_ref = [next(it) for _ in range(7)]
    wa, wb, wc, wo = [_take_w(it, precise) for _ in range(4)]
    lg_ref, lb_ref, o_ref, ob_ref = next(it), next(it), next(it), next(it)
    merged = g0_ref[...].astype(F32) * _mm(ya_ref[...], _ld(wa), precise)
    merged = merged + g1_ref[...].astype(F32) * _mm(ybt_ref[...], _ld(wb), precise, "tn")
    merged = merged + g2_ref[...].astype(F32) * _mm(yc_ref[...], _ld(wc), precise)
    mix = _mm(merged, _ld(wo), precise)
    y = _ln_rows(ALPHA * x_ref[...] + mix, lg_ref[...], lb_ref[...])
    o_ref[...] = y
    ob_ref[...] = y.astype(BF16)


def _merge_call(ya, yb_t, yc, gates_sig, x, lw, tm, bsz, t, precise):
    n = x.shape[0]
    per_b = t // tm

    def rows(w):
        return pl.BlockSpec((tm, w), lambda i: (i, 0))

    def const(shape):
        return pl.BlockSpec(shape, lambda i: (0, 0))

    gates = [pl.BlockSpec((tm, D_MODEL), lambda i, j=j: (i, j)) for j in range(N_BRANCH)]
    wspecs, wargs = [], []
    for name, k in (("w_a", D_AV), ("w_b", D_B), ("w_c", D_C), ("w_o", D_MODEL)):
        ws = _wargs(lw, name, precise)
        wargs += ws
        wspecs += [const((k, D_MODEL))] * len(ws)
    return pl.pallas_call(
        functools.partial(_merge_kernel, precise=precise), grid=(n // tm,),
        in_specs=[rows(D_AV), pl.BlockSpec((None, D_B, tm), lambda i: (i // per_b, 0, i % per_b)), rows(D_C)]
        + gates + [rows(D_MODEL)] + wspecs + [const((1, D_MODEL)), const((1, D_MODEL))],
        out_specs=[rows(D_MODEL), rows(D_MODEL)],
        out_shape=[jax.ShapeDtypeStruct((n, D_MODEL), F32), jax.ShapeDtypeStruct((n, D_MODEL), BF16)],
        compiler_params=_cparams(("parallel",)), name="merge")(
            ya, yb_t, yc, gates_sig, gates_sig, gates_sig, x, *wargs, lw["ln1_g"], lw["ln1_b"])


def _router_kernel(h_ref, whi_ref, wlo_ref, br_ref, o_ref, id_ref, w_ref):
    logits = _mm(h_ref[...], (whi_ref[...], wlo_ref[...]), True) + br_ref[...]
    lane = lax.broadcasted_iota(jnp.int32, logits.shape, 1)
    lg = jnp.where(lane < N_EXPERTS, logits, -jnp.inf)
    m1 = jnp.max(lg, axis=-1, keepdims=True)
    i1 = jnp.min(jnp.where(lg == m1, lane, LANES), axis=-1, keepdims=True)
    lg2 = jnp.where(lane == i1, -jnp.inf, lg)
    m2 = jnp.max(lg2, axis=-1, keepdims=True)
    i2 = jnp.min(jnp.where(lg2 == m2, lane, LANES), axis=-1, keepdims=True)
    e = jnp.exp(m2 - m1)
    den = 1.0 + e
    w1, w2 = 1.0 / den, e / den
    o_ref[...] = jnp.where(lane == i1, w1, 0.0) + jnp.where(lane == i2, w2, 0.0)
    id_ref[...] = jnp.where(lane == 0, i1, jnp.where(lane == 1, i2, 0))
    w_ref[...] = jnp.where(lane == 0, w1, jnp.where(lane == 1, w2, 0.0))


def _router_call(h, lw, tm):
    n = h.shape[0]
    out = pl.BlockSpec((tm, LANES), lambda i: (i, 0))
    return pl.pallas_call(
        _router_kernel, grid=(n // tm,),
        in_specs=[pl.BlockSpec((tm, D_MODEL), lambda i: (i, 0)), pl.BlockSpec((D_MODEL, LANES), lambda i: (0, 0)),
                  pl.BlockSpec((D_MODEL, LANES), lambda i: (0, 0)), pl.BlockSpec((1, LANES), lambda i: (0, 0))],
        out_specs=[out, out, out],
        out_shape=[jax.ShapeDtypeStruct((n, LANES), F32), jax.ShapeDtypeStruct((n, LANES), jnp.int32),
                   jax.ShapeDtypeStruct((n, LANES), F32)],
        compiler_params=_cparams(("parallel",)), name="router")(h, lw["wr"], lw["wr_lo"], lw["br"])


MOE_ROWS = 256
MOE_TM = 256
TOP_K = 2


def _moe_plan(ids, n):
    r = MOE_ROWS
    e = ids[:, :TOP_K].reshape(-1)
    na = n * TOP_K
    n_tiles = na // r + N_EXPERTS
    order = jnp.argsort(e, stable=True).astype(jnp.int32)
    inv = jnp.argsort(order).astype(jnp.int32)
    onehot = (e[:, None] == jnp.arange(N_EXPERTS, dtype=jnp.int32)[None, :]).astype(jnp.int32)
    counts = jnp.sum(onehot, axis=0)
    padded = ((counts + r - 1) // r) * r
    ends = jnp.cumsum(padded)
    starts = ends - padded
    first = jnp.cumsum(counts) - counts
    shift = starts - first
    pos = inv + jnp.sum(onehot * shift[None, :], axis=1)
    tile_row0 = jnp.arange(n_tiles, dtype=jnp.int32) * r
    tile_expert = jnp.minimum(jnp.sum(ends[None, :] <= tile_row0[:, None], axis=1), N_EXPERTS - 1).astype(jnp.int32)
    last = (first + counts - 1)[tile_expert]
    srt = jnp.minimum(tile_row0[:, None] - shift[tile_expert][:, None] + jnp.arange(r, dtype=jnp.int32)[None, :],
                      last[:, None])
    src_tok = order[jnp.clip(srt, 0, na - 1)] // TOP_K
    n_used = (ends[-1] // r).astype(jnp.int32).reshape(1)
    return src_tok.reshape(n_tiles, 1, r), pos.reshape(n // MOE_TM, 1, MOE_TM * TOP_K), tile_expert, n_used


def _row_copy(src_hbm, dst_vmem, sem, src_row, dst_row):
    return pltpu.make_async_copy(src_hbm.at[pl.ds(src_row, 1), :], dst_vmem.at[pl.ds(dst_row, 1), :], sem)


def _experts_kernel(te_ref, nu_ref, src_ref, h_hbm, wg_ref, wu_ref, wd_ref, y_ref, x_buf, sem):
    i = pl.program_id(0)
    r = x_buf.shape[0]

    @pl.when(i < nu_ref[0])
    def _():
        for k in range(r):
            _row_copy(h_hbm, x_buf, sem.at[0], src_ref[0, k], k).start()
        for k in range(r):
            _row_copy(h_hbm, x_buf, sem.at[0], 0, k).wait()
        xb = x_buf[...].astype(BF16)
        gate = _dot(xb, wg_ref[...])
        up = _dot(xb, wu_ref[...])
        y_ref[...] = _dot(((gate * _sigmoid(gate)) * up).astype(BF16), wd_ref[...])

    @pl.when(i >= nu_ref[0])
    def _():
        y_ref[...] = jnp.zeros_like(y_ref)


def _experts_call(h, src_tok, tile_expert, n_used, lw):
    n_tiles, _, r = src_tok.shape
    wg_spec = pl.BlockSpec((None, D_MODEL, D_FF_EXPERT), lambda i, te, nu: (te[i], 0, 0))
    wd_spec = pl.BlockSpec((None, D_FF_EXPERT, D_MODEL), lambda i, te, nu: (te[i], 0, 0))
    grid_spec = pltpu.PrefetchScalarGridSpec(
        num_scalar_prefetch=2, grid=(n_tiles,),
        in_specs=[pl.BlockSpec((None, 1, r), lambda i, te, nu: (i, 0, 0), memory_space=pltpu.SMEM),
                  pl.BlockSpec(memory_space=pl.ANY), wg_spec, wg_spec, wd_spec],
        out_specs=pl.BlockSpec((r, D_MODEL), lambda i, te, nu: (i, 0)),
        scratch_shapes=[pltpu.VMEM((r, D_MODEL), F32), pltpu.SemaphoreType.DMA((1,))])
    return pl.pallas_call(
        _experts_kernel, grid_spec=grid_spec,
        out_shape=jax.ShapeDtypeStruct((n_tiles * r, D_MODEL), F32),
        compiler_params=_cparams(("arbitrary",)), name="moe_experts")(
            tile_expert, n_used, src_tok, h, lw["w_gate"], lw["w_up"], lw["w_down"])


def _combine_kernel(pos_ref, hb_ref, h_ref, p_ref, w_ref, y_hbm, wp_ref, wpg_ref, lg_ref, lb_ref,
                    o_ref, ob_ref, y_buf, sem):
    tm = h_ref.shape[0]

    for k in range(tm):
        for j in range(TOP_K):
            _row_copy(y_hbm, y_buf.at[j], sem.at[0], pos_ref[0, TOP_K * k + j], k).start()
    hb = hb_ref[...]
    ple = _dot(p_ref[...].astype(BF16), wp_ref[...]) * _sigmoid(_dot(hb, wpg_ref[...]))
    acc = ALPHA * h_ref[...] + ple
    for k in range(tm):
        for j in range(TOP_K):
            _row_copy(y_hbm, y_buf.at[j], sem.at[0], 0, k).wait()
    w = w_ref[...]
    for j in range(TOP_K):
        acc = acc + w[:, j:j + 1] * y_buf[j]
    y = _ln_rows(acc, lg_ref[...], lb_ref[...])
    o_ref[...] = y
    ob_ref[...] = y.astype(BF16)


def _combine_call(hb, h, p, wsel, pos, y_sorted, lw):
    n = h.shape[0]
    tm = MOE_TM

    def rows(w):
        return pl.BlockSpec((tm, w), lambda i: (i, 0))

    def const(shape):
        return pl.BlockSpec(shape, lambda i: (0, 0))

    return pl.pallas_call(
        _combine_kernel, grid=(n // tm,),
        in_specs=[pl.BlockSpec((None, 1, tm * TOP_K), lambda i: (i, 0, 0), memory_space=pltpu.SMEM),
                  rows(D_MODEL), rows(D_MODEL), rows(PLE_DIM), rows(LANES), pl.BlockSpec(memory_space=pl.ANY),
                  const((PLE_DIM, D_MODEL)), const((D_MODEL, D_MODEL)), const((1, D_MODEL)), const((1, D_MODEL))],
        out_specs=[rows(D_MODEL), rows(D_MODEL)],
        out_shape=[jax.ShapeDtypeStruct((n, D_MODEL), F32), jax.ShapeDtypeStruct((n, D_MODEL), BF16)],
        scratch_shapes=[pltpu.VMEM((TOP_K, tm, D_MODEL), F32), pltpu.SemaphoreType.DMA((1,))],
        compiler_params=_cparams(("arbitrary",)), name="moe_combine")(
            pos, hb, h, p, wsel, y_sorted, lw["w_ple"], lw["w_pleg"], lw["ln2_g"], lw["ln2_b"])


def _ffn_kernel(*refs, weighted, precise):
    it = iter(refs)
    hb_ref, h_ref, p_ref = next(it), next(it), next(it)
    comb_ref = next(it) if weighted else None
    wg, wu, wd, wp, wpg = [_take_w(it, precise) for _ in range(5)]
    lg_ref, lb_ref, o_ref, ob_ref, acc_sc = next(it), next(it), next(it), next(it), next(it)
    j = pl.program_id(1)
    hb = h_ref[...] if precise else hb_ref[...]

    @pl.when(j == 0)
    def _():
        ple = _mm(p_ref[...], _ld(wp), precise) * _sigmoid(_mm(hb, _ld(wpg), precise))
        acc_sc[...] = ALPHA * h_ref[...] + ple

    gate = _mm(hb, _ld(wg), precise)
    up = _mm(hb, _ld(wu), precise)
    out = _mm((gate * _sigmoid(gate)) * up, _ld(wd), precise)
    if weighted:
        comb = comb_ref[...]
        lane = lax.broadcasted_iota(jnp.int32, comb.shape, 1)
        out = jnp.sum(jnp.where(lane == j, comb, 0.0), axis=-1, keepdims=True) * out
    acc_sc[...] += out

    @pl.when(j == pl.num_programs(1) - 1)
    def _():
        y = _ln_rows(acc_sc[...], lg_ref[...], lb_ref[...])
        o_ref[...] = y
        ob_ref[...] = y.astype(BF16)


def _ffn_call(hb, h, p, comb, lw, tm, precise):
    n = h.shape[0]
    weighted = comb is not None
    if weighted:
        n_e = N_EXPERTS
        wg_spec = pl.BlockSpec((None, D_MODEL, D_FF_EXPERT), lambda i, j: (j, 0, 0))
        wd_spec = pl.BlockSpec((None, D_FF_EXPERT, D_MODEL), lambda i, j: (j, 0, 0))
    else:
        n_e = D_FF // D_FF_EXPERT
        wg_spec = pl.BlockSpec((D_MODEL, D_FF_EXPERT), lambda i, j: (0, j))
        wd_spec = pl.BlockSpec((D_FF_EXPERT, D_MODEL), lambda i, j: (j, 0))

    def rows(w):
        return pl.BlockSpec((tm, w), lambda i, j: (i, 0))

    def const(shape):
        return pl.BlockSpec(shape, lambda i, j: (0, 0))

    wspecs, wargs = [], []
    for name, spec in (("w_gate", wg_spec), ("w_up", wg_spec), ("w_down", wd_spec),
                       ("w_ple", const((PLE_DIM, D_MODEL))), ("w_pleg", const((D_MODEL, D_MODEL)))):
        ws = _wargs(lw, name, precise)
        wargs += ws
        wspecs += [spec] * len(ws)
    return pl.pallas_call(
        functools.partial(_ffn_kernel, weighted=weighted, precise=precise), grid=(n // tm, n_e),
        in_specs=[rows(D_MODEL), rows(D_MODEL), rows(PLE_DIM)] + [rows(LANES)] * weighted + wspecs
        + [const((1, D_MODEL)), const((1, D_MODEL))],
        out_specs=[rows(D_MODEL), rows(D_MODEL)],
        out_shape=[jax.ShapeDtypeStruct((n, D_MODEL), F32), jax.ShapeDtypeStruct((n, D_MODEL), BF16)],
        scratch_shapes=[pltpu.VMEM((tm, D_MODEL), F32)],
        compiler_params=_cparams(("parallel", "arbitrary")), name="ffn")(
            hb, h, p, *([comb] * weighted), *wargs, lw["ln2_g"], lw["ln2_b"])


def _layer_weights(i, W):
    w_in = W["w_in"][i]
    sizes = [D_AV, D_AV, D_B, D_B, D_B, FOX_HEADS, D_C, CONV_DIM, SSM_HEADS, N_BRANCH * D_MODEL]
    o = [0] + [int(s) for s in np.cumsum(sizes)]
    seg = [w_in[:, o[k]:o[k + 1]] for k in range(len(sizes))]
    w_u, w_v, w_q, w_k, w_va, w_f, w_z, w_xbc, w_dt, w_gate = seg
    pad = jnp.zeros((D_MODEL, LANES - SSM_HEADS - FOX_HEADS), F32)

    def lane_row(v, lane0=0):
        return jnp.pad(v.astype(F32), (lane0, LANES - lane0 - v.shape[0])).reshape(1, LANES)

    lw = dict(
        fox_b=lane_row(W["fox_b_forget"][i]),
        gmlp_g=W["gmlp_ln_g"][i], gmlp_b=W["gmlp_ln_b"][i], gmlp_w=W["gmlp_w_spatial"][i], gmlp_bs=W["gmlp_b_spatial"][i],
        conv_w=W["ssm_conv_w"][i], conv_b=W["ssm_conv_b"][i].reshape(1, CONV_DIM),
        dtb_row=lane_row(W["ssm_dt_bias"][i], DT_LANE0), dtb_col=W["ssm_dt_bias"][i].reshape(SSM_HEADS, 1),
        alog_row=lane_row(W["ssm_a_log"][i], DT_LANE0), alog_col=W["ssm_a_log"][i].reshape(SSM_HEADS, 1),
        alog_e=jnp.repeat(W["ssm_a_log"][i], SSM_HD).reshape(1, D_C),
        dvec=jnp.repeat(W["ssm_d"][i], SSM_HD).reshape(1, D_C), norm_g=W["ssm_norm_g"][i].reshape(1, D_C),
        ln1_g=W["ln1_g"][i].reshape(1, D_MODEL), ln1_b=W["ln1_b"][i].reshape(1, D_MODEL),
        ln2_g=W["ln2_g"][i].reshape(1, D_MODEL), ln2_b=W["ln2_b"][i].reshape(1, D_MODEL),
    )
    j = i // 2
    mats = dict(
        w_main=jnp.concatenate([w_xbc, w_z, w_u, w_v], axis=1), w_bgate=w_gate, w_k=w_k, w_va=w_va,
        w_small=jnp.concatenate([w_f, w_dt, pad], axis=1),
        w_qt=jnp.transpose(w_q), w_vt=jnp.transpose(w_va),
        w_a=W["w_branch_a"][i], w_b=W["w_branch_b"][i], w_c=W["w_branch_c"][i], w_o=W["w_out"][i],
        w_ple=W["ple_w_proj"][i], w_pleg=W["ple_w_gate"][i],
    )
    if i % 2 == 0:
        mats.update(w_gate=W["ffn_w_gate"][j], w_up=W["ffn_w_up"][j], w_down=W["ffn_w_down"][j])
    else:
        mats.update(w_gate=W["moe_w_gate"][j], w_up=W["moe_w_up"][j], w_down=W["moe_w_down"][j],
                    wr=jnp.pad(W["moe_w_router"][j], ((0, 0), (0, LANES - N_EXPERTS))))
        lw["br"] = lane_row(W["moe_b_router"][j])
    for name, w in mats.items():
        c = w * np.float32(2.0**16 + 1.0)
        hi_f = c - (c - w)
        lw[name] = hi_f.astype(BF16)
        lw[name + "_lo"] = (w - hi_f).astype(BF16)
    return lw


def _trunk_layer(x, xb, p, i, lw, bsz, t, fox_cache, ssm0, conv0, tm, precise):
    n = bsz * t
    act = _act_dtype(precise)
    xin = x if precise else xb
    tmm = min(tm, t)
    tbig = min(n, PROJ_TM)
    proj = _matmul_call(xin, lw, "w_main", tbig, MAIN_TN, precise, "in_proj")
    gates_sig = _matmul_call(xin, lw, "w_bgate", tbig, GATE_TN, precise, "in_proj_gates", gate=True)
    k_new, v_new, small, logf = _kvs_call(xin, lw, tm, precise)

    w_eff, bias_eff = _gmlp_weights(lw["gmlp_w"], lw["gmlp_bs"], t)
    y_a, v_n = _gmlp_call(proj, lw["gmlp_g"], lw["gmlp_b"], w_eff, bias_eff, min(n, TOKEN_TM), precise)

    k_new = k_new.reshape(bsz, t, D_B)
    v_new = v_new.reshape(bsz, t, FOX_HEADS, FOX_HD)
    logf = logf.reshape(bsz, t, FOX_HEADS)
    q_t, v_t = _qvt_call(xin, lw, tmm, bsz, t, precise)
    if fox_cache is None:
        k_all, logf_all, q_off = k_new, logf, 0
    else:
        k_c, v_c, logf_c = fox_cache
        q_off = k_c.shape[1]
        k_all = jnp.concatenate([k_c.reshape(bsz, q_off, D_B), k_new], axis=1)
        v_t = jnp.concatenate([jnp.transpose(v_c.reshape(bsz, q_off, D_B), (0, 2, 1)).astype(act), v_t], axis=2)
        logf_all = jnp.concatenate([logf_c, logf], axis=1)
    t_k = k_all.shape[1]
    k_a = _keys_call(k_all, logf_all, FOX_TILE if t_k % FOX_TILE == 0 else t_k, precise)
    tq, tk = (FOX_TILE, FOX_TILE) if t % FOX_TILE == 0 else (t, t_k)
    y_bt = _fox_call(q_t, k_a, v_t, tq, tk, q_off, FOX_NH, precise)

    dt_t = jnp.transpose(small[:, DT_LANE0:DT_LANE0 + SSM_HEADS].reshape(bsz, t, SSM_HEADS), (0, 2, 1))
    y_c, h_fin, conv_new = _ssd_call(proj, small, dt_t, ssm0, conv0, lw, bsz, t, precise)

    x1, x1b = _merge_call(y_a, y_bt, y_c, gates_sig, x, lw, tmm, bsz, t, precise)
    if i % 2 == 0:
        x2, x2b = _ffn_call(x1b, x1, p, None, lw, tm, precise)
    else:
        comb, ids, wsel = _router_call(x1, lw, tm)
        if precise or n % MOE_TM or n < N_EXPERTS * MOE_ROWS:
            x2, x2b = _ffn_call(x1b, x1, p, comb, lw, tm, precise)
        else:
            src_tok, pos, tile_expert, n_used = _moe_plan(ids, n)
            y_sorted = _experts_call(x1, src_tok, tile_expert, n_used, lw)
            x2, x2b = _combine_call(x1b, x1, p, wsel, pos, y_sorted, lw)
    return x2, x2b, (k_new.reshape(bsz, t, FOX_HEADS, FOX_HD), v_new, logf, h_fin, conv_new,
                     v_n.reshape(bsz, t, D_AV))


def kernel(x_prompt, x_sample, p_prompt, p_sample, cache_fox_k, cache_fox_v, cache_fox_logf, state_ssm, state_conv, ln0_g, ln0_b, w_in, fox_b_forget, gmlp_ln_g, gmlp_ln_b, gmlp_w_spatial, gmlp_b_spatial, ssm_conv_w, ssm_conv_b, ssm_dt_bias, ssm_a_log, ssm_d, ssm_norm_g, w_branch_a, w_branch_b, w_branch_c, w_out, ln1_g, ln1_b, ln2_g, ln2_b, ffn_w_gate, ffn_w_up, ffn_w_down, moe_w_router, moe_b_router, moe_w_gate, moe_w_up, moe_w_down, ple_w_proj, ple_w_gate):
    W = dict(w_in=w_in, fox_b_forget=fox_b_forget, gmlp_ln_g=gmlp_ln_g, gmlp_ln_b=gmlp_ln_b,
             gmlp_w_spatial=gmlp_w_spatial, gmlp_b_spatial=gmlp_b_spatial, ssm_conv_w=ssm_conv_w,
             ssm_conv_b=ssm_conv_b, ssm_dt_bias=ssm_dt_bias, ssm_a_log=ssm_a_log, ssm_d=ssm_d,
             ssm_norm_g=ssm_norm_g, w_branch_a=w_branch_a, w_branch_b=w_branch_b,
             w_branch_c=w_branch_c, w_out=w_out, ln1_g=ln1_g, ln1_b=ln1_b, ln2_g=ln2_g, ln2_b=ln2_b,
             ffn_w_gate=ffn_w_gate, ffn_w_up=ffn_w_up, ffn_w_down=ffn_w_down,
             moe_w_router=moe_w_router, moe_b_router=moe_b_router, moe_w_gate=moe_w_gate, moe_w_up=moe_w_up,
             moe_w_down=moe_w_down, ple_w_proj=ple_w_proj, ple_w_gate=ple_w_gate)
    bp, tp, _ = x_prompt.shape
    bs, ts, _ = x_sample.shape
    tm_p = TOKEN_TM if (bp * tp) % TOKEN_TM == 0 else bp * tp
    tm_s = bs * ts
    xp, xpb = _layer_norm_call(x_prompt.reshape(bp * tp, D_MODEL), ln0_g, ln0_b, tm_p)
    xs, xsb = _layer_norm_call(x_sample.reshape(bs * ts, D_MODEL), ln0_g, ln0_b, tm_s)
    outs_p, outs_s = [], []
    for i in range(DEPTH):
        lw = _layer_weights(i, W)
        ssm0 = jnp.zeros((bp, SSM_HEADS, SSM_HD, SSM_N), F32)
        conv0 = jnp.zeros((bp, CONV_W - 1, CONV_DIM), F32)
        xp, xpb, st_p = _trunk_layer(xp, xpb, p_prompt[i].reshape(bp * tp, PLE_DIM), i, lw, bp, tp,
                                     None, ssm0, conv0, tm_p, False)
        xs, xsb, st_s = _trunk_layer(xs, xsb, p_sample[i].reshape(bs * ts, PLE_DIM), i, lw, bs, ts,
                                     (cache_fox_k[i], cache_fox_v[i], cache_fox_logf[i]),
                                     state_ssm[i], state_conv[i], tm_s, True)
        outs_p.append(st_p)
        outs_s.append(st_s)

    def stack(outs, k):
        return jnp.stack([o[k] for o in outs])

    return (xp.reshape(bp, tp, D_MODEL), xs.reshape(bs, ts, D_MODEL),
            stack(outs_p, 0), stack(outs_p, 1), stack(outs_p, 2), stack(outs_p, 3), stack(outs_p, 4),
            stack(outs_s, 0), stack(outs_s, 1), stack(outs_s, 2), stack(outs_s, 3), stack(outs_s, 4), stack(outs_s, 5))
```

```python
import functools
import math

import numpy as np
import jax
import jax.numpy as jnp
from jax import lax
from jax.experimental import pallas as pl
from jax.experimental.pallas import tpu as pltpu

F32 = jnp.float32
BF16 = jnp.bfloat16

D_MODEL = 1024
DEPTH = 2
CHUNK = 64
PLE_DIM = 256
GMLP_CHUNK = 128
GMLP_GROUPS = 4
D_AV = D_MODEL // 2
GMLP_GROUP_W = D_AV // GMLP_GROUPS
FOX_HD = 64
D_B = D_MODEL // 2
FOX_HEADS = D_B // FOX_HD
D_C = D_MODEL
SSM_HD = 64
SSM_HEADS = D_C // SSM_HD
SSM_N = 128
SSM_GROUPS = 2
CONV_W = 4
CONV_DIM = D_C + 2 * SSM_GROUPS * SSM_N
N_BRANCH = 3
D_FF = 11 * D_MODEL // 4
N_EXPERTS = 8
D_FF_EXPERT = D_FF // 2
ALPHA = (2.0 * DEPTH) ** 0.25
LN_EPS = 1e-5
RMS_EPS = 1e-5
NEG_INF = -1e30
LOG2E = math.log2(math.e)

LANES = 128
VMEM_LIMIT = 56 * 2**20
N_MAIN = CONV_DIM + D_C + 2 * D_AV
OFF_XBC, OFF_Z, OFF_U, OFF_V = 0, 1536, 2560, 3072
MAIN_TN = 512
GATE_TN = 1024
DT_LANE0 = FOX_HEADS
SSD_L = 128
CONV_PAD = 8
FOX_NBIAS = 3
FOX_DK = 128
FOX_DV = 80
FOX_QSUB = 256
FOX_TILE = 512
FOX_NH = 4
TOKEN_TM = 512
PROJ_TM = 2048


def _cparams(sem):
    return pltpu.CompilerParams(dimension_semantics=sem, vmem_limit_bytes=VMEM_LIMIT)


def _sigmoid(x):
    return 1.0 / (1.0 + jnp.exp(-x))


def _softplus(x):
    return jnp.maximum(x, 0.0) + jnp.log1p(jnp.exp(-jnp.abs(x)))


def _gelu(x):
    c = np.float32(np.sqrt(2.0 / np.pi))
    return x * (0.5 * (1.0 + jnp.tanh(c * (x + 0.044715 * (x * x * x)))))


def _ln_rows(x, g, b):
    mu = jnp.mean(x, axis=-1, keepdims=True)
    xc = x - mu
    var = jnp.mean(xc * xc, axis=-1, keepdims=True)
    return xc * lax.rsqrt(var + LN_EPS) * g + b


def _bf16_part(x):
    u = lax.bitcast_convert_type(x, jnp.uint32) & jnp.uint32(0xFFFF0000)
    return lax.bitcast_convert_type(u, F32)


def _split2(x):
    hi = _bf16_part(x)
    return hi.astype(BF16), (x - hi).astype(BF16)


def _split3(x):
    hi = _bf16_part(x)
    r1 = x - hi
    mid = _bf16_part(r1)
    return hi.astype(BF16), mid.astype(BF16), (r1 - mid).astype(BF16)


def _dot(a, b):
    return jnp.dot(a, b, preferred_element_type=F32)


def _dot_nt(a, b):
    return lax.dot_general(a, b, (((1,), (1,)), ((), ())), preferred_element_type=F32)


def _dot_tn(a, b):
    return lax.dot_general(a, b, (((0,), (0,)), ((), ())), preferred_element_type=F32)


_DOTS = {"nn": _dot, "nt": _dot_nt, "tn": _dot_tn}


def _mm(a, b, precise, dims="nn"):
    dot = _DOTS[dims]
    if not precise:
        a = a[0] if isinstance(a, tuple) else a.astype(BF16)
        b = b[0] if isinstance(b, tuple) else b.astype(BF16)
        return dot(a, b)
    ah, al = a if isinstance(a, tuple) else _split2(a)
    bh, bl = b if isinstance(b, tuple) else _split2(b)
    return (dot(ah, bh) + dot(al, bh)) + dot(ah, bl)


def _take_w(it, precise):
    hi = next(it)
    return hi, (next(it) if precise else None)


def _ld(w, idx=...):
    return w[0][idx], (None if w[1] is None else w[1][idx])


def _wargs(lw, name, precise):
    return [lw[name], lw[name + "_lo"]] if precise else [lw[name]]


def _act_dtype(precise):
    return F32 if precise else BF16


def _ln_kernel(x_ref, g_ref, b_ref, o_ref, ob_ref):
    y = _ln_rows(x_ref[...], g_ref[...], b_ref[...])
    o_ref[...] = y
    ob_ref[...] = y.astype(BF16)


def _layer_norm_call(x, g, b, tm):
    n, d = x.shape
    row = pl.BlockSpec((tm, d), lambda i: (i, 0))
    vec = pl.BlockSpec((1, d), lambda i: (0, 0))
    return pl.pallas_call(
        _ln_kernel, grid=(n // tm,), in_specs=[row, vec, vec], out_specs=[row, row],
        out_shape=[jax.ShapeDtypeStruct((n, d), F32), jax.ShapeDtypeStruct((n, d), BF16)],
        compiler_params=_cparams(("parallel",)), name="ln_in")(x, g.reshape(1, d), b.reshape(1, d))


def _mm_kernel(*refs, precise, gate):
    it = iter(refs)
    x_ref = next(it)
    w = _take_w(it, precise)
    o_ref = next(it)
    y = _mm(x_ref[...], _ld(w), precise)
    o_ref[...] = (_sigmoid(y) if gate else y).astype(o_ref.dtype)


def _matmul_call(x, lw, wname, tm, tn, precise, name, gate=False):
    m, k = x.shape
    ws = _wargs(lw, wname, precise)
    n = ws[0].shape[1]
    wspec = pl.BlockSpec((k, tn), lambda i, j: (0, j))
    return pl.pallas_call(
        functools.partial(_mm_kernel, precise=precise, gate=gate), grid=(m // tm, n // tn),
        in_specs=[pl.BlockSpec((tm, k), lambda i, j: (i, 0))] + [wspec] * len(ws),
        out_specs=pl.BlockSpec((tm, tn), lambda i, j: (i, j)),
        out_shape=jax.ShapeDtypeStruct((m, n), _act_dtype(precise) if gate else F32),
        compiler_params=_cparams(("parallel", "parallel")), name=name)(x, *ws)


def _gmlp_kernel(*refs, rows, precise):
    it = iter(refs)
    u_ref, v_ref, g_ref, b_ref = next(it), next(it), next(it), next(it)
    w = _take_w(it, precise)
    bias_ref, ya_ref, vn_ref = next(it), next(it), next(it)
    u = _gelu(u_ref[...])
    vn = _ln_rows(_gelu(v_ref[...]), g_ref[...], b_ref[...])
    vn_ref[...] = vn
    vb = vn if precise else vn.astype(BF16)
    for c in range(rows // GMLP_CHUNK):
        r0 = c * GMLP_CHUNK
        for g in range(GMLP_GROUPS):
            c0 = g * GMLP_GROUP_W
            mixed = _mm(_ld(w, g), vb[r0:r0 + GMLP_CHUNK, c0:c0 + GMLP_GROUP_W], precise)
            mixed = mixed + bias_ref[:, c0:c0 + GMLP_GROUP_W]
            ya_ref[r0:r0 + GMLP_CHUNK, c0:c0 + GMLP_GROUP_W] = (
                u[r0:r0 + GMLP_CHUNK, c0:c0 + GMLP_GROUP_W] * mixed).astype(ya_ref.dtype)


def _gmlp_call(proj, ln_g, ln_b, w_eff, bias_eff, rows, precise):
    n = proj.shape[0]
    ublk, vblk = OFF_U // D_AV, OFF_V // D_AV
    vec = pl.BlockSpec((1, D_AV), lambda i: (0, 0))
    ws = list(_split2(w_eff)) if precise else [w_eff.astype(BF16)]
    wspec = pl.BlockSpec((GMLP_GROUPS, GMLP_CHUNK, GMLP_CHUNK), lambda i: (0, 0, 0))
    return pl.pallas_call(
        functools.partial(_gmlp_kernel, rows=rows, precise=precise), grid=(n // rows,),
        in_specs=[pl.BlockSpec((rows, D_AV), lambda i: (i, ublk)), pl.BlockSpec((rows, D_AV), lambda i: (i, vblk)),
                  vec, vec] + [wspec] * len(ws) + [pl.BlockSpec((GMLP_CHUNK, D_AV), lambda i: (0, 0))],
        out_specs=[pl.BlockSpec((rows, D_AV), lambda i: (i, 0)), pl.BlockSpec((rows, D_AV), lambda i: (i, 0))],
        out_shape=[jax.ShapeDtypeStruct((n, D_AV), _act_dtype(precise)), jax.ShapeDtypeStruct((n, D_AV), F32)],
        compiler_params=_cparams(("parallel",)), name="gmlp")(
            proj, proj, ln_g.reshape(1, D_AV), ln_b.reshape(1, D_AV), *ws, bias_eff)


def _gmlp_weights(w_s, b_s, t):
    l = min(GMLP_CHUNK, t)
    pos = np.arange(l)
    mask = (pos[None, :] // CHUNK) <= (pos[:, None] // CHUNK)
    w = jnp.where(mask[None], w_s[:, :l, :l], 0.0)
    bias = jnp.transpose(b_s[:, :l])
    reps = GMLP_CHUNK // l
    if reps > 1:
        eye = jnp.eye(reps, dtype=w.dtype)
        w = jnp.einsum("ab,gij->gaibj", eye, w).reshape(GMLP_GROUPS, GMLP_CHUNK, GMLP_CHUNK)
        bias = jnp.tile(bias, (reps, 1))
    return w, jnp.repeat(bias, GMLP_GROUP_W, axis=1)


def _kvs_kernel(*refs, precise):
    it = iter(refs)
    x_ref = next(it)
    wk, wv, ws = [_take_w(it, precise) for _ in range(3)]
    fb_ref, k_ref, v_ref, s_ref, lf_ref = [next(it) for _ in range(5)]
    x = x_ref[...]
    k_ref[...] = _mm(x, _ld(wk), precise)
    v_ref[...] = _mm(x, _ld(wv), precise)
    small = _mm(x, _ld(ws), precise)
    s_ref[...] = small
    lf_ref[...] = (-_softplus(-(small + fb_ref[...])))[:, :FOX_HEADS]


def _kvs_call(x, lw, tm, precise):
    n = x.shape[0]
    wargs, wspecs = [], []
    for name, width in (("w_k", D_B), ("w_va", D_B), ("w_small", LANES)):
        ws = _wargs(lw, name, precise)
        wargs += ws
        wspecs += [pl.BlockSpec((D_MODEL, width), lambda i: (0, 0))] * len(ws)

    def rows(w):
        return pl.BlockSpec((tm, w), lambda i: (i, 0))

    return pl.pallas_call(
        functools.partial(_kvs_kernel, precise=precise), grid=(n // tm,),
        in_specs=[rows(D_MODEL)] + wspecs + [pl.BlockSpec((1, LANES), lambda i: (0, 0))],
        out_specs=[rows(D_B), rows(D_B), rows(LANES), rows(FOX_HEADS)],
        out_shape=[jax.ShapeDtypeStruct((n, D_B), F32), jax.ShapeDtypeStruct((n, D_B), F32),
                   jax.ShapeDtypeStruct((n, LANES), F32), jax.ShapeDtypeStruct((n, FOX_HEADS), F32)],
        compiler_params=_cparams(("parallel",)), name="fox_kv")(x, *wargs, lw["fox_b"])


def _qvt_kernel(*refs, precise):
    it = iter(refs)
    x_ref = next(it)
    wq, wv = _take_w(it, precise), _take_w(it, precise)
    qt_ref, vt_ref = next(it), next(it)
    x = x_ref[...]
    qt_ref[...] = (_mm(_ld(wq), x, precise, "nt") * (FOX_HD ** -0.5 * LOG2E)).astype(qt_ref.dtype)
    vt_ref[...] = _mm(_ld(wv), x, precise, "nt").astype(vt_ref.dtype)


def _qvt_call(x, lw, tm, bsz, t, precise):
    n = x.shape[0]
    per_b = t // tm
    act = _act_dtype(precise)
    ws = _wargs(lw, "w_qt", precise) + _wargs(lw, "w_vt", precise)
    out = pl.BlockSpec((None, D_B, tm), lambda i: (i // per_b, 0, i % per_b))
    return pl.pallas_call(
        functools.partial(_qvt_kernel, precise=precise), grid=(n // tm,),
        in_specs=[pl.BlockSpec((tm, D_MODEL), lambda i: (i, 0))]
        + [pl.BlockSpec((D_B, D_MODEL), lambda i: (0, 0))] * len(ws),
        out_specs=[out, out],
        out_shape=[jax.ShapeDtypeStruct((bsz, D_B, t), act), jax.ShapeDtypeStruct((bsz, D_B, t), act)],
        compiler_params=_cparams(("parallel",)), name="fox_qv_t")(x, *ws)


def _place(x, sel, precise):
    if not precise:
        return _dot(x.astype(BF16), sel)
    hi, mid, lo = _split3(x)
    return (_dot(hi, sel) + _dot(mid, sel)) + _dot(lo, sel)


def _keys_kernel(k_ref, lf_ref, tril_ref, selk_ref, selg_ref, o_ref, carry_ref, *, precise):
    @pl.when(pl.program_id(1) == 0)
    def _():
        carry_ref[...] = jnp.zeros_like(carry_ref)

    tril = tril_ref[...]
    hi, mid, lo = _split3(lf_ref[...])
    f = (_dot(tril, hi) + _dot(tril, mid)) + _dot(tril, lo) + carry_ref[0:1, :]
    tc = f.shape[0]
    carry_ref[...] = jnp.broadcast_to(f[tc - 1:tc, :], carry_ref.shape)
    g = f * (-LOG2E)
    g1 = _bf16_part(g)
    r = g - g1
    g2 = _bf16_part(r)
    gcat = jnp.concatenate([g1, g2, r - g2], axis=1)
    ka = _place(k_ref[...], selk_ref[...], precise) + _place(gcat, selg_ref[...], precise)
    o_ref[...] = ka.astype(o_ref.dtype)


def _keys_call(k_all, logf_all, tc, precise):
    bsz, t_k, _ = k_all.shape
    sel_k = np.zeros((D_B, FOX_HEADS * FOX_DK), np.float32)
    sel_g = np.zeros((FOX_NBIAS * FOX_HEADS, FOX_HEADS * FOX_DK), np.float32)
    for h in range(FOX_HEADS):
        sel_k[h * FOX_HD + np.arange(FOX_HD), h * FOX_DK + np.arange(FOX_HD)] = 1.0
        for j in range(FOX_NBIAS):
            sel_g[j * FOX_HEADS + h, h * FOX_DK + FOX_HD + j] = 1.0
    tril = jnp.tril(jnp.ones((tc, tc), F32)).astype(BF16)
    return pl.pallas_call(
        functools.partial(_keys_kernel, precise=precise), grid=(bsz, t_k // tc),
        in_specs=[pl.BlockSpec((None, tc, D_B), lambda b, j: (b, j, 0)),
                  pl.BlockSpec((None, tc, FOX_HEADS), lambda b, j: (b, j, 0)),
                  pl.BlockSpec((tc, tc), lambda b, j: (0, 0)),
                  pl.BlockSpec(sel_k.shape, lambda b, j: (0, 0)), pl.BlockSpec(sel_g.shape, lambda b, j: (0, 0))],
        out_specs=pl.BlockSpec((None, tc, FOX_HEADS * FOX_DK), lambda b, j: (b, j, 0)),
        out_shape=jax.ShapeDtypeStruct((bsz, t_k, FOX_HEADS * FOX_DK), _act_dtype(precise)),
        scratch_shapes=[pltpu.VMEM((8, FOX_HEADS), F32)],
        compiler_params=_cparams(("parallel", "arbitrary")), name="fox_keys")(
            k_all, logf_all, tril, jnp.asarray(sel_k, BF16), jnp.asarray(sel_g, BF16))


def _fox_kernel(qt_ref, ka_ref, vt_ref, o_ref, qa_sc, m_sc, acc_sc, s_sc, mb_sc, p_sc, al_sc,
                *, tq, tk, qs, q_off, precise):
    i = pl.program_id(2)
    act = qa_sc.dtype
    nh = qa_sc.shape[0]
    rows = lax.broadcasted_iota(jnp.int32, (FOX_DK - FOX_HD, tq), 0)
    ones_rows = jnp.where(rows < FOX_NBIAS, 1.0, 0.0).astype(act)
    for h in range(nh):
        qa_sc[h] = jnp.concatenate([qt_ref[h * FOX_HD:(h + 1) * FOX_HD, :], ones_rows], axis=0)
    vrows = lax.broadcasted_iota(jnp.int32, (FOX_DV - FOX_HD, tk), 0)
    v_extra = jnp.where(vrows < 1, 1.0, 0.0).astype(act)
    m_sc[...] = jnp.full_like(m_sc, -jnp.inf)
    acc_sc[...] = jnp.zeros_like(acc_sc)
    first_q = q_off + i * tq
    subs = [(h, c, slice(c * qs, (c + 1) * qs)) for h in range(nh) for c in range(tq // qs)]

    def scores(k0):
        for h, _, sl in subs:
            ka = ka_ref[pl.ds(k0, tk), h * FOX_DK:(h + 1) * FOX_DK]
            s = _mm(ka, qa_sc[h, :, sl], precise)
            s_sc[h, :, sl] = s
            mb_sc[h, :, sl] = jnp.max(s, axis=0, keepdims=True)

    def consume(k0, masked):
        for h, c, sl in subs:
            vt = jnp.concatenate([vt_ref[h * FOX_HD:(h + 1) * FOX_HD, pl.ds(k0, tk)], v_extra], axis=0)
            s = s_sc[h, :, sl]
            if masked:
                kpos = k0 + lax.broadcasted_iota(jnp.int32, (tk, qs), 0)
                qpos = first_q + c * qs + lax.broadcasted_iota(jnp.int32, (tk, qs), 1)
                s = jnp.where(kpos <= qpos, s, NEG_INF)
                mb = jnp.max(s, axis=0, keepdims=True)
            else:
                mb = mb_sc[h, :, sl]
            m_prev = m_sc[h, :, sl]
            m_new = jnp.maximum(m_prev, mb)
            alpha = jnp.exp2(m_prev - m_new)
            p = jnp.exp2(s - m_new)
            acc_sc[h, :, sl] = alpha * acc_sc[h, :, sl] + _mm(vt, p, precise)
            m_sc[h, :, sl] = m_new

    def softmax_only(_k0):
        for h, _, sl in subs:
            m_prev = m_sc[h, :, sl]
            m_new = jnp.maximum(m_prev, mb_sc[h, :, sl])
            al_sc[h, :, sl] = jnp.exp2(m_prev - m_new)
            p_sc[h, :, sl] = jnp.exp2(s_sc[h, :, sl] - m_new).astype(p_sc.dtype)
            m_sc[h, :, sl] = m_new

    def values(k0):
        for h, _, sl in subs:
            vt = jnp.concatenate([vt_ref[h * FOX_HD:(h + 1) * FOX_HD, pl.ds(k0, tk)], v_extra], axis=0)
            acc_sc[h, :, sl] = al_sc[h, :, sl] * acc_sc[h, :, sl] + _mm(vt, p_sc[h, :, sl], precise)

    scores(0)
    if ka_ref.shape[0] == tk:
        consume(0, True)
    else:
        assert tq == tk and q_off == 0
        @pl.when(i > 0)
        def _():
            softmax_only(0)
            scores(tk)

        def body(j, c):
            k0 = pl.multiple_of(j * tk, tk)
            values(pl.multiple_of(k0 - tk, tk))
            softmax_only(k0)
            scores(pl.multiple_of(k0 + tk, tk))
            return c

        lax.fori_loop(1, i, body, 0)

        @pl.when(i > 0)
        def _():
            values(pl.multiple_of(i * tk - tk, tk))

        consume(pl.multiple_of(i * tk, tk), True)
    for h in range(nh):
        acc = acc_sc[h]
        o_ref[h * FOX_HD:(h + 1) * FOX_HD, :] = (acc[:FOX_HD] / acc[FOX_HD:FOX_HD + 1]).astype(o_ref.dtype)


def _fox_call(q_t, k_a, v_t, tq, tk, q_off, nh, precise):
    bsz, _, t_q = q_t.shape
    t_k = k_a.shape[1]
    qs = min(FOX_QSUB, tq)
    act = _act_dtype(precise)
    qo = pl.BlockSpec((None, nh * FOX_HD, tq), lambda b, hh, i: (b, hh, i))
    return pl.pallas_call(
        functools.partial(_fox_kernel, tq=tq, tk=tk, qs=qs, q_off=q_off, precise=precise),
        grid=(bsz, FOX_HEADS // nh, t_q // tq),
        in_specs=[qo, pl.BlockSpec((None, t_k, nh * FOX_DK), lambda b, hh, i: (b, 0, hh)),
                  pl.BlockSpec((None, nh * FOX_HD, t_k), lambda b, hh, i: (b, hh, 0))],
        out_specs=qo, out_shape=jax.ShapeDtypeStruct((bsz, D_B, t_q), act),
        scratch_shapes=[pltpu.VMEM((nh, FOX_DK, tq), act), pltpu.VMEM((nh, 1, tq), F32),
                        pltpu.VMEM((nh, FOX_DV, tq), F32), pltpu.VMEM((nh, tk, tq), F32),
                        pltpu.VMEM((nh, 1, tq), F32), pltpu.VMEM((nh, tk, tq), act), pltpu.VMEM((nh, 1, tq), F32)],
        compiler_params=_cparams(("parallel", "parallel", "arbitrary")), name="fox_attn")(q_t, k_a, v_t)


def _ssd_kernel(xbc_ref, z0_ref, z1_ref, dtp_ref, dtt_ref, h0_ref, c0_ref, cw_ref, cb_ref, dtb_row_ref, dtb_col_ref,
                alog_row_ref, alog_col_ref, aloge_ref, dvec_ref, ng_ref, tril_ref, triu_ref, expand_ref,
                yc_ref, hfin_ref, cnew_ref, h_sc, pad_sc, y_sc, *, L, precise):
    c = pl.program_id(1)

    @pl.when(c == 0)
    def _():
        h_sc[...] = h0_ref[...]
        pad_sc[CONV_PAD - (CONV_W - 1):CONV_PAD, :] = c0_ref[...]

    xin = xbc_ref[...]
    pad_sc[CONV_PAD:CONV_PAD + L, :] = xin
    y = cb_ref[...]
    for j in range(CONV_W - 1):
        r0 = CONV_PAD - (CONV_W - 1) + j
        y = y + pad_sc[r0:r0 + L, :] * cw_ref[j:j + 1, :]
    y = y + xin * cw_ref[CONV_W - 1:CONV_W, :]
    tail = pad_sc[CONV_PAD + L - (CONV_W - 1):CONV_PAD + L, :]
    pad_sc[CONV_PAD - (CONV_W - 1):CONV_PAD, :] = tail
    cnew_ref[...] = tail
    act = y * _sigmoid(y)

    dt_c = _softplus(dtp_ref[...] + dtb_row_ref[...])
    a_c = dt_c * (-jnp.exp(alog_row_ref[...]))
    tril = tril_ref[...]
    hi, mid, lo = _split3(a_c)
    acum_c = (_dot(tril, hi) + _dot(tril, mid)) + _dot(tril, lo)
    dt_r = _softplus(dtt_ref[...] + dtb_col_ref[...])
    a_r = dt_r * (-jnp.exp(alog_col_ref[...]))
    triu = triu_ref[...]
    hi, mid, lo = _split3(a_r)
    acum_r = (_dot(hi, triu) + _dot(mid, triu)) + _dot(lo, triu)

    dt_e = _place(dt_c, expand_ref[...], True)
    hi, mid, lo = _split3(dt_e * (-jnp.exp(aloge_ref[...])))
    acum_e = (_dot(tril, hi) + _dot(tril, mid)) + _dot(tril, lo)
    xs = act[:, :D_C]
    e_in = jnp.exp(acum_e)
    xw = xs * (jnp.exp(acum_e[L - 1:L, :] - acum_e) * dt_e)
    dx = dvec_ref[...] * xs

    row = lax.broadcasted_iota(jnp.int32, (L, L), 0)
    col = lax.broadcasted_iota(jnp.int32, (L, L), 1)
    causal = col <= row
    first_of_pair = lax.broadcasted_iota(jnp.int32, (L, 2 * SSM_HD), 1) < SSM_HD
    rep = SSM_HEADS // SSM_GROUPS
    gw = rep * SSM_HD
    for g in range(SSM_GROUPS):
        b0 = D_C + g * SSM_N
        c0 = D_C + SSM_GROUPS * SSM_N + g * SSM_N
        b_g = act[:, b0:b0 + SSM_N]
        c_g = act[:, c0:c0 + SSM_N]
        b_gs = _split2(b_g) if precise else (b_g.astype(BF16), None)
        cb = _mm(c_g, b_gs, precise, "nt")
        h_prev = h_sc[g * rep:(g + 1) * rep].reshape(gw, SSM_N)
        y_off = _mm(c_g, h_prev, precise, "nt") * e_in[:, g * gw:(g + 1) * gw]
        s_g = _mm(xw[:, g * gw:(g + 1) * gw], b_gs, precise, "tn")
        for hh in range(rep):
            h = g * rep + hh
            alast = acum_c[L - 1:L, DT_LANE0 + h:DT_LANE0 + h + 1]
            h_sc[h] = jnp.exp(alast) * h_sc[h] + s_g[hh * SSM_HD:(hh + 1) * SSM_HD, :]
        for pr in range(rep // 2):
            ch0 = g * gw + pr * 2 * SSM_HD
            x_pair = xs[:, ch0:ch0 + 2 * SSM_HD]
            x_pair = _split2(x_pair) if precise else (x_pair.astype(BF16), None)
            y_pair = []
            for hh in (2 * pr, 2 * pr + 1):
                h = g * rep + hh
                lane = DT_LANE0 + h
                decay = jnp.exp(jnp.where(causal, acum_c[:, lane:lane + 1] - acum_r[h:h + 1, :], -jnp.inf))
                y_pair.append(_mm((cb * decay) * dt_r[h:h + 1, :], x_pair, precise))
            y_diag = jnp.where(first_of_pair, y_pair[0], y_pair[1])
            y_sc[:, ch0:ch0 + 2 * SSM_HD] = (y_diag + y_off[:, ch0 - g * gw:ch0 - g * gw + 2 * SSM_HD]) + dx[:, ch0:ch0 + 2 * SSM_HD]

    @pl.when(c == pl.num_programs(1) - 1)
    def _():
        hfin_ref[...] = h_sc[...]

    z = jnp.concatenate([z0_ref[...], z1_ref[...]], axis=1)
    yg =y_sc[...] * (z * _sigmoid(z))
    ms = jnp.mean(yg * yg, axis=-1, keepdims=True)
    yc_ref[...] = ((yg * lax.rsqrt(ms + RMS_EPS)) * ng_ref[...]).astype(yc_ref.dtype)


def _ssd_call(proj, small, dt_t, h0, conv0, lw, bsz, t, precise):
    L = min(SSD_L, t)
    nc = t // L
    zhalf = D_C // 2
    zblk = OFF_Z // zhalf

    def const(shape):
        return pl.BlockSpec(shape, lambda b, c: (0,) * len(shape))

    in_specs = [
        pl.BlockSpec((L, CONV_DIM), lambda b, c: (b * nc + c, OFF_XBC // CONV_DIM)),
        pl.BlockSpec((L, zhalf), lambda b, c: (b * nc + c, zblk)),
        pl.BlockSpec((L, zhalf), lambda b, c: (b * nc + c, zblk + 1)),
        pl.BlockSpec((L, LANES), lambda b, c: (b * nc + c, 0)),
        pl.BlockSpec((None, SSM_HEADS, L), lambda b, c: (b, 0, c)),
        pl.BlockSpec((None, SSM_HEADS, SSM_HD, SSM_N), lambda b, c: (b, 0, 0, 0)),
        pl.BlockSpec((None, CONV_W - 1, CONV_DIM), lambda b, c: (b, 0, 0)),
        const((CONV_W, CONV_DIM)), const((1, CONV_DIM)), const((1, LANES)), const((SSM_HEADS, 1)),
        const((1, LANES)), const((SSM_HEADS, 1)), const((1, D_C)), const((1, D_C)), const((1, D_C)),
        const((L, L)), const((L, L)), const((LANES, D_C)),
    ]
    expand = np.zeros((LANES, D_C), np.float32)
    for h in range(SSM_HEADS):
        expand[DT_LANE0 + h, h * SSM_HD:(h + 1) * SSM_HD] = 1.0
    out_specs = [
        pl.BlockSpec((L, D_C), lambda b, c: (b * nc + c, 0)),
        pl.BlockSpec((None, SSM_HEADS, SSM_HD, SSM_N), lambda b, c: (b, 0, 0, 0)),
        pl.BlockSpec((None, CONV_W - 1, CONV_DIM), lambda b, c: (b, 0, 0)),
    ]
    out_shape = [jax.ShapeDtypeStruct((bsz * t, D_C), _act_dtype(precise)),
                 jax.ShapeDtypeStruct((bsz, SSM_HEADS, SSM_HD, SSM_N), F32),
                 jax.ShapeDtypeStruct((bsz, CONV_W - 1, CONV_DIM), F32)]
    ones = jnp.ones((L, L), F32)
    return pl.pallas_call(
        functools.partial(_ssd_kernel, L=L, precise=precise), grid=(bsz, nc), in_specs=in_specs,
        out_specs=out_specs, out_shape=out_shape,
        scratch_shapes=[pltpu.VMEM((SSM_HEADS, SSM_HD, SSM_N), F32), pltpu.VMEM((CONV_PAD + L, CONV_DIM), F32),
                        pltpu.VMEM((L, D_C), F32)],
        compiler_params=_cparams(("parallel", "arbitrary")), name="ssd")(
            proj, proj, proj, small, dt_t, h0, conv0, lw["conv_w"], lw["conv_b"], lw["dtb_row"], lw["dtb_col"],
            lw["alog_row"], lw["alog_col"], lw["alog_e"], lw["dvec"], lw["norm_g"],
            jnp.tril(ones).astype(BF16), jnp.triu(ones).astype(BF16), jnp.asarray(expand, BF16))


def _merge_kernel(*refs, precise):
    it = iter(refs)
    ya_ref, ybt_ref, yc_ref, g0_ref, g1_ref, g2_ref, x_ref = [next(it) for _ in range(7)]
    wa, wb, wc, wo = [_take_w(it, precise) for _ in range(4)]
    lg_ref, lb_ref, o_ref, ob_ref = next(it), next(it), next(it), next(it)
    merged = g0_ref[...].astype(F32) * _mm(ya_ref[...], _ld(wa), precise)
    merged = merged + g1_ref[...].astype(F32) * _mm(ybt_ref[...], _ld(wb), precise, "tn")
    merged = merged + g2_ref[...].astype(F32) * _mm(yc_ref[...], _ld(wc), precise)
    mix = _mm(merged, _ld(wo), precise)
    y = _ln_rows(ALPHA * x_ref[...] + mix, lg_ref[...], lb_ref[...])
    o_ref[...] = y
    ob_ref[...] = y.astype(BF16)


def _merge_call(ya, yb_t, yc, gates_sig, x, lw, tm, bsz, t, precise):
    n = x.shape[0]
    per_b = t // tm

    def rows(w):
        return pl.BlockSpec((tm, w), lambda i: (i, 0))

    def const(shape):
        return pl.BlockSpec(shape, lambda i: (0, 0))

    gates = [pl.BlockSpec((tm, D_MODEL), lambda i, j=j: (i, j)) for j in range(N_BRANCH)]
    wspecs, wargs = [], []
    for name, k in (("w_a", D_AV), ("w_b", D_B), ("w_c", D_C), ("w_o", D_MODEL)):
        ws = _wargs(lw, name, precise)
        wargs += ws
        wspecs += [const((k, D_MODEL))] * len(ws)
    return pl.pallas_call(
        functools.partial(_merge_kernel, precise=precise), grid=(n // tm,),
        in_specs=[rows(D_AV), pl.BlockSpec((None, D_B, tm), lambda i: (i // per_b, 0, i % per_b)), rows(D_C)]
        + gates + [rows(D_MODEL)] + wspecs + [const((1, D_MODEL)), const((1, D_MODEL))],
        out_specs=[rows(D_MODEL), rows(D_MODEL)],
        out_shape=[jax.ShapeDtypeStruct((n, D_MODEL), F32), jax.ShapeDtypeStruct((n, D_MODEL), BF16)],
        compiler_params=_cparams(("parallel",)), name="merge")(
            ya, yb_t, yc, gates_sig, gates_sig, gates_sig, x, *wargs, lw["ln1_g"], lw["ln1_b"])


def _router_kernel(h_ref, whi_ref, wlo_ref, br_ref, o_ref, id_ref, w_ref):
    logits = _mm(h_ref[...], (whi_ref[...], wlo_ref[...]), True) + br_ref[...]
    lane = lax.broadcasted_iota(jnp.int32, logits.shape, 1)
    lg = jnp.where(lane < N_EXPERTS, logits, -jnp.inf)
    m1 = jnp.max(lg, axis=-1, keepdims=True)
    i1 = jnp.min(jnp.where(lg == m1, lane, LANES), axis=-1, keepdims=True)
    lg2 = jnp.where(lane == i1, -jnp.inf, lg)
    m2 = jnp.max(lg2, axis=-1, keepdims=True)
    i2 = jnp.min(jnp.where(lg2 == m2, lane, LANES), axis=-1, keepdims=True)
    e = jnp.exp(m2 - m1)
    den = 1.0 + e
    w1, w2 = 1.0 / den, e / den
    o_ref[...] = jnp.where(lane == i1, w1, 0.0) + jnp.where(lane == i2, w2, 0.0)
    id_ref[...] = jnp.where(lane == 0, i1, jnp.where(lane == 1, i2, 0))
    w_ref[...] = jnp.where(lane == 0, w1, jnp.where(lane == 1, w2, 0.0))


def _router_call(h, lw, tm):
    n = h.shape[0]
    out = pl.BlockSpec((tm, LANES), lambda i: (i, 0))
    return pl.pallas_call(
        _router_kernel, grid=(n // tm,),
        in_specs=[pl.BlockSpec((tm, D_MODEL), lambda i: (i, 0)), pl.BlockSpec((D_MODEL, LANES), lambda i: (0, 0)),
                  pl.BlockSpec((D_MODEL, LANES), lambda i: (0, 0)), pl.BlockSpec((1, LANES), lambda i: (0, 0))],
        out_specs=[out, out, out],
        out_shape=[jax.ShapeDtypeStruct((n, LANES), F32), jax.ShapeDtypeStruct((n, LANES), jnp.int32),
                   jax.ShapeDtypeStruct((n, LANES), F32)],
        compiler_params=_cparams(("parallel",)), name="router")(h, lw["wr"], lw["wr_lo"], lw["br"])


MOE_ROWS = 256
MOE_TM = 256
TOP_K = 2


def _moe_plan(ids, n):
    r = MOE_ROWS
    e = ids[:, :TOP_K].reshape(-1)
    na = n * TOP_K
    n_tiles = na // r + N_EXPERTS
    order = jnp.argsort(e, stable=True).astype(jnp.int32)
    inv = jnp.argsort(order).astype(jnp.int32)
    onehot = (e[:, None] == jnp.arange(N_EXPERTS, dtype=jnp.int32)[None, :]).astype(jnp.int32)
    counts = jnp.sum(onehot, axis=0)
    padded = ((counts + r - 1) // r) * r
    ends = jnp.cumsum(padded)
    starts = ends - padded
    first = jnp.cumsum(counts) - counts
    shift = starts - first
    pos = inv + jnp.sum(onehot * shift[None, :], axis=1)
    tile_row0 = jnp.arange(n_tiles, dtype=jnp.int32) * r
    tile_expert = jnp.minimum(jnp.sum(ends[None, :] <= tile_row0[:, None], axis=1), N_EXPERTS - 1).astype(jnp.int32)
    last = (first + counts - 1)[tile_expert]
    srt = jnp.minimum(tile_row0[:, None] - shift[tile_expert][:, None] + jnp.arange(r, dtype=jnp.int32)[None, :],
                      last[:, None])
    src_tok = order[jnp.clip(srt, 0, na - 1)] // TOP_K
    n_used = (ends[-1] // r).astype(jnp.int32).reshape(1)
    return src_tok.reshape(n_tiles, 1, r), pos.reshape(n // MOE_TM, 1, MOE_TM * TOP_K), tile_expert, n_used


def _row_copy(src_hbm, dst_vmem, sem, src_row, dst_row):
    return pltpu.make_async_copy(src_hbm.at[pl.ds(src_row, 1), :], dst_vmem.at[pl.ds(dst_row, 1), :], sem)


def _experts_kernel(te_ref, nu_ref, src_ref, h_hbm, wg_ref, wu_ref, wd_ref, y_ref, x_buf, sem):
    i = pl.program_id(0)
    r = x_buf.shape[0]

    @pl.when(i < nu_ref[0])
    def _():
        for k in range(r):
            _row_copy(h_hbm, x_buf, sem.at[0], src_ref[0, k], k).start(priority=k % 2)
        for k in range(r):
            _row_copy(h_hbm, x_buf, sem.at[0], 0, k).wait()
        xb = x_buf[...].astype(BF16)
        gate = _dot(xb, wg_ref[...])
        up = _dot(xb, wu_ref[...])
        y_ref[...] = _dot(((gate * _sigmoid(gate)) * up).astype(BF16), wd_ref[...])

    @pl.when(i >= nu_ref[0])
    def _():
        y_ref[...] = jnp.zeros_like(y_ref)


def _experts_call(h, src_tok, tile_expert, n_used, lw):
    n_tiles, _, r = src_tok.shape
    wg_spec = pl.BlockSpec((None, D_MODEL, D_FF_EXPERT), lambda i, te, nu: (te[i], 0, 0))
    wd_spec = pl.BlockSpec((None, D_FF_EXPERT, D_MODEL), lambda i, te, nu: (te[i], 0, 0))
    grid_spec = pltpu.PrefetchScalarGridSpec(
        num_scalar_prefetch=2, grid=(n_tiles,),
        in_specs=[pl.BlockSpec((None, 1, r), lambda i, te, nu: (i, 0, 0), memory_space=pltpu.SMEM),
                  pl.BlockSpec(memory_space=pl.ANY), wg_spec, wg_spec, wd_spec],
        out_specs=pl.BlockSpec((r, D_MODEL), lambda i, te, nu: (i, 0)),
        scratch_shapes=[pltpu.VMEM((r, D_MODEL), F32), pltpu.SemaphoreType.DMA((1,))])
    return pl.pallas_call(
        _experts_kernel, grid_spec=grid_spec,
        out_shape=jax.ShapeDtypeStruct((n_tiles * r, D_MODEL), F32),
        compiler_params=_cparams(("arbitrary",)), name="moe_experts")(
            tile_expert, n_used, src_tok, h, lw["w_gate"], lw["w_up"], lw["w_down"])


def _combine_kernel(pos_ref, hb_ref, h_ref, p_ref, w_ref, y_hbm, wp_ref, wpg_ref, lg_ref, lb_ref,
                    o_ref, ob_ref, y_buf, sem):
    tm = h_ref.shape[0]

    for k in range(tm):
        for j in range(TOP_K):
            _row_copy(y_hbm, y_buf.at[j], sem.at[0], pos_ref[0, TOP_K * k + j], k).start(priority=j % 2)
    hb = hb_ref[...]
    ple = _dot(p_ref[...].astype(BF16), wp_ref[...]) * _sigmoid(_dot(hb, wpg_ref[...]))
    acc = ALPHA * h_ref[...] + ple
    for k in range(tm):
        for j in range(TOP_K):
            _row_copy(y_hbm, y_buf.at[j], sem.at[0], 0, k).wait()
    w = w_ref[...]
    for j in range(TOP_K):
        acc = acc + w[:, j:j + 1] * y_buf[j]
    y = _ln_rows(acc, lg_ref[...], lb_ref[...])
    o_ref[...] = y
    ob_ref[...] = y.astype(BF16)


def _combine_call(hb, h, p, wsel, pos, y_sorted, lw):
    n = h.shape[0]
    tm = MOE_TM

    def rows(w):
        return pl.BlockSpec((tm, w), lambda i: (i, 0))

    def const(shape):
        return pl.BlockSpec(shape, lambda i: (0, 0))

    return pl.pallas_call(
        _combine_kernel, grid=(n // tm,),
        in_specs=[pl.BlockSpec((None, 1, tm * TOP_K), lambda i: (i, 0, 0), memory_space=pltpu.SMEM),
                  rows(D_MODEL), rows(D_MODEL), rows(PLE_DIM), rows(LANES), pl.BlockSpec(memory_space=pl.ANY),
                  const((PLE_DIM, D_MODEL)), const((D_MODEL, D_MODEL)), const((1, D_MODEL)), const((1, D_MODEL))],
        out_specs=[rows(D_MODEL), rows(D_MODEL)],
        out_shape=[jax.ShapeDtypeStruct((n, D_MODEL), F32), jax.ShapeDtypeStruct((n, D_MODEL), BF16)],
        scratch_shapes=[pltpu.VMEM((TOP_K, tm, D_MODEL), F32), pltpu.SemaphoreType.DMA((1,))],
        compiler_params=_cparams(("arbitrary",)), name="moe_combine")(
            pos, hb, h, p, wsel, y_sorted, lw["w_ple"], lw["w_pleg"], lw["ln2_g"], lw["ln2_b"])


def _ffn_kernel(*refs, weighted, precise):
    it = iter(refs)
    hb_ref, h_ref, p_ref = next(it), next(it), next(it)
    comb_ref = next(it) if weighted else None
    wg, wu, wd, wp, wpg = [_take_w(it, precise) for _ in range(5)]
    lg_ref, lb_ref, o_ref, ob_ref, acc_sc = next(it), next(it), next(it), next(it), next(it)
    j = pl.program_id(1)
    hb = h_ref[...] if precise else hb_ref[...]

    @pl.when(j == 0)
    def _():
        ple = _mm(p_ref[...], _ld(wp), precise) * _sigmoid(_mm(hb, _ld(wpg), precise))
        acc_sc[...] = ALPHA * h_ref[...] + ple

    gate = _mm(hb, _ld(wg), precise)
    up = _mm(hb, _ld(wu), precise)
    out = _mm((gate * _sigmoid(gate)) * up, _ld(wd), precise)
    if weighted:
        comb = comb_ref[...]
        lane = lax.broadcasted_iota(jnp.int32, comb.shape, 1)
        out = jnp.sum(jnp.where(lane == j, comb, 0.0), axis=-1, keepdims=True) * out
    acc_sc[...] += out

    @pl.when(j == pl.num_programs(1) - 1)
    def _():
        y = _ln_rows(acc_sc[...], lg_ref[...], lb_ref[...])
        o_ref[...] = y
        ob_ref[...] = y.astype(BF16)


def _ffn_call(hb, h, p, comb, lw, tm, precise):
    n = h.shape[0]
    weighted = comb is not None
    if weighted:
        n_e = N_EXPERTS
        wg_spec = pl.BlockSpec((None, D_MODEL, D_FF_EXPERT), lambda i, j: (j, 0, 0))
        wd_spec = pl.BlockSpec((None, D_FF_EXPERT, D_MODEL), lambda i, j: (j, 0, 0))
    else:
        n_e = D_FF // D_FF_EXPERT
        wg_spec = pl.BlockSpec((D_MODEL, D_FF_EXPERT), lambda i, j: (0, j))
        wd_spec = pl.BlockSpec((D_FF_EXPERT, D_MODEL), lambda i, j: (j, 0))

    def rows(w):
        return pl.BlockSpec((tm, w), lambda i, j: (i, 0))

    def const(shape):
        return pl.BlockSpec(shape, lambda i, j: (0, 0))

    wspecs, wargs = [], []
    for name, spec in (("w_gate", wg_spec), ("w_up", wg_spec), ("w_down", wd_spec),
                       ("w_ple", const((PLE_DIM, D_MODEL))), ("w_pleg", const((D_MODEL, D_MODEL)))):
        ws = _wargs(lw, name, precise)
        wargs += ws
        wspecs += [spec] * len(ws)
    return pl.pallas_call(
        functools.partial(_ffn_kernel, weighted=weighted, precise=precise), grid=(n // tm, n_e),
        in_specs=[rows(D_MODEL), rows(D_MODEL), rows(PLE_DIM)] + [rows(LANES)] * weighted + wspecs
        + [const((1, D_MODEL)), const((1, D_MODEL))],
        out_specs=[rows(D_MODEL), rows(D_MODEL)],
        out_shape=[jax.ShapeDtypeStruct((n, D_MODEL), F32), jax.ShapeDtypeStruct((n, D_MODEL), BF16)],
        scratch_shapes=[pltpu.VMEM((tm, D_MODEL), F32)],
        compiler_params=_cparams(("parallel", "arbitrary")), name="ffn")(
            hb, h, p, *([comb] * weighted), *wargs, lw["ln2_g"], lw["ln2_b"])


def _layer_weights(i, W):
    w_in = W["w_in"][i]
    sizes = [D_AV, D_AV, D_B, D_B, D_B, FOX_HEADS, D_C, CONV_DIM, SSM_HEADS, N_BRANCH * D_MODEL]
    o = [0] + [int(s) for s in np.cumsum(sizes)]
    seg = [w_in[:, o[k]:o[k + 1]] for k in range(len(sizes))]
    w_u, w_v, w_q, w_k, w_va, w_f, w_z, w_xbc, w_dt, w_gate = seg
    pad = jnp.zeros((D_MODEL, LANES - SSM_HEADS - FOX_HEADS), F32)

    def lane_row(v, lane0=0):
        return jnp.pad(v.astype(F32), (lane0, LANES - lane0 - v.shape[0])).reshape(1, LANES)

    lw = dict(
        fox_b=lane_row(W["fox_b_forget"][i]),
        gmlp_g=W["gmlp_ln_g"][i], gmlp_b=W["gmlp_ln_b"][i], gmlp_w=W["gmlp_w_spatial"][i], gmlp_bs=W["gmlp_b_spatial"][i],
        conv_w=W["ssm_conv_w"][i], conv_b=W["ssm_conv_b"][i].reshape(1, CONV_DIM),
        dtb_row=lane_row(W["ssm_dt_bias"][i], DT_LANE0), dtb_col=W["ssm_dt_bias"][i].reshape(SSM_HEADS, 1),
        alog_row=lane_row(W["ssm_a_log"][i], DT_LANE0), alog_col=W["ssm_a_log"][i].reshape(SSM_HEADS, 1),
        alog_e=jnp.repeat(W["ssm_a_log"][i], SSM_HD).reshape(1, D_C),
        dvec=jnp.repeat(W["ssm_d"][i], SSM_HD).reshape(1, D_C), norm_g=W["ssm_norm_g"][i].reshape(1, D_C),
        ln1_g=W["ln1_g"][i].reshape(1, D_MODEL), ln1_b=W["ln1_b"][i].reshape(1, D_MODEL),
        ln2_g=W["ln2_g"][i].reshape(1, D_MODEL), ln2_b=W["ln2_b"][i].reshape(1, D_MODEL),
    )
    j = i // 2
    mats = dict(
        w_main=jnp.concatenate([w_xbc, w_z, w_u, w_v], axis=1), w_bgate=w_gate, w_k=w_k, w_va=w_va,
        w_small=jnp.concatenate([w_f, w_dt, pad], axis=1),
        w_qt=jnp.transpose(w_q), w_vt=jnp.transpose(w_va),
        w_a=W["w_branch_a"][i], w_b=W["w_branch_b"][i], w_c=W["w_branch_c"][i], w_o=W["w_out"][i],
        w_ple=W["ple_w_proj"][i], w_pleg=W["ple_w_gate"][i],
    )
    if i % 2 == 0:
        mats.update(w_gate=W["ffn_w_gate"][j], w_up=W["ffn_w_up"][j], w_down=W["ffn_w_down"][j])
    else:
        mats.update(w_gate=W["moe_w_gate"][j], w_up=W["moe_w_up"][j], w_down=W["moe_w_down"][j],
                    wr=jnp.pad(W["moe_w_router"][j], ((0, 0), (0, LANES - N_EXPERTS))))
        lw["br"] = lane_row(W["moe_b_router"][j])
    for name, w in mats.items():
        c = w * np.float32(2.0**16 + 1.0)
        hi_f = c - (c - w)
        lw[name] = hi_f.astype(BF16)
        lw[name + "_lo"] = (w - hi_f).astype(BF16)
    return lw


def _trunk_layer(x, xb, p, i, lw, bsz, t, fox_cache, ssm0, conv0, tm, precise):
    n = bsz * t
    act = _act_dtype(precise)
    xin = x if precise else xb
    tmm = min(tm, t)
    tbig = min(n, PROJ_TM)
    proj = _matmul_call(xin, lw, "w_main", tbig, MAIN_TN, precise, "in_proj")
    gates_sig = _matmul_call(xin, lw, "w_bgate", tbig, GATE_TN, precise, "in_proj_gates", gate=True)
    k_new, v_new, small, logf = _kvs_call(xin, lw, tm, precise)

    w_eff, bias_eff = _gmlp_weights(lw["gmlp_w"], lw["gmlp_bs"], t)
    y_a, v_n = _gmlp_call(proj, lw["gmlp_g"], lw["gmlp_b"], w_eff, bias_eff, min(n, TOKEN_TM), precise)

    k_new = k_new.reshape(bsz, t, D_B)
    v_new = v_new.reshape(bsz, t, FOX_HEADS, FOX_HD)
    logf = logf.reshape(bsz, t, FOX_HEADS)
    q_t, v_t = _qvt_call(xin, lw, tmm, bsz, t, precise)
    if fox_cache is None:
        k_all, logf_all, q_off = k_new, logf, 0
    else:
        k_c, v_c, logf_c = fox_cache
        q_off = k_c.shape[1]
        k_all = jnp.concatenate([k_c.reshape(bsz, q_off, D_B), k_new], axis=1)
        v_t = jnp.concatenate([jnp.transpose(v_c.reshape(bsz, q_off, D_B), (0, 2, 1)).astype(act), v_t], axis=2)
        logf_all = jnp.concatenate([logf_c, logf], axis=1)
    t_k = k_all.shape[1]
    k_a = _keys_call(k_all, logf_all, FOX_TILE if t_k % FOX_TILE == 0 else t_k, precise)
    tq, tk = (FOX_TILE, FOX_TILE) if t % FOX_TILE == 0 else (t, t_k)
    y_bt = _fox_call(q_t, k_a, v_t, tq, tk, q_off, FOX_NH, precise)

    dt_t = jnp.transpose(small[:, DT_LANE0:DT_LANE0 + SSM_HEADS].reshape(bsz, t, SSM_HEADS), (0, 2, 1))
    y_c, h_fin, conv_new = _ssd_call(proj, small, dt_t, ssm0, conv0, lw, bsz, t, precise)

    x1, x1b = _merge_call(y_a, y_bt, y_c, gates_sig, x, lw, tmm, bsz, t, precise)
    if i % 2 == 0:
        x2, x2b = _ffn_call(x1b, x1, p, None, lw, tm, precise)
    else:
        comb, ids, wsel = _router_call(x1, lw, tm)
        if precise or n % MOE_TM or n < N_EXPERTS * MOE_ROWS:
            x2, x2b = _ffn_call(x1b, x1, p, comb, lw, tm, precise)
        else:
            src_tok, pos, tile_expert, n_used = _moe_plan(ids, n)
            y_sorted = _experts_call(x1, src_tok, tile_expert, n_used, lw)
            x2, x2b = _combine_call(x1b, x1, p, wsel, pos, y_sorted, lw)
    return x2, x2b, (k_new.reshape(bsz, t, FOX_HEADS, FOX_HD), v_new, logf, h_fin, conv_new,
                     v_n.reshape(bsz, t, D_AV))


def kernel(x_prompt, x_sample, p_prompt, p_sample, cache_fox_k, cache_fox_v, cache_fox_logf, state_ssm, state_conv, ln0_g, ln0_b, w_in, fox_b_forget, gmlp_ln_g, gmlp_ln_b, gmlp_w_spatial, gmlp_b_spatial, ssm_conv_w, ssm_conv_b, ssm_dt_bias, ssm_a_log, ssm_d, ssm_norm_g, w_branch_a, w_branch_b, w_branch_c, w_out, ln1_g, ln1_b, ln2_g, ln2_b, ffn_w_gate, ffn_w_up, ffn_w_down, moe_w_router, moe_b_router, moe_w_gate, moe_w_up, moe_w_down, ple_w_proj, ple_w_gate):
    W = dict(w_in=w_in, fox_b_forget=fox_b_forget, gmlp_ln_g=gmlp_ln_g, gmlp_ln_b=gmlp_ln_b,
             gmlp_w_spatial=gmlp_w_spatial, gmlp_b_spatial=gmlp_b_spatial, ssm_conv_w=ssm_conv_w,
             ssm_conv_b=ssm_conv_b, ssm_dt_bias=ssm_dt_bias, ssm_a_log=ssm_a_log, ssm_d=ssm_d,
             ssm_norm_g=ssm_norm_g, w_branch_a=w_branch_a, w_branch_b=w_branch_b,
             w_branch_c=w_branch_c, w_out=w_out, ln1_g=ln1_g, ln1_b=ln1_b, ln2_g=ln2_g, ln2_b=ln2_b,
             ffn_w_gate=ffn_w_gate, ffn_w_up=ffn_w_up, ffn_w_down=ffn_w_down,
             moe_w_router=moe_w_router, moe_b_router=moe_b_router, moe_w_gate=moe_w_gate, moe_w_up=moe_w_up,
             moe_w_down=moe_w_down, ple_w_proj=ple_w_proj, ple_w_gate=ple_w_gate)
    bp, tp, _ = x_prompt.shape
    bs, ts, _ = x_sample.shape
    tm_p = TOKEN_TM if (bp * tp) % TOKEN_TM == 0 else bp * tp
    tm_s = bs * ts
    xp, xpb = _layer_norm_call(x_prompt.reshape(bp * tp, D_MODEL), ln0_g, ln0_b, tm_p)
    xs, xsb = _layer_norm_call(x_sample.reshape(bs * ts, D_MODEL), ln0_g, ln0_b, tm_s)
    outs_p, outs_s = [], []
    for i in range(DEPTH):
        lw = _layer_weights(i, W)
        ssm0 = jnp.zeros((bp, SSM_HEADS, SSM_HD, SSM_N), F32)
        conv0 = jnp.zeros((bp, CONV_W - 1, CONV_DIM), F32)
        xp, xpb, st_p = _trunk_layer(xp, xpb, p_prompt[i].reshape(bp * tp, PLE_DIM), i, lw, bp, tp,
                                     None, ssm0, conv0, tm_p, False)
        xs, xsb, st_s = _trunk_layer(xs, xsb, p_sample[i].reshape(bs * ts, PLE_DIM), i, lw, bs, ts,
                                     (cache_fox_k[i], cache_fox_v[i], cache_fox_logf[i]),
                                     state_ssm[i], state_conv[i], tm_s, True)
        outs_p.append(st_p)
        outs_s.append(st_s)

    def stack(outs, k):
        return jnp.stack([o[k] for o in outs])

    return (xp.reshape(bp, tp, D_MODEL), xs.reshape(bs, ts, D_MODEL),
            stack(outs_p, 0), stack(outs_p, 1), stack(outs_p, 2), stack(outs_p, 3), stack(outs_p, 4),
            stack(outs_s, 0), stack(outs_s, 1), stack(outs_s, 2), stack(outs_s, 3), stack(outs_s, 4), stack(outs_s, 5))
```

```python
import functools
import math

import numpy as np
import jax
import jax.numpy as jnp
from jax import lax
from jax.experimental import pallas as pl
from jax.experimental.pallas import tpu as pltpu

F32 = jnp.float32
BF16 = jnp.bfloat16

D_MODEL = 1024
DEPTH = 2
CHUNK = 64
PLE_DIM = 256
GMLP_CHUNK = 128
GMLP_GROUPS = 4
D_AV = D_MODEL // 2
GMLP_GROUP_W = D_AV // GMLP_GROUPS
FOX_HD = 64
D_B = D_MODEL // 2
FOX_HEADS = D_B // FOX_HD
D_C = D_MODEL
SSM_HD = 64
SSM_HEADS = D_C // SSM_HD
SSM_N = 128
SSM_GROUPS = 2
CONV_W = 4
CONV_DIM = D_C + 2 * SSM_GROUPS * SSM_N
N_BRANCH = 3
D_FF = 11 * D_MODEL // 4
N_EXPERTS = 8
D_FF_EXPERT = D_FF // 2
ALPHA = (2.0 * DEPTH) ** 0.25
LN_EPS = 1e-5
RMS_EPS = 1e-5
NEG_INF = -1e30
LOG2E = math.log2(math.e)

LANES = 128
VMEM_LIMIT = 56 * 2**20
N_MAIN = CONV_DIM + D_C + 2 * D_AV
OFF_XBC, OFF_Z, OFF_U, OFF_V = 0, 1536, 2560, 3072
MAIN_TN = 512
GATE_TN = 1024
DT_LANE0 = FOX_HEADS
SSD_L = 128
CONV_PAD = 8
FOX_NBIAS = 3
FOX_DK = 128
FOX_DV = 80
FOX_QSUB = 256
FOX_TILE = 512
FOX_NH = 4
TOKEN_TM = 512
PROJ_TM = 2048


def _cparams(sem):
    return pltpu.CompilerParams(dimension_semantics=sem, vmem_limit_bytes=VMEM_LIMIT)


def _sigmoid(x):
    return 1.0 / (1.0 + jnp.exp(-x))


def _softplus(x):
    return jnp.maximum(x, 0.0) + jnp.log1p(jnp.exp(-jnp.abs(x)))


def _gelu(x):
    c = np.float32(np.sqrt(2.0 / np.pi))
    return x * (0.5 * (1.0 + jnp.tanh(c * (x + 0.044715 * (x * x * x)))))


def _ln_rows(x, g, b):
    mu = jnp.mean(x, axis=-1, keepdims=True)
    xc = x - mu
    var = jnp.mean(xc * xc, axis=-1, keepdims=True)
    return xc * lax.rsqrt(var + LN_EPS) * g + b


def _bf16_part(x):
    u = lax.bitcast_convert_type(x, jnp.uint32) & jnp.uint32(0xFFFF0000)
    return lax.bitcast_convert_type(u, F32)


def _split2(x):
    hi = _bf16_part(x)
    return hi.astype(BF16), (x - hi).astype(BF16)


def _split3(x):
    hi = _bf16_part(x)
    r1 = x - hi
    mid = _bf16_part(r1)
    return hi.astype(BF16), mid.astype(BF16), (r1 - mid).astype(BF16)


def _dot(a, b):
    return jnp.dot(a, b, preferred_element_type=F32)


def _dot_nt(a, b):
    return lax.dot_general(a, b, (((1,), (1,)), ((), ())), preferred_element_type=F32)


def _dot_tn(a, b):
    return lax.dot_general(a, b, (((0,), (0,)), ((), ())), preferred_element_type=F32)


_DOTS = {"nn": _dot, "nt": _dot_nt, "tn": _dot_tn}


def _mm(a, b, precise, dims="nn"):
    dot = _DOTS[dims]
    if not precise:
        a = a[0] if isinstance(a, tuple) else a.astype(BF16)
        b = b[0] if isinstance(b, tuple) else b.astype(BF16)
        return dot(a, b)
    ah, al = a if isinstance(a, tuple) else _split2(a)
    bh, bl = b if isinstance(b, tuple) else _split2(b)
    return (dot(ah, bh) + dot(al, bh)) + dot(ah, bl)


def _take_w(it, precise):
    hi = next(it)
    return hi, (next(it) if precise else None)


def _ld(w, idx=...):
    return w[0][idx], (None if w[1] is None else w[1][idx])


def _wargs(lw, name, precise):
    return [lw[name], lw[name + "_lo"]] if precise else [lw[name]]


def _act_dtype(precise):
    return F32 if precise else BF16


def _ln_kernel(x_ref, g_ref, b_ref, o_ref, ob_ref):
    y = _ln_rows(x_ref[...], g_ref[...], b_ref[...])
    o_ref[...] = y
    ob_ref[...] = y.astype(BF16)


def _layer_norm_call(x, g, b, tm):
    n, d = x.shape
    row = pl.BlockSpec((tm, d), lambda i: (i, 0))
    vec = pl.BlockSpec((1, d), lambda i: (0, 0))
    return pl.pallas_call(
        _ln_kernel, grid=(n // tm,), in_specs=[row, vec, vec], out_specs=[row, row],
        out_shape=[jax.ShapeDtypeStruct((n, d), F32), jax.ShapeDtypeStruct((n, d), BF16)],
        compiler_params=_cparams(("parallel",)), name="ln_in")(x, g.reshape(1, d), b.reshape(1, d))


def _mm_kernel(*refs, precise, gate):
    it = iter(refs)
    x_ref = next(it)
    w = _take_w(it, precise)
    o_ref = next(it)
    y = _mm(x_ref[...], _ld(w), precise)
    o_ref[...] = (_sigmoid(y) if gate else y).astype(o_ref.dtype)


def _matmul_call(x, lw, wname, tm, tn, precise, name, gate=False):
    m, k = x.shape
    ws = _wargs(lw, wname, precise)
    n = ws[0].shape[1]
    wspec = pl.BlockSpec((k, tn), lambda i, j: (0, j))
    return pl.pallas_call(
        functools.partial(_mm_kernel, precise=precise, gate=gate), grid=(m // tm, n // tn),
        in_specs=[pl.BlockSpec((tm, k), lambda i, j: (i, 0))] + [wspec] * len(ws),
        out_specs=pl.BlockSpec((tm, tn), lambda i, j: (i, j)),
        out_shape=jax.ShapeDtypeStruct((m, n), _act_dtype(precise) if gate else F32),
        compiler_params=_cparams(("parallel", "parallel")), name=name)(x, *ws)


def _gmlp_kernel(*refs, rows, precise):
    it = iter(refs)
    u_ref, v_ref, g_ref, b_ref = next(it), next(it), next(it), next(it)
    w = _take_w(it, precise)
    bias_ref, ya_ref, vn_ref = next(it), next(it), next(it)
    u = _gelu(u_ref[...])
    vn = _ln_rows(_gelu(v_ref[...]), g_ref[...], b_ref[...])
    vn_ref[...] = vn
    vb = vn if precise else vn.astype(BF16)
    for c in range(rows // GMLP_CHUNK):
        r0 = c * GMLP_CHUNK
        for g in range(GMLP_GROUPS):
            c0 = g * GMLP_GROUP_W
            mixed = _mm(_ld(w, g), vb[r0:r0 + GMLP_CHUNK, c0:c0 + GMLP_GROUP_W], precise)
            mixed = mixed + bias_ref[:, c0:c0 + GMLP_GROUP_W]
            ya_ref[r0:r0 + GMLP_CHUNK, c0:c0 + GMLP_GROUP_W] = (
                u[r0:r0 + GMLP_CHUNK, c0:c0 + GMLP_GROUP_W] * mixed).astype(ya_ref.dtype)


def _gmlp_call(proj, ln_g, ln_b, w_eff, bias_eff, rows, precise):
    n = proj.shape[0]
    ublk, vblk = OFF_U // D_AV, OFF_V // D_AV
    vec = pl.BlockSpec((1, D_AV), lambda i: (0, 0))
    ws = list(_split2(w_eff)) if precise else [w_eff.astype(BF16)]
    wspec = pl.BlockSpec((GMLP_GROUPS, GMLP_CHUNK, GMLP_CHUNK), lambda i: (0, 0, 0))
    return pl.pallas_call(
        functools.partial(_gmlp_kernel, rows=rows, precise=precise), grid=(n // rows,),
        in_specs=[pl.BlockSpec((rows, D_AV), lambda i: (i, ublk)), pl.BlockSpec((rows, D_AV), lambda i: (i, vblk)),
                  vec, vec] + [wspec] * len(ws) + [pl.BlockSpec((GMLP_CHUNK, D_AV), lambda i: (0, 0))],
        out_specs=[pl.BlockSpec((rows, D_AV), lambda i: (i, 0)), pl.BlockSpec((rows, D_AV), lambda i: (i, 0))],
        out_shape=[jax.ShapeDtypeStruct((n, D_AV), _act_dtype(precise)), jax.ShapeDtypeStruct((n, D_AV), F32)],
        compiler_params=_cparams(("parallel",)), name="gmlp")(
            proj, proj, ln_g.reshape(1, D_AV), ln_b.reshape(1, D_AV), *ws, bias_eff)


def _gmlp_weights(w_s, b_s, t):
    l = min(GMLP_CHUNK, t)
    pos = np.arange(l)
    mask = (pos[None, :] // CHUNK) <= (pos[:, None] // CHUNK)
    w = jnp.where(mask[None], w_s[:, :l, :l], 0.0)
    bias = jnp.transpose(b_s[:, :l])
    reps = GMLP_CHUNK // l
    if reps > 1:
        eye = jnp.eye(reps, dtype=w.dtype)
        w = jnp.einsum("ab,gij->gaibj", eye, w).reshape(GMLP_GROUPS, GMLP_CHUNK, GMLP_CHUNK)
        bias = jnp.tile(bias, (reps, 1))
    return w, jnp.repeat(bias, GMLP_GROUP_W, axis=1)


def _kvs_kernel(*refs, precise):
    it = iter(refs)
    x_ref = next(it)
    wk, wv, ws = [_take_w(it, precise) for _ in range(3)]
    fb_ref, k_ref, v_ref, s_ref, lf_ref = [next(it) for _ in range(5)]
    x = x_ref[...]
    k_ref[...] = _mm(x, _ld(wk), precise)
    v_ref[...] = _mm(x, _ld(wv), precise)
    small = _mm(x, _ld(ws), precise)
    s_ref[...] = small
    lf_ref[...] = (-_softplus(-(small + fb_ref[...])))[:, :FOX_HEADS]


def _kvs_call(x, lw, tm, precise):
    n = x.shape[0]
    wargs, wspecs = [], []
    for name, width in (("w_k", D_B), ("w_va", D_B), ("w_small", LANES)):
        ws = _wargs(lw, name, precise)
        wargs += ws
        wspecs += [pl.BlockSpec((D_MODEL, width), lambda i: (0, 0))] * len(ws)

    def rows(w):
        return pl.BlockSpec((tm, w), lambda i: (i, 0))

    return pl.pallas_call(
        functools.partial(_kvs_kernel, precise=precise), grid=(n // tm,),
        in_specs=[rows(D_MODEL)] + wspecs + [pl.BlockSpec((1, LANES), lambda i: (0, 0))],
        out_specs=[rows(D_B), rows(D_B), rows(LANES), rows(FOX_HEADS)],
        out_shape=[jax.ShapeDtypeStruct((n, D_B), F32), jax.ShapeDtypeStruct((n, D_B), F32),
                   jax.ShapeDtypeStruct((n, LANES), F32), jax.ShapeDtypeStruct((n, FOX_HEADS), F32)],
        compiler_params=_cparams(("parallel",)), name="fox_kv")(x, *wargs, lw["fox_b"])


def _qvt_kernel(*refs, precise):
    it = iter(refs)
    x_ref = next(it)
    wq, wv = _take_w(it, precise), _take_w(it, precise)
    qt_ref, vt_ref = next(it), next(it)
    x = x_ref[...]
    qt_ref[...] = (_mm(_ld(wq), x, precise, "nt") * (FOX_HD ** -0.5 * LOG2E)).astype(qt_ref.dtype)
    vt_ref[...] = _mm(_ld(wv), x, precise, "nt").astype(vt_ref.dtype)


def _qvt_call(x, lw, tm, bsz, t, precise):
    n = x.shape[0]
    per_b = t // tm
    act = _act_dtype(precise)
    ws = _wargs(lw, "w_qt", precise) + _wargs(lw, "w_vt", precise)
    out = pl.BlockSpec((None, D_B, tm), lambda i: (i // per_b, 0, i % per_b))
    return pl.pallas_call(
        functools.partial(_qvt_kernel, precise=precise), grid=(n // tm,),
        in_specs=[pl.BlockSpec((tm, D_MODEL), lambda i: (i, 0))]
        + [pl.BlockSpec((D_B, D_MODEL), lambda i: (0, 0))] * len(ws),
        out_specs=[out, out],
        out_shape=[jax.ShapeDtypeStruct((bsz, D_B, t), act), jax.ShapeDtypeStruct((bsz, D_B, t), act)],
        compiler_params=_cparams(("parallel",)), name="fox_qv_t")(x, *ws)


def _place(x, sel, precise):
    if not precise:
        return _dot(x.astype(BF16), sel)
    hi, mid, lo = _split3(x)
    return (_dot(hi, sel) + _dot(mid, sel)) + _dot(lo, sel)


def _keys_kernel(k_ref, lf_ref, tril_ref, selk_ref, selg_ref, o_ref, carry_ref, *, precise):
    @pl.when(pl.program_id(1) == 0)
    def _():
        carry_ref[...] = jnp.zeros_like(carry_ref)

    tril = tril_ref[...]
    hi, mid, lo = _split3(lf_ref[...])
    f = (_dot(tril, hi) + _dot(tril, mid)) + _dot(tril, lo) + carry_ref[0:1, :]
    tc = f.shape[0]
    carry_ref[...] = jnp.broadcast_to(f[tc - 1:tc, :], carry_ref.shape)
    g = f * (-LOG2E)
    g1 = _bf16_part(g)
    r = g - g1
    g2 = _bf16_part(r)
    gcat = jnp.concatenate([g1, g2, r - g2], axis=1)
    ka = _place(k_ref[...], selk_ref[...], precise) + _place(gcat, selg_ref[...], precise)
    o_ref[...] = ka.astype(o_ref.dtype)


def _keys_call(k_all, logf_all, tc, precise):
    bsz, t_k, _ = k_all.shape
    sel_k = np.zeros((D_B, FOX_HEADS * FOX_DK), np.float32)
    sel_g = np.zeros((FOX_NBIAS * FOX_HEADS, FOX_HEADS * FOX_DK), np.float32)
    for h in range(FOX_HEADS):
        sel_k[h * FOX_HD + np.arange(FOX_HD), h * FOX_DK + np.arange(FOX_HD)] = 1.0
        for j in range(FOX_NBIAS):
            sel_g[j * FOX_HEADS + h, h * FOX_DK + FOX_HD + j] = 1.0
    tril = jnp.tril(jnp.ones((tc, tc), F32)).astype(BF16)
    return pl.pallas_call(
        functools.partial(_keys_kernel, precise=precise), grid=(bsz, t_k // tc),
        in_specs=[pl.BlockSpec((None, tc, D_B), lambda b, j: (b, j, 0)),
                  pl.BlockSpec((None, tc, FOX_HEADS), lambda b, j: (b, j, 0)),
                  pl.BlockSpec((tc, tc), lambda b, j: (0, 0)),
                  pl.BlockSpec(sel_k.shape, lambda b, j: (0, 0)), pl.BlockSpec(sel_g.shape, lambda b, j: (0, 0))],
        out_specs=pl.BlockSpec((None, tc, FOX_HEADS * FOX_DK), lambda b, j: (b, j, 0)),
        out_shape=jax.ShapeDtypeStruct((bsz, t_k, FOX_HEADS * FOX_DK), _act_dtype(precise)),
        scratch_shapes=[pltpu.VMEM((8, FOX_HEADS), F32)],
        compiler_params=_cparams(("parallel", "arbitrary")), name="fox_keys")(
            k_all, logf_all, tril, jnp.asarray(sel_k, BF16), jnp.asarray(sel_g, BF16))


def _fox_kernel(qt_ref, ka_ref, vt_ref, o_ref, qa_sc, m_sc, acc_sc, s_sc, mb_sc, p_sc, al_sc,
                *, tq, tk, qs, q_off, precise):
    i = pl.program_id(2)
    act = qa_sc.dtype
    nh = qa_sc.shape[0]
    rows = lax.broadcasted_iota(jnp.int32, (FOX_DK - FOX_HD, tq), 0)
    ones_rows = jnp.where(rows < FOX_NBIAS, 1.0, 0.0).astype(act)
    for h in range(nh):
        qa_sc[h] = jnp.concatenate([qt_ref[h * FOX_HD:(h + 1) * FOX_HD, :], ones_rows], axis=0)
    vrows = lax.broadcasted_iota(jnp.int32, (FOX_DV - FOX_HD, tk), 0)
    v_extra = jnp.where(vrows < 1, 1.0, 0.0).astype(act)
    m_sc[...] = jnp.full_like(m_sc, -jnp.inf)
    acc_sc[...] = jnp.zeros_like(acc_sc)
    first_q = q_off + i * tq
    subs = [(h, c, slice(c * qs, (c + 1) * qs)) for h in range(nh) for c in range(tq // qs)]

    def scores(k0):
        for h, _, sl in subs:
            ka = ka_ref[pl.ds(k0, tk), h * FOX_DK:(h + 1) * FOX_DK]
            s = _mm(ka, qa_sc[h, :, sl], precise)
            s_sc[h, :, sl] = s
            mb_sc[h, :, sl] = jnp.max(s, axis=0, keepdims=True)

    def consume(k0, masked):
        for h, c, sl in subs:
            vt = jnp.concatenate([vt_ref[h * FOX_HD:(h + 1) * FOX_HD, pl.ds(k0, tk)], v_extra], axis=0)
            s = s_sc[h, :, sl]
            if masked:
                kpos = k0 + lax.broadcasted_iota(jnp.int32, (tk, qs), 0)
                qpos = first_q + c * qs + lax.broadcasted_iota(jnp.int32, (tk, qs), 1)
                s = jnp.where(kpos <= qpos, s, NEG_INF)
                mb = jnp.max(s, axis=0, keepdims=True)
            else:
                mb = mb_sc[h, :, sl]
            m_prev = m_sc[h, :, sl]
            m_new = jnp.maximum(m_prev, mb)
            alpha = jnp.exp2(m_prev - m_new)
            p = jnp.exp2(s - m_new)
            acc_sc[h, :, sl] = alpha * acc_sc[h, :, sl] + _mm(vt, p, precise)
            m_sc[h, :, sl] = m_new

    def softmax_only(_k0):
        for h, _, sl in subs:
            m_prev = m_sc[h, :, sl]
            m_new = jnp.maximum(m_prev, mb_sc[h, :, sl])
            al_sc[h, :, sl] = jnp.exp2(m_prev - m_new)
            p_sc[h, :, sl] = jnp.exp2(s_sc[h, :, sl] - m_new).astype(p_sc.dtype)
            m_sc[h, :, sl] = m_new

    def values(k0):
        for h, _, sl in subs:
            vt = jnp.concatenate([vt_ref[h * FOX_HD:(h + 1) * FOX_HD, pl.ds(k0, tk)], v_extra], axis=0)
            acc_sc[h, :, sl] = al_sc[h, :, sl] * acc_sc[h, :, sl] + _mm(vt, p_sc[h, :, sl], precise)

    scores(0)
    if ka_ref.shape[0] == tk:
        consume(0, True)
    else:
        assert tq == tk and q_off == 0
        @pl.when(i > 0)
        def _():
            softmax_only(0)
            scores(tk)

        def body(j, c):
            k0 = pl.multiple_of(j * tk, tk)
            values(pl.multiple_of(k0 - tk, tk))
            softmax_only(k0)
            scores(pl.multiple_of(k0 + tk, tk))
            return c

        lax.fori_loop(1, i, body, 0)

        @pl.when(i > 0)
        def _():
            values(pl.multiple_of(i * tk - tk, tk))

        consume(pl.multiple_of(i * tk, tk), True)
    for h in range(nh):
        acc = acc_sc[h]
        o_ref[h * FOX_HD:(h + 1) * FOX_HD, :] = (acc[:FOX_HD] / acc[FOX_HD:FOX_HD + 1]).astype(o_ref.dtype)


def _fox_call(q_t, k_a, v_t, tq, tk, q_off, nh, precise):
    bsz, _, t_q = q_t.shape
    t_k = k_a.shape[1]
    qs = min(FOX_QSUB, tq)
    act = _act_dtype(precise)
    qo = pl.BlockSpec((None, nh * FOX_HD, tq), lambda b, hh, i: (b, hh, i))
    return pl.pallas_call(
        functools.partial(_fox_kernel, tq=tq, tk=tk, qs=qs, q_off=q_off, precise=precise),
        grid=(bsz, FOX_HEADS // nh, t_q // tq),
        in_specs=[qo, pl.BlockSpec((None, t_k, nh * FOX_DK), lambda b, hh, i: (b, 0, hh)),
                  pl.BlockSpec((None, nh * FOX_HD, t_k), lambda b, hh, i: (b, hh, 0))],
        out_specs=qo, out_shape=jax.ShapeDtypeStruct((bsz, D_B, t_q), act),
        scratch_shapes=[pltpu.VMEM((nh, FOX_DK, tq), act), pltpu.VMEM((nh, 1, tq), F32),
                        pltpu.VMEM((nh, FOX_DV, tq), F32), pltpu.VMEM((nh, tk, tq), F32),
                        pltpu.VMEM((nh, 1, tq), F32), pltpu.VMEM((nh, tk, tq), act), pltpu.VMEM((nh, 1, tq), F32)],
        compiler_params=_cparams(("parallel", "parallel", "arbitrary")), name="fox_attn")(q_t, k_a, v_t)


def _ssd_kernel(xbc_ref, z0_ref, z1_ref, dtp_ref, dtt_ref, h0_ref, c0_ref, cw_ref, cb_ref, dtb_row_ref, dtb_col_ref,
                alog_row_ref, alog_col_ref, aloge_ref, dvec_ref, ng_ref, tril_ref, triu_ref, expand_ref,
                yc_ref, hfin_ref, cnew_ref, h_sc, pad_sc, y_sc, *, L, precise):
    c = pl.program_id(1)

    @pl.when(c == 0)
    def _():
        h_sc[...] = h0_ref[...]
        pad_sc[CONV_PAD - (CONV_W - 1):CONV_PAD, :] = c0_ref[...]

    xin = xbc_ref[...]
    pad_sc[CONV_PAD:CONV_PAD + L, :] = xin
    y = cb_ref[...]
    for j in range(CONV_W - 1):
        r0 = CONV_PAD - (CONV_W - 1) + j
        y = y + pad_sc[r0:r0 + L, :] * cw_ref[j:j + 1, :]
    y = y + xin * cw_ref[CONV_W - 1:CONV_W, :]
    tail = pad_sc[CONV_PAD + L - (CONV_W - 1):CONV_PAD + L, :]
    pad_sc[CONV_PAD - (CONV_W - 1):CONV_PAD, :] = tail
    cnew_ref[...] = tail
    act = y * _sigmoid(y)

    dt_c = _softplus(dtp_ref[...] + dtb_row_ref[...])
    a_c = dt_c * (-jnp.exp(alog_row_ref[...]))
    tril = tril_ref[...]
    hi, mid, lo = _split3(a_c)
    acum_c = (_dot(tril, hi) + _dot(tril, mid)) + _dot(tril, lo)
    dt_r = _softplus(dtt_ref[...] + dtb_col_ref[...])
    a_r = dt_r * (-jnp.exp(alog_col_ref[...]))
    triu = triu_ref[...]
    hi, mid, lo = _split3(a_r)
    acum_r = (_dot(hi, triu) + _dot(mid, triu)) + _dot(lo, triu)

    dt_e = _place(dt_c, expand_ref[...], True)
    hi, mid, lo = _split3(dt_e * (-jnp.exp(aloge_ref[...])))
    acum_e = (_dot(tril, hi) + _dot(tril, mid)) + _dot(tril, lo)
    xs = act[:, :D_C]
    e_in = jnp.exp(acum_e)
    xw = xs * (jnp.exp(acum_e[L - 1:L, :] - acum_e) * dt_e)
    dx = dvec_ref[...] * xs

    row = lax.broadcasted_iota(jnp.int32, (L, L), 0)
    col = lax.broadcasted_iota(jnp.int32, (L, L), 1)
    causal = col <= row
    first_of_pair = lax.broadcasted_iota(jnp.int32, (L, 2 * SSM_HD), 1) < SSM_HD
    rep = SSM_HEADS // SSM_GROUPS
    gw = rep * SSM_HD
    for g in range(SSM_GROUPS):
        b0 = D_C + g * SSM_N
        c0 = D_C + SSM_GROUPS * SSM_N + g * SSM_N
        b_g = act[:, b0:b0 + SSM_N]
        c_g = act[:, c0:c0 + SSM_N]
        b_gs = _split2(b_g) if precise else (b_g.astype(BF16), None)
        cb = _mm(c_g, b_gs, precise, "nt")
        h_prev = h_sc[g * rep:(g + 1) * rep].reshape(gw, SSM_N)
        y_off = _mm(c_g, h_prev, precise, "nt") * e_in[:, g * gw:(g + 1) * gw]
        s_g = _mm(xw[:, g * gw:(g + 1) * gw], b_gs, precise, "tn")
        for hh in range(rep):
            h = g * rep + hh
            alast = acum_c[L - 1:L, DT_LANE0 + h:DT_LANE0 + h + 1]
            h_sc[h] = jnp.exp(alast) * h_sc[h] + s_g[hh * SSM_HD:(hh + 1) * SSM_HD, :]
        for pr in range(rep // 2):
            ch0 = g * gw + pr * 2 * SSM_HD
            x_pair = xs[:, ch0:ch0 + 2 * SSM_HD]
            x_pair = _split2(x_pair) if precise else (x_pair.astype(BF16), None)
            y_pair = []
            for hh in (2 * pr, 2 * pr + 1):
                h = g * rep + hh
                lane = DT_LANE0 + h
                decay = jnp.exp(jnp.where(causal, acum_c[:, lane:lane + 1] - acum_r[h:h + 1, :], -jnp.inf))
                y_pair.append(_mm((cb * decay) * dt_r[h:h + 1, :], x_pair, precise))
            y_diag = jnp.where(first_of_pair, y_pair[0], y_pair[1])
            y_sc[:, ch0:ch0 + 2 * SSM_HD] = (y_diag + y_off[:, ch0 - g * gw:ch0 - g * gw + 2 * SSM_HD]) + dx[:, ch0:ch0 + 2 * SSM_HD]

    @pl.when(c == pl.num_programs(1) - 1)
    def _():
        hfin_ref[...] = h_sc[...]

    z = jnp.concatenate([z0_ref[...], z1_ref[...]], axis=1)
    yg =y_sc[...] * (z * _sigmoid(z))
    ms = jnp.mean(yg * yg, axis=-1, keepdims=True)
    yc_ref[...] = ((yg * lax.rsqrt(ms + RMS_EPS)) * ng_ref[...]).astype(yc_ref.dtype)


def _ssd_call(proj, small, dt_t, h0, conv0, lw, bsz, t, precise):
    L = min(SSD_L, t)
    nc = t // L
    zhalf = D_C // 2
    zblk = OFF_Z // zhalf

    def const(shape):
        return pl.BlockSpec(shape, lambda b, c: (0,) * len(shape))

    in_specs = [
        pl.BlockSpec((L, CONV_DIM), lambda b, c: (b * nc + c, OFF_XBC // CONV_DIM)),
        pl.BlockSpec((L, zhalf), lambda b, c: (b * nc + c, zblk)),
        pl.BlockSpec((L, zhalf), lambda b, c: (b * nc + c, zblk + 1)),
        pl.BlockSpec((L, LANES), lambda b, c: (b * nc + c, 0)),
        pl.BlockSpec((None, SSM_HEADS, L), lambda b, c: (b, 0, c)),
        pl.BlockSpec((None, SSM_HEADS, SSM_HD, SSM_N), lambda b, c: (b, 0, 0, 0)),
        pl.BlockSpec((None, CONV_W - 1, CONV_DIM), lambda b, c: (b, 0, 0)),
        const((CONV_W, CONV_DIM)), const((1, CONV_DIM)), const((1, LANES)), const((SSM_HEADS, 1)),
        const((1, LANES)), const((SSM_HEADS, 1)), const((1, D_C)), const((1, D_C)), const((1, D_C)),
        const((L, L)), const((L, L)), const((LANES, D_C)),
    ]
    expand = np.zeros((LANES, D_C), np.float32)
    for h in range(SSM_HEADS):
        expand[DT_LANE0 + h, h * SSM_HD:(h + 1) * SSM_HD] = 1.0
    out_specs = [
        pl.BlockSpec((L, D_C), lambda b, c: (b * nc + c, 0)),
        pl.BlockSpec((None, SSM_HEADS, SSM_HD, SSM_N), lambda b, c: (b, 0, 0, 0)),
        pl.BlockSpec((None, CONV_W - 1, CONV_DIM), lambda b, c: (b, 0, 0)),
    ]
    out_shape = [jax.ShapeDtypeStruct((bsz * t, D_C), _act_dtype(precise)),
                 jax.ShapeDtypeStruct((bsz, SSM_HEADS, SSM_HD, SSM_N), F32),
                 jax.ShapeDtypeStruct((bsz, CONV_W - 1, CONV_DIM), F32)]
    ones = jnp.ones((L, L), F32)
    return pl.pallas_call(
        functools.partial(_ssd_kernel, L=L, precise=precise), grid=(bsz, nc), in_specs=in_specs,
        out_specs=out_specs, out_shape=out_shape,
        scratch_shapes=[pltpu.VMEM((SSM_HEADS, SSM_HD, SSM_N), F32), pltpu.VMEM((CONV_PAD + L, CONV_DIM), F32),
                        pltpu.VMEM((L, D_C), F32)],
        compiler_params=_cparams(("parallel", "arbitrary")), name="ssd")(
            proj, proj, proj, small, dt_t, h0, conv0, lw["conv_w"], lw["conv_b"], lw["dtb_row"], lw["dtb_col"],
            lw["alog_row"], lw["alog_col"], lw["alog_e"], lw["dvec"], lw["norm_g"],
            jnp.tril(ones).astype(BF16), jnp.triu(ones).astype(BF16), jnp.asarray(expand, BF16))


def _merge_kernel(*refs, precise):
    it = iter(refs)
    ya_ref, ybt_ref, yc_ref, g0_ref, g1_ref, g2_ref, x_ref = [next(it) for _ in range(7)]
    wa, wb, wc, wo = [_take_w(it, precise) for _ in range(4)]
    lg_ref, lb_ref, o_ref, ob_ref = next(it), next(it), next(it), next(it)
    merged = g0_ref[...].astype(F32) * _mm(ya_ref[...], _ld(wa), precise)
    merged = merged + g1_ref[...].astype(F32) * _mm(ybt_ref[...], _ld(wb), precise, "tn")
    merged = merged + g2_ref[...].astype(F32) * _mm(yc_ref[...], _ld(wc), precise)
    mix = _mm(merged, _ld(wo), precise)
    y = _ln_rows(ALPHA * x_ref[...] + mix, lg_ref[...], lb_ref[...])
    o_ref[...] = y
    ob_ref[...] = y.astype(BF16)


def _merge_call(ya, yb_t, yc, gates_sig, x, lw, tm, bsz, t, precise):
    n = x.shape[0]
    per_b = t // tm

    def rows(w):
        return pl.BlockSpec((tm, w), lambda i: (i, 0))

    def const(shape):
        return pl.BlockSpec(shape, lambda i: (0, 0))

    gates = [pl.BlockSpec((tm, D_MODEL), lambda i, j=j: (i, j)) for j in range(N_BRANCH)]
    wspecs, wargs = [], []
    for name, k in (("w_a", D_AV), ("w_b", D_B), ("w_c", D_C), ("w_o", D_MODEL)):
        ws = _wargs(lw, name, precise)
        wargs += ws
        wspecs += [const((k, D_MODEL))] * len(ws)
    return pl.pallas_call(
        functools.partial(_merge_kernel, precise=precise), grid=(n // tm,),
        in_specs=[rows(D_AV), pl.BlockSpec((None, D_B, tm), lambda i: (i // per_b, 0, i % per_b)), rows(D_C)]
        + gates + [rows(D_MODEL)] + wspecs + [const((1, D_MODEL)), const((1, D_MODEL))],
        out_specs=[rows(D_MODEL), rows(D_MODEL)],
        out_shape=[jax.ShapeDtypeStruct((n, D_MODEL), F32), jax.ShapeDtypeStruct((n, D_MODEL), BF16)],
        compiler_params=_cparams(("parallel",)), name="merge")(
            ya, yb_t, yc, gates_sig, gates_sig, gates_sig, x, *wargs, lw["ln1_g"], lw["ln1_b"])


def _router_kernel(h_ref, whi_ref, wlo_ref, br_ref, o_ref, id_ref, w_ref):
    logits = _mm(h_ref[...], (whi_ref[...], wlo_ref[...]), True) + br_ref[...]
    lane = lax.broadcasted_iota(jnp.int32, logits.shape, 1)
    lg = jnp.where(lane < N_EXPERTS, logits, -jnp.inf)
    m1 = jnp.max(lg, axis=-1, keepdims=True)
    i1 = jnp.min(jnp.where(lg == m1, lane, LANES), axis=-1, keepdims=True)
    lg2 = jnp.where(lane == i1, -jnp.inf, lg)
    m2 = jnp.max(lg2, axis=-1, keepdims=True)
    i2 = jnp.min(jnp.where(lg2 == m2, lane, LANES), axis=-1, keepdims=True)
    e = jnp.exp(m2 - m1)
    den = 1.0 + e
    w1, w2 = 1.0 / den, e / den
    o_ref[...] = jnp.where(lane == i1, w1, 0.0) + jnp.where(lane == i2, w2, 0.0)
    id_ref[...] = jnp.where(lane == 0, i1, jnp.where(lane == 1, i2, 0))
    w_ref[...] = jnp.where(lane == 0, w1, jnp.where(lane == 1, w2, 0.0))


def _router_call(h, lw, tm):
    n = h.shape[0]
    out = pl.BlockSpec((tm, LANES), lambda i: (i, 0))
    return pl.pallas_call(
        _router_kernel, grid=(n // tm,),
        in_specs=[pl.BlockSpec((tm, D_MODEL), lambda i: (i, 0)), pl.BlockSpec((D_MODEL, LANES), lambda i: (0, 0)),
                  pl.BlockSpec((D_MODEL, LANES), lambda i: (0, 0)), pl.BlockSpec((1, LANES), lambda i: (0, 0))],
        out_specs=[out, out, out],
        out_shape=[jax.ShapeDtypeStruct((n, LANES), F32), jax.ShapeDtypeStruct((n, LANES), jnp.int32),
                   jax.ShapeDtypeStruct((n, LANES), F32)],
        compiler_params=_cparams(("parallel",)), name="router")(h, lw["wr"], lw["wr_lo"], lw["br"])


MOE_ROWS = 256
MOE_TM = 256
TOP_K = 2


def _moe_plan(ids, n):
    r = MOE_ROWS
    e = ids[:, :TOP_K].reshape(-1)
    na = n * TOP_K
    n_tiles = na // r + N_EXPERTS
    order = jnp.argsort(e, stable=True).astype(jnp.int32)
    inv = jnp.argsort(order).astype(jnp.int32)
    onehot = (e[:, None] == jnp.arange(N_EXPERTS, dtype=jnp.int32)[None, :]).astype(jnp.int32)
    counts = jnp.sum(onehot, axis=0)
    padded = ((counts + r - 1) // r) * r
    ends = jnp.cumsum(padded)
    starts = ends - padded
    first = jnp.cumsum(counts) - counts
    shift = starts - first
    pos = inv + jnp.sum(onehot * shift[None, :], axis=1)
    tile_row0 = jnp.arange(n_tiles, dtype=jnp.int32) * r
    tile_expert = jnp.minimum(jnp.sum(ends[None, :] <= tile_row0[:, None], axis=1), N_EXPERTS - 1).astype(jnp.int32)
    last = (first + counts - 1)[tile_expert]
    srt = jnp.minimum(tile_row0[:, None] - shift[tile_expert][:, None] + jnp.arange(r, dtype=jnp.int32)[None, :],
                      last[:, None])
    src_tok = order[jnp.clip(srt, 0, na - 1)] // TOP_K
    n_used = (ends[-1] // r).astype(jnp.int32).reshape(1)
    return src_tok.reshape(n_tiles, 1, r), pos.reshape(n // MOE_TM, 1, MOE_TM * TOP_K), tile_expert, n_used


def _row_copy(src_hbm, dst_vmem, sem, src_row, dst_row):
    return pltpu.make_async_copy(src_hbm.at[pl.ds(src_row, 1), :], dst_vmem.at[pl.ds(dst_row, 1), :], sem)


def _experts_kernel(te_ref, nu_ref, src_ref, nxt_ref, h_hbm, wg_ref, wu_ref, wd_ref, y_ref, x_buf, sem):
    i = pl.program_id(0)
    r = x_buf.shape[1]
    slot = i & 1

    def gather(idx_ref, s):
        for k in range(r):
            _row_copy(h_hbm, x_buf.at[s], sem.at[s], idx_ref[0, k], k).start(priority=k % 2)

    @pl.when(i < nu_ref[0])
    def _():
        @pl.when(i == 0)
        def _():
            gather(src_ref, 0)

        @pl.when(i + 1 < nu_ref[0])
        def _():
            gather(nxt_ref, 1 - slot)

        for k in range(r):
            _row_copy(h_hbm, x_buf.at[slot], sem.at[slot], 0, k).wait()
        xb = x_buf[slot].astype(BF16)
        gate = _dot(xb, wg_ref[...])
        up = _dot(xb, wu_ref[...])
        y_ref[...] = _dot(((gate * _sigmoid(gate)) * up).astype(BF16), wd_ref[...])

    @pl.when(i >= nu_ref[0])
    def _():
        y_ref[...] = jnp.zeros_like(y_ref)


def _experts_call(h, src_tok, tile_expert, n_used, lw):
    n_tiles, _, r = src_tok.shape
    wg_spec = pl.BlockSpec((None, D_MODEL, D_FF_EXPERT), lambda i, te, nu: (te[i], 0, 0))
    wd_spec = pl.BlockSpec((None, D_FF_EXPERT, D_MODEL), lambda i, te, nu: (te[i], 0, 0))
    grid_spec = pltpu.PrefetchScalarGridSpec(
        num_scalar_prefetch=2, grid=(n_tiles,),
        in_specs=[pl.BlockSpec((None, 1, r), lambda i, te, nu: (i, 0, 0), memory_space=pltpu.SMEM),
                  pl.BlockSpec((None, 1, r), lambda i, te, nu: (jnp.minimum(i + 1, n_tiles - 1), 0, 0),
                               memory_space=pltpu.SMEM),
                  pl.BlockSpec(memory_space=pl.ANY), wg_spec, wg_spec, wd_spec],
        out_specs=pl.BlockSpec((r, D_MODEL), lambda i, te, nu: (i, 0)),
        scratch_shapes=[pltpu.VMEM((2, r, D_MODEL), F32), pltpu.SemaphoreType.DMA((2,))])
    return pl.pallas_call(
        _experts_kernel, grid_spec=grid_spec,
        out_shape=jax.ShapeDtypeStruct((n_tiles * r, D_MODEL), F32),
        compiler_params=_cparams(("arbitrary",)), name="moe_experts")(
            tile_expert, n_used, src_tok, src_tok, h, lw["w_gate"], lw["w_up"], lw["w_down"])


def _combine_kernel(pos_ref, hb_ref, h_ref, p_ref, w_ref, y_hbm, wp_ref, wpg_ref, lg_ref, lb_ref,
                    o_ref, ob_ref, y_buf, sem):
    tm = h_ref.shape[0]

    for k in range(tm):
        for j in range(TOP_K):
            _row_copy(y_hbm, y_buf.at[j], sem.at[0], pos_ref[0, TOP_K * k + j], k).start(priority=j % 2)
    hb = hb_ref[...]
    ple = _dot(p_ref[...].astype(BF16), wp_ref[...]) * _sigmoid(_dot(hb, wpg_ref[...]))
    acc = ALPHA * h_ref[...] + ple
    for k in range(tm):
        for j in range(TOP_K):
            _row_copy(y_hbm, y_buf.at[j], sem.at[0], 0, k).wait()
    w = w_ref[...]
    for j in range(TOP_K):
        acc = acc + w[:, j:j + 1] * y_buf[j]
    y = _ln_rows(acc, lg_ref[...], lb_ref[...])
    o_ref[...] = y
    ob_ref[...] = y.astype(BF16)


def _combine_call(hb, h, p, wsel, pos, y_sorted, lw):
    n = h.shape[0]
    tm = MOE_TM

    def rows(w):
        return pl.BlockSpec((tm, w), lambda i: (i, 0))

    def const(shape):
        return pl.BlockSpec(shape, lambda i: (0, 0))

    return pl.pallas_call(
        _combine_kernel, grid=(n // tm,),
        in_specs=[pl.BlockSpec((None, 1, tm * TOP_K), lambda i: (i, 0, 0), memory_space=pltpu.SMEM),
                  rows(D_MODEL), rows(D_MODEL), rows(PLE_DIM), rows(LANES), pl.BlockSpec(memory_space=pl.ANY),
                  const((PLE_DIM, D_MODEL)), const((D_MODEL, D_MODEL)), const((1, D_MODEL)), const((1, D_MODEL))],
        out_specs=[rows(D_MODEL), rows(D_MODEL)],
        out_shape=[jax.ShapeDtypeStruct((n, D_MODEL), F32), jax.ShapeDtypeStruct((n, D_MODEL), BF16)],
        scratch_shapes=[pltpu.VMEM((TOP_K, tm, D_MODEL), F32), pltpu.SemaphoreType.DMA((1,))],
        compiler_params=_cparams(("arbitrary",)), name="moe_combine")(
            pos, hb, h, p, wsel, y_sorted, lw["w_ple"], lw["w_pleg"], lw["ln2_g"], lw["ln2_b"])


def _ffn_kernel(*refs, weighted, precise):
    it = iter(refs)
    hb_ref, h_ref, p_ref = next(it), next(it), next(it)
    comb_ref = next(it) if weighted else None
    wg, wu, wd, wp, wpg = [_take_w(it, precise) for _ in range(5)]
    lg_ref, lb_ref, o_ref, ob_ref, acc_sc = next(it), next(it), next(it), next(it), next(it)
    j = pl.program_id(1)
    hb = h_ref[...] if precise else hb_ref[...]

    @pl.when(j == 0)
    def _():
        ple = _mm(p_ref[...], _ld(wp), precise) * _sigmoid(_mm(hb, _ld(wpg), precise))
        acc_sc[...] = ALPHA * h_ref[...] + ple

    gate = _mm(hb, _ld(wg), precise)
    up = _mm(hb, _ld(wu), precise)
    out = _mm((gate * _sigmoid(gate)) * up, _ld(wd), precise)
    if weighted:
        comb = comb_ref[...]
        lane = lax.broadcasted_iota(jnp.int32, comb.shape, 1)
        out = jnp.sum(jnp.where(lane == j, comb, 0.0), axis=-1, keepdims=True) * out
    acc_sc[...] += out

    @pl.when(j == pl.num_programs(1) - 1)
    def _():
        y = _ln_rows(acc_sc[...], lg_ref[...], lb_ref[...])
        o_ref[...] = y
        ob_ref[...] = y.astype(BF16)


def _ffn_call(hb, h, p, comb, lw, tm, precise):
    n = h.shape[0]
    weighted = comb is not None
    if weighted:
        n_e = N_EXPERTS
        wg_spec = pl.BlockSpec((None, D_MODEL, D_FF_EXPERT), lambda i, j: (j, 0, 0))
        wd_spec = pl.BlockSpec((None, D_FF_EXPERT, D_MODEL), lambda i, j: (j, 0, 0))
    else:
        n_e = D_FF // D_FF_EXPERT
        wg_spec = pl.BlockSpec((D_MODEL, D_FF_EXPERT), lambda i, j: (0, j))
        wd_spec = pl.BlockSpec((D_FF_EXPERT, D_MODEL), lambda i, j: (j, 0))

    def rows(w):
        return pl.BlockSpec((tm, w), lambda i, j: (i, 0))

    def const(shape):
        return pl.BlockSpec(shape, lambda i, j: (0, 0))

    wspecs, wargs = [], []
    for name, spec in (("w_gate", wg_spec), ("w_up", wg_spec), ("w_down", wd_spec),
                       ("w_ple", const((PLE_DIM, D_MODEL))), ("w_pleg", const((D_MODEL, D_MODEL)))):
        ws = _wargs(lw, name, precise)
        wargs += ws
        wspecs += [spec] * len(ws)
    return pl.pallas_call(
        functools.partial(_ffn_kernel, weighted=weighted, precise=precise), grid=(n // tm, n_e),
        in_specs=[rows(D_MODEL), rows(D_MODEL), rows(PLE_DIM)] + [rows(LANES)] * weighted + wspecs
        + [const((1, D_MODEL)), const((1, D_MODEL))],
        out_specs=[rows(D_MODEL), rows(D_MODEL)],
        out_shape=[jax.ShapeDtypeStruct((n, D_MODEL), F32), jax.ShapeDtypeStruct((n, D_MODEL), BF16)],
        scratch_shapes=[pltpu.VMEM((tm, D_MODEL), F32)],
        compiler_params=_cparams(("parallel", "arbitrary")), name="ffn")(
            hb, h, p, *([comb] * weighted), *wargs, lw["ln2_g"], lw["ln2_b"])


def _layer_weights(i, W):
    w_in = W["w_in"][i]
    sizes = [D_AV, D_AV, D_B, D_B, D_B, FOX_HEADS, D_C, CONV_DIM, SSM_HEADS, N_BRANCH * D_MODEL]
    o = [0] + [int(s) for s in np.cumsum(sizes)]
    seg = [w_in[:, o[k]:o[k + 1]] for k in range(len(sizes))]
    w_u, w_v, w_q, w_k, w_va, w_f, w_z, w_xbc, w_dt, w_gate = seg
    pad = jnp.zeros((D_MODEL, LANES - SSM_HEADS - FOX_HEADS), F32)

    def lane_row(v, lane0=0):
        return jnp.pad(v.astype(F32), (lane0, LANES - lane0 - v.shape[0])).reshape(1, LANES)

    lw = dict(
        fox_b=lane_row(W["fox_b_forget"][i]),
        gmlp_g=W["gmlp_ln_g"][i], gmlp_b=W["gmlp_ln_b"][i], gmlp_w=W["gmlp_w_spatial"][i], gmlp_bs=W["gmlp_b_spatial"][i],
        conv_w=W["ssm_conv_w"][i], conv_b=W["ssm_conv_b"][i].reshape(1, CONV_DIM),
        dtb_row=lane_row(W["ssm_dt_bias"][i], DT_LANE0), dtb_col=W["ssm_dt_bias"][i].reshape(SSM_HEADS, 1),
        alog_row=lane_row(W["ssm_a_log"][i], DT_LANE0), alog_col=W["ssm_a_log"][i].reshape(SSM_HEADS, 1),
        alog_e=jnp.repeat(W["ssm_a_log"][i], SSM_HD).reshape(1, D_C),
        dvec=jnp.repeat(W["ssm_d"][i], SSM_HD).reshape(1, D_C), norm_g=W["ssm_norm_g"][i].reshape(1, D_C),
        ln1_g=W["ln1_g"][i].reshape(1, D_MODEL), ln1_b=W["ln1_b"][i].reshape(1, D_MODEL),
        ln2_g=W["ln2_g"][i].reshape(1, D_MODEL), ln2_b=W["ln2_b"][i].reshape(1, D_MODEL),
    )
    j = i // 2
    mats = dict(
        w_main=jnp.concatenate([w_xbc, w_z, w_u, w_v], axis=1), w_bgate=w_gate, w_k=w_k, w_va=w_va,
        w_small=jnp.concatenate([w_f, w_dt, pad], axis=1),
        w_qt=jnp.transpose(w_q), w_vt=jnp.transpose(w_va),
        w_a=W["w_branch_a"][i], w_b=W["w_branch_b"][i], w_c=W["w_branch_c"][i], w_o=W["w_out"][i],
        w_ple=W["ple_w_proj"][i], w_pleg=W["ple_w_gate"][i],
    )
    if i % 2 == 0:
        mats.update(w_gate=W["ffn_w_gate"][j], w_up=W["ffn_w_up"][j], w_down=W["ffn_w_down"][j])
    else:
        mats.update(w_gate=W["moe_w_gate"][j], w_up=W["moe_w_up"][j], w_down=W["moe_w_down"][j],
                    wr=jnp.pad(W["moe_w_router"][j], ((0, 0), (0, LANES - N_EXPERTS))))
        lw["br"] = lane_row(W["moe_b_router"][j])
    for name, w in mats.items():
        c = w * np.float32(2.0**16 + 1.0)
        hi_f = c - (c - w)
        lw[name] = hi_f.astype(BF16)
        lw[name + "_lo"] = (w - hi_f).astype(BF16)
    return lw


def _trunk_layer(x, xb, p, i, lw, bsz, t, fox_cache, ssm0, conv0, tm, precise):
    n = bsz * t
    act = _act_dtype(precise)
    xin = x if precise else xb
    tmm = min(tm, t)
    tbig = min(n, PROJ_TM)
    proj = _matmul_call(xin, lw, "w_main", tbig, MAIN_TN, precise, "in_proj")
    gates_sig = _matmul_call(xin, lw, "w_bgate", tbig, GATE_TN, precise, "in_proj_gates", gate=True)
    k_new, v_new, small, logf = _kvs_call(xin, lw, tm, precise)

    w_eff, bias_eff = _gmlp_weights(lw["gmlp_w"], lw["gmlp_bs"], t)
    y_a, v_n = _gmlp_call(proj, lw["gmlp_g"], lw["gmlp_b"], w_eff, bias_eff, min(n, TOKEN_TM), precise)

    k_new = k_new.reshape(bsz, t, D_B)
    v_new = v_new.reshape(bsz, t, FOX_HEADS, FOX_HD)
    logf = logf.reshape(bsz, t, FOX_HEADS)
    q_t, v_t = _qvt_call(xin, lw, tmm, bsz, t, precise)
    if fox_cache is None:
        k_all, logf_all, q_off = k_new, logf, 0
    else:
        k_c, v_c, logf_c = fox_cache
        q_off = k_c.shape[1]
        k_all = jnp.concatenate([k_c.reshape(bsz, q_off, D_B), k_new], axis=1)
        v_t = jnp.concatenate([jnp.transpose(v_c.reshape(bsz, q_off, D_B), (0, 2, 1)).astype(act), v_t], axis=2)
        logf_all = jnp.concatenate([logf_c, logf], axis=1)
    t_k = k_all.shape[1]
    k_a = _keys_call(k_all, logf_all, FOX_TILE if t_k % FOX_TILE == 0 else t_k, precise)
    tq, tk = (FOX_TILE, FOX_TILE) if t % FOX_TILE == 0 else (t, t_k)
    y_bt = _fox_call(q_t, k_a, v_t, tq, tk, q_off, FOX_NH, precise)

    dt_t = jnp.transpose(small[:, DT_LANE0:DT_LANE0 + SSM_HEADS].reshape(bsz, t, SSM_HEADS), (0, 2, 1))
    y_c, h_fin, conv_new = _ssd_call(proj, small, dt_t, ssm0, conv0, lw, bsz, t, precise)

    x1, x1b = _merge_call(y_a, y_bt, y_c, gates_sig, x, lw, tmm, bsz, t, precise)
    if i % 2 == 0:
        x2, x2b = _ffn_call(x1b, x1, p, None, lw, tm, precise)
    else:
        comb, ids, wsel = _router_call(x1, lw, tm)
        if precise or n % MOE_TM or n < N_EXPERTS * MOE_ROWS:
            x2, x2b = _ffn_call(x1b, x1, p, comb, lw, tm, precise)
        else:
            src_tok, pos, tile_expert, n_used = _moe_plan(ids, n)
            y_sorted = _experts_call(x1, src_tok, tile_expert, n_used, lw)
            x2, x2b = _combine_call(x1b, x1, p, wsel, pos, y_sorted, lw)
    return x2, x2b, (k_new.reshape(bsz, t, FOX_HEADS, FOX_HD), v_new, logf, h_fin, conv_new,
                     v_n.reshape(bsz, t, D_AV))


def kernel(x_prompt, x_sample, p_prompt, p_sample, cache_fox_k, cache_fox_v, cache_fox_logf, state_ssm, state_conv, ln0_g, ln0_b, w_in, fox_b_forget, gmlp_ln_g, gmlp_ln_b, gmlp_w_spatial, gmlp_b_spatial, ssm_conv_w, ssm_conv_b, ssm_dt_bias, ssm_a_log, ssm_d, ssm_norm_g, w_branch_a, w_branch_b, w_branch_c, w_out, ln1_g, ln1_b, ln2_g, ln2_b, ffn_w_gate, ffn_w_up, ffn_w_down, moe_w_router, moe_b_router, moe_w_gate, moe_w_up, moe_w_down, ple_w_proj, ple_w_gate):
    W = dict(w_in=w_in, fox_b_forget=fox_b_forget, gmlp_ln_g=gmlp_ln_g, gmlp_ln_b=gmlp_ln_b,
             gmlp_w_spatial=gmlp_w_spatial, gmlp_b_spatial=gmlp_b_spatial, ssm_conv_w=ssm_conv_w,
             ssm_conv_b=ssm_conv_b, ssm_dt_bias=ssm_dt_bias, ssm_a_log=ssm_a_log, ssm_d=ssm_d,
             ssm_norm_g=ssm_norm_g, w_branch_a=w_branch_a, w_branch_b=w_branch_b,
             w_branch_c=w_branch_c, w_out=w_out, ln1_g=ln1_g, ln1_b=ln1_b, ln2_g=ln2_g, ln2_b=ln2_b,
             ffn_w_gate=ffn_w_gate, ffn_w_up=ffn_w_up, ffn_w_down=ffn_w_down,
             moe_w_router=moe_w_router, moe_b_router=moe_b_router, moe_w_gate=moe_w_gate, moe_w_up=moe_w_up,
             moe_w_down=moe_w_down, ple_w_proj=ple_w_proj, ple_w_gate=ple_w_gate)
    bp, tp, _ = x_prompt.shape
    bs, ts, _ = x_sample.shape
    tm_p = TOKEN_TM if (bp * tp) % TOKEN_TM == 0 else bp * tp
    tm_s = bs * ts
    xp, xpb = _layer_norm_call(x_prompt.reshape(bp * tp, D_MODEL), ln0_g, ln0_b, tm_p)
    xs, xsb = _layer_norm_call(x_sample.reshape(bs * ts, D_MODEL), ln0_g, ln0_b, tm_s)
    outs_p, outs_s = [], []
    for i in range(DEPTH):
        lw = _layer_weights(i, W)
        ssm0 = jnp.zeros((bp, SSM_HEADS, SSM_HD, SSM_N), F32)
        conv0 = jnp.zeros((bp, CONV_W - 1, CONV_DIM), F32)
        xp, xpb, st_p = _trunk_layer(xp, xpb, p_prompt[i].reshape(bp * tp, PLE_DIM), i, lw, bp, tp,
                                     None, ssm0, conv0, tm_p, False)
        xs, xsb, st_s = _trunk_layer(xs, xsb, p_sample[i].reshape(bs * ts, PLE_DIM), i, lw, bs, ts,
                                     (cache_fox_k[i], cache_fox_v[i], cache_fox_logf[i]),
                                     state_ssm[i], state_conv[i], tm_s, True)
        outs_p.append(st_p)
        outs_s.append(st_s)

    def stack(outs, k):
        return jnp.stack([o[k] for o in outs])

    return (xp.reshape(bp, tp, D_MODEL), xs.reshape(bs, ts, D_MODEL),
            stack(outs_p, 0), stack(outs_p, 1), stack(outs_p, 2), stack(outs_p, 3), stack(outs_p, 4),
            stack(outs_s, 0), stack(outs_s, 1), stack(outs_s, 2), stack(outs_s, 3), stack(outs_s, 4), stack(outs_s, 5))
```

```python
import functools
import math

import numpy as np
import jax
import jax.numpy as jnp
from jax import lax
from jax.experimental import pallas as pl
from jax.experimental.pallas import tpu as pltpu

F32 = jnp.float32
BF16 = jnp.bfloat16

D_MODEL = 1024
DEPTH = 2
CHUNK = 64
PLE_DIM = 256
GMLP_CHUNK = 128
GMLP_GROUPS = 4
D_AV = D_MODEL // 2
GMLP_GROUP_W = D_AV // GMLP_GROUPS
FOX_HD = 64
D_B = D_MODEL // 2
FOX_HEADS = D_B // FOX_HD
D_C = D_MODEL
SSM_HD = 64
SSM_HEADS = D_C // SSM_HD
SSM_N = 128
SSM_GROUPS = 2
CONV_W = 4
CONV_DIM = D_C + 2 * SSM_GROUPS * SSM_N
N_BRANCH = 3
D_FF = 11 * D_MODEL // 4
N_EXPERTS = 8
D_FF_EXPERT = D_FF // 2
ALPHA = (2.0 * DEPTH) ** 0.25
LN_EPS = 1e-5
RMS_EPS = 1e-5
NEG_INF = -1e30
LOG2E = math.log2(math.e)

LANES = 128
VMEM_LIMIT = 56 * 2**20
N_MAIN = CONV_DIM + D_C + 2 * D_AV
OFF_XBC, OFF_Z, OFF_U, OFF_V = 0, 1536, 2560, 3072
MAIN_TN = 512
GATE_TN = 1024
DT_LANE0 = FOX_HEADS
SSD_L = 128
CONV_PAD = 8
FOX_NBIAS = 3
FOX_DK = 128
FOX_DV = 80
FOX_QSUB = 256
FOX_TILE = 512
FOX_NH = 4
TOKEN_TM = 512
PROJ_TM = 2048


def _cparams(sem):
    return pltpu.CompilerParams(dimension_semantics=sem, vmem_limit_bytes=VMEM_LIMIT)


def _sigmoid(x):
    return 1.0 / (1.0 + jnp.exp(-x))


def _softplus(x):
    return jnp.maximum(x, 0.0) + jnp.log1p(jnp.exp(-jnp.abs(x)))


def _gelu(x):
    c = np.float32(np.sqrt(2.0 / np.pi))
    return x * (0.5 * (1.0 + jnp.tanh(c * (x + 0.044715 * (x * x * x)))))


def _ln_rows(x, g, b):
    mu = jnp.mean(x, axis=-1, keepdims=True)
    xc = x - mu
    var = jnp.mean(xc * xc, axis=-1, keepdims=True)
    return xc * lax.rsqrt(var + LN_EPS) * g + b


def _bf16_part(x):
    u = lax.bitcast_convert_type(x, jnp.uint32) & jnp.uint32(0xFFFF0000)
    return lax.bitcast_convert_type(u, F32)


def _split2(x):
    hi = _bf16_part(x)
    return hi.astype(BF16), (x - hi).astype(BF16)


def _split3(x):
    hi = _bf16_part(x)
    r1 = x - hi
    mid = _bf16_part(r1)
    return hi.astype(BF16), mid.astype(BF16), (r1 - mid).astype(BF16)


def _dot(a, b):
    return jnp.dot(a, b, preferred_element_type=F32)


def _dot_nt(a, b):
    return lax.dot_general(a, b, (((1,), (1,)), ((), ())), preferred_element_type=F32)


def _dot_tn(a, b):
    return lax.dot_general(a, b, (((0,), (0,)), ((), ())), preferred_element_type=F32)


_DOTS = {"nn": _dot, "nt": _dot_nt, "tn": _dot_tn}


def _mm(a, b, precise, dims="nn"):
    dot = _DOTS[dims]
    if not precise:
        a = a[0] if isinstance(a, tuple) else a.astype(BF16)
        b = b[0] if isinstance(b, tuple) else b.astype(BF16)
        return dot(a, b)
    ah, al = a if isinstance(a, tuple) else _split2(a)
    bh, bl = b if isinstance(b, tuple) else _split2(b)
    return (dot(ah, bh) + dot(al, bh)) + dot(ah, bl)


def _take_w(it, precise):
    hi = next(it)
    return hi, (next(it) if precise else None)


def _ld(w, idx=...):
    return w[0][idx], (None if w[1] is None else w[1][idx])


def _wargs(lw, name, precise):
    return [lw[name], lw[name + "_lo"]] if precise else [lw[name]]


def _act_dtype(precise):
    return F32 if precise else BF16


def _ln_kernel(x_ref, g_ref, b_ref, o_ref, ob_ref):
    y = _ln_rows(x_ref[...], g_ref[...], b_ref[...])
    o_ref[...] = y
    ob_ref[...] = y.astype(BF16)


def _layer_norm_call(x, g, b, tm):
    n, d = x.shape
    row = pl.BlockSpec((tm, d), lambda i: (i, 0))
    vec = pl.BlockSpec((1, d), lambda i: (0, 0))
    return pl.pallas_call(
        _ln_kernel, grid=(n // tm,), in_specs=[row, vec, vec], out_specs=[row, row],
        out_shape=[jax.ShapeDtypeStruct((n, d), F32), jax.ShapeDtypeStruct((n, d), BF16)],
        compiler_params=_cparams(("parallel",)), name="ln_in")(x, g.reshape(1, d), b.reshape(1, d))


def _mm_kernel(*refs, precise, gate):
    it = iter(refs)
    x_ref = next(it)
    w = _take_w(it, precise)
    o_ref = next(it)
    y = _mm(x_ref[...], _ld(w), precise)
    o_ref[...] = (_sigmoid(y) if gate else y).astype(o_ref.dtype)


def _matmul_call(x, lw, wname, tm, tn, precise, name, gate=False):
    m, k = x.shape
    ws = _wargs(lw, wname, precise)
    n = ws[0].shape[1]
    wspec = pl.BlockSpec((k, tn), lambda i, j: (0, j))
    return pl.pallas_call(
        functools.partial(_mm_kernel, precise=precise, gate=gate), grid=(m // tm, n // tn),
        in_specs=[pl.BlockSpec((tm, k), lambda i, j: (i, 0))] + [wspec] * len(ws),
        out_specs=pl.BlockSpec((tm, tn), lambda i, j: (i, j)),
        out_shape=jax.ShapeDtypeStruct((m, n), _act_dtype(precise) if gate else F32),
        compiler_params=_cparams(("parallel", "parallel")), name=name)(x, *ws)


def _gmlp_kernel(*refs, rows, precise):
    it = iter(refs)
    u_ref, v_ref, g_ref, b_ref = next(it), next(it), next(it), next(it)
    w = _take_w(it, precise)
    bias_ref, ya_ref, vn_ref = next(it), next(it), next(it)
    u = _gelu(u_ref[...])
    vn = _ln_rows(_gelu(v_ref[...]), g_ref[...], b_ref[...])
    vn_ref[...] = vn
    vb = vn if precise else vn.astype(BF16)
    for c in range(rows // GMLP_CHUNK):
        r0 = c * GMLP_CHUNK
        for g in range(GMLP_GROUPS):
            c0 = g * GMLP_GROUP_W
            mixed = _mm(_ld(w, g), vb[r0:r0 + GMLP_CHUNK, c0:c0 + GMLP_GROUP_W], precise)
            mixed = mixed + bias_ref[:, c0:c0 + GMLP_GROUP_W]
            ya_ref[r0:r0 + GMLP_CHUNK, c0:c0 + GMLP_GROUP_W] = (
                u[r0:r0 + GMLP_CHUNK, c0:c0 + GMLP_GROUP_W] * mixed).astype(ya_ref.dtype)


def _gmlp_call(proj, ln_g, ln_b, w_eff, bias_eff, rows, precise):
    n = proj.shape[0]
    ublk, vblk = OFF_U // D_AV, OFF_V // D_AV
    vec = pl.BlockSpec((1, D_AV), lambda i: (0, 0))
    ws = list(_split2(w_eff)) if precise else [w_eff.astype(BF16)]
    wspec = pl.BlockSpec((GMLP_GROUPS, GMLP_CHUNK, GMLP_CHUNK), lambda i: (0, 0, 0))
    return pl.pallas_call(
        functools.partial(_gmlp_kernel, rows=rows, precise=precise), grid=(n // rows,),
        in_specs=[pl.BlockSpec((rows, D_AV), lambda i: (i, ublk)), pl.BlockSpec((rows, D_AV), lambda i: (i, vblk)),
                  vec, vec] + [wspec] * len(ws) + [pl.BlockSpec((GMLP_CHUNK, D_AV), lambda i: (0, 0))],
        out_specs=[pl.BlockSpec((rows, D_AV), lambda i: (i, 0)), pl.BlockSpec((rows, D_AV), lambda i: (i, 0))],
        out_shape=[jax.ShapeDtypeStruct((n, D_AV), _act_dtype(precise)), jax.ShapeDtypeStruct((n, D_AV), F32)],
        compiler_params=_cparams(("parallel",)), name="gmlp")(
            proj, proj, ln_g.reshape(1, D_AV), ln_b.reshape(1, D_AV), *ws, bias_eff)


def _gmlp_weights(w_s, b_s, t):
    l = min(GMLP_CHUNK, t)
    pos = np.arange(l)
    mask = (pos[None, :] // CHUNK) <= (pos[:, None] // CHUNK)
    w = jnp.where(mask[None], w_s[:, :l, :l], 0.0)
    bias = jnp.transpose(b_s[:, :l])
    reps = GMLP_CHUNK // l
    if reps > 1:
        eye = jnp.eye(reps, dtype=w.dtype)
        w = jnp.einsum("ab,gij->gaibj", eye, w).reshape(GMLP_GROUPS, GMLP_CHUNK, GMLP_CHUNK)
        bias = jnp.tile(bias, (reps, 1))
    return w, jnp.repeat(bias, GMLP_GROUP_W, axis=1)


def _kvs_kernel(*refs, precise):
    it = iter(refs)
    x_ref = next(it)
    wk, wv, ws = [_take_w(it, precise) for _ in range(3)]
    fb_ref, k_ref, v_ref, s_ref, lf_ref = [next(it) for _ in range(5)]
    x = x_ref[...]
    k_ref[...] = _mm(x, _ld(wk), precise)
    v_ref[...] = _mm(x, _ld(wv), precise)
    small = _mm(x, _ld(ws), precise)
    s_ref[...] = small
    lf_ref[...] = (-_softplus(-(small + fb_ref[...])))[:, :FOX_HEADS]


def _kvs_call(x, lw, tm, precise):
    n = x.shape[0]
    wargs, wspecs = [], []
    for name, width in (("w_k", D_B), ("w_va", D_B), ("w_small", LANES)):
        ws = _wargs(lw, name, precise)
        wargs += ws
        wspecs += [pl.BlockSpec((D_MODEL, width), lambda i: (0, 0))] * len(ws)

    def rows(w):
        return pl.BlockSpec((tm, w), lambda i: (i, 0))

    return pl.pallas_call(
        functools.partial(_kvs_kernel, precise=precise), grid=(n // tm,),
        in_specs=[rows(D_MODEL)] + wspecs + [pl.BlockSpec((1, LANES), lambda i: (0, 0))],
        out_specs=[rows(D_B), rows(D_B), rows(LANES), rows(FOX_HEADS)],
        out_shape=[jax.ShapeDtypeStruct((n, D_B), F32), jax.ShapeDtypeStruct((n, D_B), F32),
                   jax.ShapeDtypeStruct((n, LANES), F32), jax.ShapeDtypeStruct((n, FOX_HEADS), F32)],
        compiler_params=_cparams(("parallel",)), name="fox_kv")(x, *wargs, lw["fox_b"])


def _qvt_kernel(*refs, precise):
    it = iter(refs)
    x_ref = next(it)
    wq, wv = _take_w(it, precise), _take_w(it, precise)
    qt_ref, vt_ref = next(it), next(it)
    x = x_ref[...]
    qt_ref[...] = (_mm(_ld(wq), x, precise, "nt") * (FOX_HD ** -0.5 * LOG2E)).astype(qt_ref.dtype)
    vt_ref[...] = _mm(_ld(wv), x, precise, "nt").astype(vt_ref.dtype)


def _qvt_call(x, lw, tm, bsz, t, precise):
    n = x.shape[0]
    per_b = t // tm
    act = _act_dtype(precise)
    ws = _wargs(lw, "w_qt", precise) + _wargs(lw, "w_vt", precise)
    out = pl.BlockSpec((None, D_B, tm), lambda i: (i // per_b, 0, i % per_b))
    return pl.pallas_call(
        functools.partial(_qvt_kernel, precise=precise), grid=(n // tm,),
        in_specs=[pl.BlockSpec((tm, D_MODEL), lambda i: (i, 0))]
        + [pl.BlockSpec((D_B, D_MODEL), lambda i: (0, 0))] * len(ws),
        out_specs=[out, out],
        out_shape=[jax.ShapeDtypeStruct((bsz, D_B, t), act), jax.ShapeDtypeStruct((bsz, D_B, t), act)],
        compiler_params=_cparams(("parallel",)), name="fox_qv_t")(x, *ws)


def _place(x, sel, precise):
    if not precise:
        return _dot(x.astype(BF16), sel)
    hi, mid, lo = _split3(x)
    return (_dot(hi, sel) + _dot(mid, sel)) + _dot(lo, sel)


def _keys_kernel(k_ref, lf_ref, tril_ref, selk_ref, selg_ref, o_ref, carry_ref, *, precise):
    @pl.when(pl.program_id(1) == 0)
    def _():
        carry_ref[...] = jnp.zeros_like(carry_ref)

    tril = tril_ref[...]
    hi, mid, lo = _split3(lf_ref[...])
    f = (_dot(tril, hi) + _dot(tril, mid)) + _dot(tril, lo) + carry_ref[0:1, :]
    tc = f.shape[0]
    carry_ref[...] = jnp.broadcast_to(f[tc - 1:tc, :], carry_ref.shape)
    g = f * (-LOG2E)
    g1 = _bf16_part(g)
    r = g - g1
    g2 = _bf16_part(r)
    gcat = jnp.concatenate([g1, g2, r - g2], axis=1)
    ka = _place(k_ref[...], selk_ref[...], precise) + _place(gcat, selg_ref[...], precise)
    o_ref[...] = ka.astype(o_ref.dtype)


def _keys_call(k_all, logf_all, tc, precise):
    bsz, t_k, _ = k_all.shape
    sel_k = np.zeros((D_B, FOX_HEADS * FOX_DK), np.float32)
    sel_g = np.zeros((FOX_NBIAS * FOX_HEADS, FOX_HEADS * FOX_DK), np.float32)
    for h in range(FOX_HEADS):
        sel_k[h * FOX_HD + np.arange(FOX_HD), h * FOX_DK + np.arange(FOX_HD)] = 1.0
        for j in range(FOX_NBIAS):
            sel_g[j * FOX_HEADS + h, h * FOX_DK + FOX_HD + j] = 1.0
    tril = jnp.tril(jnp.ones((tc, tc), F32)).astype(BF16)
    return pl.pallas_call(
        functools.partial(_keys_kernel, precise=precise), grid=(bsz, t_k // tc),
        in_specs=[pl.BlockSpec((None, tc, D_B), lambda b, j: (b, j, 0)),
                  pl.BlockSpec((None, tc, FOX_HEADS), lambda b, j: (b, j, 0)),
                  pl.BlockSpec((tc, tc), lambda b, j: (0, 0)),
                  pl.BlockSpec(sel_k.shape, lambda b, j: (0, 0)), pl.BlockSpec(sel_g.shape, lambda b, j: (0, 0))],
        out_specs=pl.BlockSpec((None, tc, FOX_HEADS * FOX_DK), lambda b, j: (b, j, 0)),
        out_shape=jax.ShapeDtypeStruct((bsz, t_k, FOX_HEADS * FOX_DK), _act_dtype(precise)),
        scratch_shapes=[pltpu.VMEM((8, FOX_HEADS), F32)],
        compiler_params=_cparams(("parallel", "arbitrary")), name="fox_keys")(
            k_all, logf_all, tril, jnp.asarray(sel_k, BF16), jnp.asarray(sel_g, BF16))


def _fox_kernel(qt_ref, ka_ref, vt_ref, o_ref, qa_sc, m_sc, acc_sc, s_sc, mb_sc, p_sc, al_sc,
                *, tq, tk, qs, q_off, precise):
    i = pl.program_id(2)
    act = qa_sc.dtype
    nh = qa_sc.shape[0]
    rows = lax.broadcasted_iota(jnp.int32, (FOX_DK - FOX_HD, tq), 0)
    ones_rows = jnp.where(rows < FOX_NBIAS, 1.0, 0.0).astype(act)
    for h in range(nh):
        qa_sc[h] = jnp.concatenate([qt_ref[h * FOX_HD:(h + 1) * FOX_HD, :], ones_rows], axis=0)
    vrows = lax.broadcasted_iota(jnp.int32, (FOX_DV - FOX_HD, tk), 0)
    v_extra = jnp.where(vrows < 1, 1.0, 0.0).astype(act)
    m_sc[...] = jnp.full_like(m_sc, -jnp.inf)
    acc_sc[...] = jnp.zeros_like(acc_sc)
    first_q = q_off + i * tq
    subs = [(h, c, slice(c * qs, (c + 1) * qs)) for h in range(nh) for c in range(tq // qs)]

    def scores(k0):
        for h, _, sl in subs:
            ka = ka_ref[pl.ds(k0, tk), h * FOX_DK:(h + 1) * FOX_DK]
            s = _mm(ka, qa_sc[h, :, sl], precise)
            s_sc[h, :, sl] = s
            mb_sc[h, :, sl] = jnp.max(s, axis=0, keepdims=True)

    def consume(k0, masked):
        for h, c, sl in subs:
            vt = jnp.concatenate([vt_ref[h * FOX_HD:(h + 1) * FOX_HD, pl.ds(k0, tk)], v_extra], axis=0)
            s = s_sc[h, :, sl]
            if masked:
                kpos = k0 + lax.broadcasted_iota(jnp.int32, (tk, qs), 0)
                qpos = first_q + c * qs + lax.broadcasted_iota(jnp.int32, (tk, qs), 1)
                s = jnp.where(kpos <= qpos, s, NEG_INF)
                mb = jnp.max(s, axis=0, keepdims=True)
            else:
                mb = mb_sc[h, :, sl]
            m_prev = m_sc[h, :, sl]
            m_new = jnp.maximum(m_prev, mb)
            alpha = jnp.exp2(m_prev - m_new)
            p = jnp.exp2(s - m_new)
            acc_sc[h, :, sl] = alpha * acc_sc[h, :, sl] + _mm(vt, p, precise)
            m_sc[h, :, sl] = m_new

    def softmax_only(_k0):
        for h, _, sl in subs:
            m_prev = m_sc[h, :, sl]
            m_new = jnp.maximum(m_prev, mb_sc[h, :, sl])
            al_sc[h, :, sl] = jnp.exp2(m_prev - m_new)
            p_sc[h, :, sl] = jnp.exp2(s_sc[h, :, sl] - m_new).astype(p_sc.dtype)
            m_sc[h, :, sl] = m_new

    def values(k0):
        for h, _, sl in subs:
            vt = jnp.concatenate([vt_ref[h * FOX_HD:(h + 1) * FOX_HD, pl.ds(k0, tk)], v_extra], axis=0)
            acc_sc[h, :, sl] = al_sc[h, :, sl] * acc_sc[h, :, sl] + _mm(vt, p_sc[h, :, sl], precise)

    scores(0)
    if ka_ref.shape[0] == tk:
        consume(0, True)
    else:
        assert tq == tk and q_off == 0
        @pl.when(i > 0)
        def _():
            softmax_only(0)
            scores(tk)

        def body(j, c):
            k0 = pl.multiple_of(j * tk, tk)
            values(pl.multiple_of(k0 - tk, tk))
            softmax_only(k0)
            scores(pl.multiple_of(k0 + tk, tk))
            return c

        lax.fori_loop(1, i, body, 0)

        @pl.when(i > 0)
        def _():
            values(pl.multiple_of(i * tk - tk, tk))

        consume(pl.multiple_of(i * tk, tk), True)
    for h in range(nh):
        acc = acc_sc[h]
        o_ref[h * FOX_HD:(h + 1) * FOX_HD, :] = (acc[:FOX_HD] / acc[FOX_HD:FOX_HD + 1]).astype(o_ref.dtype)


def _fox_call(q_t, k_a, v_t, tq, tk, q_off, nh, precise):
    bsz, _, t_q = q_t.shape
    t_k = k_a.shape[1]
    qs = min(FOX_QSUB, tq)
    act = _act_dtype(precise)
    qo = pl.BlockSpec((None, nh * FOX_HD, tq), lambda b, hh, i: (b, hh, i))
    return pl.pallas_call(
        functools.partial(_fox_kernel, tq=tq, tk=tk, qs=qs, q_off=q_off, precise=precise),
        grid=(bsz, FOX_HEADS // nh, t_q // tq),
        in_specs=[qo, pl.BlockSpec((None, t_k, nh * FOX_DK), lambda b, hh, i: (b, 0, hh)),
                  pl.BlockSpec((None, nh * FOX_HD, t_k), lambda b, hh, i: (b, hh, 0))],
        out_specs=qo, out_shape=jax.ShapeDtypeStruct((bsz, D_B, t_q), act),
        scratch_shapes=[pltpu.VMEM((nh, FOX_DK, tq), act), pltpu.VMEM((nh, 1, tq), F32),
                        pltpu.VMEM((nh, FOX_DV, tq), F32), pltpu.VMEM((nh, tk, tq), F32),
                        pltpu.VMEM((nh, 1, tq), F32), pltpu.VMEM((nh, tk, tq), act), pltpu.VMEM((nh, 1, tq), F32)],
        compiler_params=_cparams(("parallel", "parallel", "arbitrary")), name="fox_attn")(q_t, k_a, v_t)


def _ssd_kernel(xbc_ref, z0_ref, z1_ref, dtp_ref, dtt_ref, h0_ref, c0_ref, cw_ref, cb_ref, dtb_row_ref, dtb_col_ref,
                alog_row_ref, alog_col_ref, aloge_ref, dvec_ref, ng_ref, tril_ref, triu_ref, expand_ref,
                yc_ref, hfin_ref, cnew_ref, h_sc, pad_sc, y_sc, *, L, precise):
    c = pl.program_id(1)

    @pl.when(c == 0)
    def _():
        h_sc[...] = h0_ref[...]
        pad_sc[CONV_PAD - (CONV_W - 1):CONV_PAD, :] = c0_ref[...]

    xin = xbc_ref[...]
    pad_sc[CONV_PAD:CONV_PAD + L, :] = xin
    y = cb_ref[...]
    for j in range(CONV_W - 1):
        r0 = CONV_PAD - (CONV_W - 1) + j
        y = y + pad_sc[r0:r0 + L, :] * cw_ref[j:j + 1, :]
    y = y + xin * cw_ref[CONV_W - 1:CONV_W, :]
    tail = pad_sc[CONV_PAD + L - (CONV_W - 1):CONV_PAD + L, :]
    pad_sc[CONV_PAD - (CONV_W - 1):CONV_PAD, :] = tail
    cnew_ref[...] = tail
    act = y * _sigmoid(y)

    dt_c = _softplus(dtp_ref[...] + dtb_row_ref[...])
    a_c = dt_c * (-jnp.exp(alog_row_ref[...]))
    tril = tril_ref[...]
    hi, mid, lo = _split3(a_c)
    acum_c = (_dot(tril, hi) + _dot(tril, mid)) + _dot(tril, lo)
    dt_r = _softplus(dtt_ref[...] + dtb_col_ref[...])
    a_r = dt_r * (-jnp.exp(alog_col_ref[...]))
    triu = triu_ref[...]
    hi, mid, lo = _split3(a_r)
    acum_r = (_dot(hi, triu) + _dot(mid, triu)) + _dot(lo, triu)

    dt_e = _place(dt_c, expand_ref[...], True)
    hi, mid, lo = _split3(dt_e * (-jnp.exp(aloge_ref[...])))
    acum_e = (_dot(tril, hi) + _dot(tril, mid)) + _dot(tril, lo)
    xs = act[:, :D_C]
    e_in = jnp.exp(acum_e)
    xw = xs * (jnp.exp(acum_e[L - 1:L, :] - acum_e) * dt_e)
    dx = dvec_ref[...] * xs

    row = lax.broadcasted_iota(jnp.int32, (L, L), 0)
    col = lax.broadcasted_iota(jnp.int32, (L, L), 1)
    causal = col <= row
    first_of_pair = lax.broadcasted_iota(jnp.int32, (L, 2 * SSM_HD), 1) < SSM_HD
    rep = SSM_HEADS // SSM_GROUPS
    gw = rep * SSM_HD
    for g in range(SSM_GROUPS):
        b0 = D_C + g * SSM_N
        c0 = D_C + SSM_GROUPS * SSM_N + g * SSM_N
        b_g = act[:, b0:b0 + SSM_N]
        c_g = act[:, c0:c0 + SSM_N]
        b_gs = _split2(b_g) if precise else (b_g.astype(BF16), None)
        cb = _mm(c_g, b_gs, precise, "nt")
        h_prev = h_sc[g * rep:(g + 1) * rep].reshape(gw, SSM_N)
        y_off = _mm(c_g, h_prev, precise, "nt") * e_in[:, g * gw:(g + 1) * gw]
        s_g = _mm(xw[:, g * gw:(g + 1) * gw], b_gs, precise, "tn")
        for hh in range(rep):
            h = g * rep + hh
            alast = acum_c[L - 1:L, DT_LANE0 + h:DT_LANE0 + h + 1]
            h_sc[h] = jnp.exp(alast) * h_sc[h] + s_g[hh * SSM_HD:(hh + 1) * SSM_HD, :]
        for pr in range(rep // 2):
            ch0 = g * gw + pr * 2 * SSM_HD
            x_pair = xs[:, ch0:ch0 + 2 * SSM_HD]
            x_pair = _split2(x_pair) if precise else (x_pair.astype(BF16), None)
            y_pair = []
            for hh in (2 * pr, 2 * pr + 1):
                h = g * rep + hh
                lane = DT_LANE0 + h
                decay = jnp.exp(jnp.where(causal, acum_c[:, lane:lane + 1] - acum_r[h:h + 1, :], -jnp.inf))
                y_pair.append(_mm((cb * decay) * dt_r[h:h + 1, :], x_pair, precise))
            y_diag = jnp.where(first_of_pair, y_pair[0], y_pair[1])
            y_sc[:, ch0:ch0 + 2 * SSM_HD] = (y_diag + y_off[:, ch0 - g * gw:ch0 - g * gw + 2 * SSM_HD]) + dx[:, ch0:ch0 + 2 * SSM_HD]

    @pl.when(c == pl.num_programs(1) - 1)
    def _():
        hfin_ref[...] = h_sc[...]

    z = jnp.concatenate([z0_ref[...], z1_ref[...]], axis=1)
    yg =y_sc[...] * (z * _sigmoid(z))
    ms = jnp.mean(yg * yg, axis=-1, keepdims=True)
    yc_ref[...] = ((yg * lax.rsqrt(ms + RMS_EPS)) * ng_ref[...]).astype(yc_ref.dtype)


def _ssd_call(proj, small, dt_t, h0, conv0, lw, bsz, t, precise):
    L = min(SSD_L, t)
    nc = t // L
    zhalf = D_C // 2
    zblk = OFF_Z // zhalf

    def const(shape):
        return pl.BlockSpec(shape, lambda b, c: (0,) * len(shape))

    in_specs = [
        pl.BlockSpec((L, CONV_DIM), lambda b, c: (b * nc + c, OFF_XBC // CONV_DIM)),
        pl.BlockSpec((L, zhalf), lambda b, c: (b * nc + c, zblk)),
        pl.BlockSpec((L, zhalf), lambda b, c: (b * nc + c, zblk + 1)),
        pl.BlockSpec((L, LANES), lambda b, c: (b * nc + c, 0)),
        pl.BlockSpec((None, SSM_HEADS, L), lambda b, c: (b, 0, c)),
        pl.BlockSpec((None, SSM_HEADS, SSM_HD, SSM_N), lambda b, c: (b, 0, 0, 0)),
        pl.BlockSpec((None, CONV_W - 1, CONV_DIM), lambda b, c: (b, 0, 0)),
        const((CONV_W, CONV_DIM)), const((1, CONV_DIM)), const((1, LANES)), const((SSM_HEADS, 1)),
        const((1, LANES)), const((SSM_HEADS, 1)), const((1, D_C)), const((1, D_C)), const((1, D_C)),
        const((L, L)), const((L, L)), const((LANES, D_C)),
    ]
    expand = np.zeros((LANES, D_C), np.float32)
    for h in range(SSM_HEADS):
        expand[DT_LANE0 + h, h * SSM_HD:(h + 1) * SSM_HD] = 1.0
    out_specs = [
        pl.BlockSpec((L, D_C), lambda b, c: (b * nc + c, 0)),
        pl.BlockSpec((None, SSM_HEADS, SSM_HD, SSM_N), lambda b, c: (b, 0, 0, 0)),
        pl.BlockSpec((None, CONV_W - 1, CONV_DIM), lambda b, c: (b, 0, 0)),
    ]
    out_shape = [jax.ShapeDtypeStruct((bsz * t, D_C), _act_dtype(precise)),
                 jax.ShapeDtypeStruct((bsz, SSM_HEADS, SSM_HD, SSM_N), F32),
                 jax.ShapeDtypeStruct((bsz, CONV_W - 1, CONV_DIM), F32)]
    ones = jnp.ones((L, L), F32)
    return pl.pallas_call(
        functools.partial(_ssd_kernel, L=L, precise=precise), grid=(bsz, nc), in_specs=in_specs,
        out_specs=out_specs, out_shape=out_shape,
        scratch_shapes=[pltpu.VMEM((SSM_HEADS, SSM_HD, SSM_N), F32), pltpu.VMEM((CONV_PAD + L, CONV_DIM), F32),
                        pltpu.VMEM((L, D_C), F32)],
        compiler_params=_cparams(("parallel", "arbitrary")), name="ssd")(
            proj, proj, proj, small, dt_t, h0, conv0, lw["conv_w"], lw["conv_b"], lw["dtb_row"], lw["dtb_col"],
            lw["alog_row"], lw["alog_col"], lw["alog_e"], lw["dvec"], lw["norm_g"],
            jnp.tril(ones).astype(BF16), jnp.triu(ones).astype(BF16), jnp.asarray(expand, BF16))


def _merge_kernel(*refs, precise):
    it = iter(refs)
    ya_ref, ybt_ref, yc_ref, g0_ref, g1_ref, g2_ref, x_ref = [next(it) for _ in range(7)]
    wa, wb, wc, wo = [_take_w(it, precise) for _ in range(4)]
    lg_ref, lb_ref, o_ref, ob_ref = next(it), next(it), next(it), next(it)
    merged = g0_ref[...].astype(F32) * _mm(ya_ref[...], _ld(wa), precise)
    merged = merged + g1_ref[...].astype(F32) * _mm(ybt_ref[...], _ld(wb), precise, "tn")
    merged = merged + g2_ref[...].astype(F32) * _mm(yc_ref[...], _ld(wc), precise)
    mix = _mm(merged, _ld(wo), precise)
    y = _ln_rows(ALPHA * x_ref[...] + mix, lg_ref[...], lb_ref[...])
    o_ref[...] = y
    ob_ref[...] = y.astype(BF16)


def _merge_call(ya, yb_t, yc, gates_sig, x, lw, tm, bsz, t, precise):
    n = x.shape[0]
    per_b = t // tm

    def rows(w):
        return pl.BlockSpec((tm, w), lambda i: (i, 0))

    def const(shape):
        return pl.BlockSpec(shape, lambda i: (0, 0))

    gates = [pl.BlockSpec((tm, D_MODEL), lambda i, j=j: (i, j)) for j in range(N_BRANCH)]
    wspecs, wargs = [], []
    for name, k in (("w_a", D_AV), ("w_b", D_B), ("w_c", D_C), ("w_o", D_MODEL)):
        ws = _wargs(lw, name, precise)
        wargs += ws
        wspecs += [const((k, D_MODEL))] * len(ws)
    return pl.pallas_call(
        functools.partial(_merge_kernel, precise=precise), grid=(n // tm,),
        in_specs=[rows(D_AV), pl.BlockSpec((None, D_B, tm), lambda i: (i // per_b, 0, i % per_b)), rows(D_C)]
        + gates + [rows(D_MODEL)] + wspecs + [const((1, D_MODEL)), const((1, D_MODEL))],
        out_specs=[rows(D_MODEL), rows(D_MODEL)],
        out_shape=[jax.ShapeDtypeStruct((n, D_MODEL), F32), jax.ShapeDtypeStruct((n, D_MODEL), BF16)],
        compiler_params=_cparams(("parallel",)), name="merge")(
            ya, yb_t, yc, gates_sig, gates_sig, gates_sig, x, *wargs, lw["ln1_g"], lw["ln1_b"])


def _router_kernel(h_ref, whi_ref, wlo_ref, br_ref, o_ref, id_ref, w_ref):
    logits = _mm(h_ref[...], (whi_ref[...], wlo_ref[...]), True) + br_ref[...]
    lane = lax.broadcasted_iota(jnp.int32, logits.shape, 1)
    lg = jnp.where(lane < N_EXPERTS, logits, -jnp.inf)
    m1 = jnp.max(lg, axis=-1, keepdims=True)
    i1 = jnp.min(jnp.where(lg == m1, lane, LANES), axis=-1, keepdims=True)
    lg2 = jnp.where(lane == i1, -jnp.inf, lg)
    m2 = jnp.max(lg2, axis=-1, keepdims=True)
    i2 = jnp.min(jnp.where(lg2 == m2, lane, LANES), axis=-1, keepdims=True)
    e = jnp.exp(m2 - m1)
    den = 1.0 + e
    w1, w2 = 1.0 / den, e / den
    o_ref[...] = jnp.where(lane == i1, w1, 0.0) + jnp.where(lane == i2, w2, 0.0)
    id_ref[...] = jnp.where(lane == 0, i1, jnp.where(lane == 1, i2, 0))
    w_ref[...] = jnp.where(lane == 0, w1, jnp.where(lane == 1, w2, 0.0))


def _router_call(h, lw, tm):
    n = h.shape[0]
    out = pl.BlockSpec((tm, LANES), lambda i: (i, 0))
    return pl.pallas_call(
        _router_kernel, grid=(n // tm,),
        in_specs=[pl.BlockSpec((tm, D_MODEL), lambda i: (i, 0)), pl.BlockSpec((D_MODEL, LANES), lambda i: (0, 0)),
                  pl.BlockSpec((D_MODEL, LANES), lambda i: (0, 0)), pl.BlockSpec((1, LANES), lambda i: (0, 0))],
        out_specs=[out, out, out],
        out_shape=[jax.ShapeDtypeStruct((n, LANES), F32), jax.ShapeDtypeStruct((n, LANES), jnp.int32),
                   jax.ShapeDtypeStruct((n, LANES), F32)],
        compiler_params=_cparams(("parallel",)), name="router")(h, lw["wr"], lw["wr_lo"], lw["br"])


MOE_ROWS = 256
MOE_TM = 256
TOP_K = 2


def _moe_plan(ids, n):
    r = MOE_ROWS
    e = ids[:, :TOP_K].reshape(-1)
    na = n * TOP_K
    n_tiles = na // r + N_EXPERTS
    order = jnp.argsort(e, stable=True).astype(jnp.int32)
    inv = jnp.argsort(order).astype(jnp.int32)
    onehot = (e[:, None] == jnp.arange(N_EXPERTS, dtype=jnp.int32)[None, :]).astype(jnp.int32)
    counts = jnp.sum(onehot, axis=0)
    padded = ((counts + r - 1) // r) * r
    ends = jnp.cumsum(padded)
    starts = ends - padded
    first = jnp.cumsum(counts) - counts
    shift = starts - first
    pos = inv + jnp.sum(onehot * shift[None, :], axis=1)
    tile_row0 = jnp.arange(n_tiles, dtype=jnp.int32) * r
    tile_expert = jnp.minimum(jnp.sum(ends[None, :] <= tile_row0[:, None], axis=1), N_EXPERTS - 1).astype(jnp.int32)
    last = (first + counts - 1)[tile_expert]
    srt = jnp.minimum(tile_row0[:, None] - shift[tile_expert][:, None] + jnp.arange(r, dtype=jnp.int32)[None, :],
                      last[:, None])
    src_tok = order[jnp.clip(srt, 0, na - 1)] // TOP_K
    n_used = (ends[-1] // r).astype(jnp.int32).reshape(1)
    return src_tok.reshape(n_tiles, 1, r), pos.reshape(n // MOE_TM, 1, MOE_TM * TOP_K), tile_expert, n_used


def _row_copy(src_hbm, dst_vmem, sem, src_row, dst_row):
    return pltpu.make_async_copy(src_hbm.at[pl.ds(src_row, 1), :], dst_vmem.at[pl.ds(dst_row, 1), :], sem)


def _experts_kernel(te_ref, nu_ref, src_ref, nxt_ref, h_hbm, wg_ref, wu_ref, wd_ref, y_ref, x_buf, sem):
    i = pl.program_id(0)
    r = x_buf.shape[1]
    slot = i & 1

    def gather(idx_ref, s):
        for k in range(r):
            _row_copy(h_hbm, x_buf.at[s], sem.at[s], idx_ref[0, k], k).start(priority=k % 2)

    @pl.when(i < nu_ref[0])
    def _():
        @pl.when(i == 0)
        def _():
            gather(src_ref, 0)

        @pl.when(i + 1 < nu_ref[0])
        def _():
            gather(nxt_ref, 1 - slot)

        for k in range(r):
            _row_copy(h_hbm, x_buf.at[slot], sem.at[slot], 0, k).wait()
        xb = x_buf[slot].astype(BF16)
        gate = _dot(xb, wg_ref[...])
        up = _dot(xb, wu_ref[...])
        y_ref[...] = _dot(((gate * _sigmoid(gate)) * up).astype(BF16), wd_ref[...])

    @pl.when(i >= nu_ref[0])
    def _():
        y_ref[...] = jnp.zeros_like(y_ref)


def _experts_call(h, src_tok, tile_expert, n_used, lw):
    n_tiles, _, r = src_tok.shape
    wg_spec = pl.BlockSpec((None, D_MODEL, D_FF_EXPERT), lambda i, te, nu: (te[i], 0, 0))
    wd_spec = pl.BlockSpec((None, D_FF_EXPERT, D_MODEL), lambda i, te, nu: (te[i], 0, 0))
    grid_spec = pltpu.PrefetchScalarGridSpec(
        num_scalar_prefetch=2, grid=(n_tiles,),
        in_specs=[pl.BlockSpec((None, 1, r), lambda i, te, nu: (i, 0, 0), memory_space=pltpu.SMEM),
                  pl.BlockSpec((None, 1, r), lambda i, te, nu: (jnp.minimum(i + 1, n_tiles - 1), 0, 0),
                               memory_space=pltpu.SMEM),
                  pl.BlockSpec(memory_space=pl.ANY), wg_spec, wg_spec, wd_spec],
        out_specs=pl.BlockSpec((r, D_MODEL), lambda i, te, nu: (i, 0)),
        scratch_shapes=[pltpu.VMEM((2, r, D_MODEL), F32), pltpu.SemaphoreType.DMA((2,))])
    return pl.pallas_call(
        _experts_kernel, grid_spec=grid_spec,
        out_shape=jax.ShapeDtypeStruct((n_tiles * r, D_MODEL), F32),
        compiler_params=_cparams(("arbitrary",)), name="moe_experts")(
            tile_expert, n_used, src_tok, src_tok, h, lw["w_gate"], lw["w_up"], lw["w_down"])


def _combine_kernel(pos_ref, nxt_ref, hb_ref, h_ref, p_ref, w_ref, y_hbm, wp_ref, wpg_ref, lg_ref, lb_ref,
                    o_ref, ob_ref, y_buf, sem):
    i = pl.program_id(0)
    tm = h_ref.shape[0]
    slot = i & 1

    def gather(idx_ref, s):
        for k in range(tm):
            for j in range(TOP_K):
                _row_copy(y_hbm, y_buf.at[s, j], sem.at[s], idx_ref[0, TOP_K * k + j], k).start(priority=j % 2)

    @pl.when(i == 0)
    def _():
        gather(pos_ref, 0)

    @pl.when(i + 1 < pl.num_programs(0))
    def _():
        gather(nxt_ref, 1 - slot)

    hb = hb_ref[...]
    ple = _dot(p_ref[...].astype(BF16), wp_ref[...]) * _sigmoid(_dot(hb, wpg_ref[...]))
    acc = ALPHA * h_ref[...] + ple
    for k in range(tm):
        for j in range(TOP_K):
            _row_copy(y_hbm, y_buf.at[slot, j], sem.at[slot], 0, k).wait()
    w = w_ref[...]
    for j in range(TOP_K):
        acc = acc + w[:, j:j + 1] * y_buf[slot, j]
    y = _ln_rows(acc, lg_ref[...], lb_ref[...])
    o_ref[...] = y
    ob_ref[...] = y.astype(BF16)


def _combine_call(hb, h, p, wsel, pos, y_sorted, lw):
    n = h.shape[0]
    tm = MOE_TM

    def rows(w):
        return pl.BlockSpec((tm, w), lambda i: (i, 0))

    def const(shape):
        return pl.BlockSpec(shape, lambda i: (0, 0))

    return pl.pallas_call(
        _combine_kernel, grid=(n // tm,),
        in_specs=[pl.BlockSpec((None, 1, tm * TOP_K), lambda i: (i, 0, 0), memory_space=pltpu.SMEM),
                  pl.BlockSpec((None, 1, tm * TOP_K), lambda i: (jnp.minimum(i + 1, n // tm - 1), 0, 0),
                               memory_space=pltpu.SMEM),
                  rows(D_MODEL), rows(D_MODEL), rows(PLE_DIM), rows(LANES), pl.BlockSpec(memory_space=pl.ANY),
                  const((PLE_DIM, D_MODEL)), const((D_MODEL, D_MODEL)), const((1, D_MODEL)), const((1, D_MODEL))],
        out_specs=[rows(D_MODEL), rows(D_MODEL)],
        out_shape=[jax.ShapeDtypeStruct((n, D_MODEL), F32), jax.ShapeDtypeStruct((n, D_MODEL), BF16)],
        scratch_shapes=[pltpu.VMEM((2, TOP_K, tm, D_MODEL), F32), pltpu.SemaphoreType.DMA((2,))],
        compiler_params=_cparams(("arbitrary",)), name="moe_combine")(
            pos, pos, hb, h, p, wsel, y_sorted, lw["w_ple"], lw["w_pleg"], lw["ln2_g"], lw["ln2_b"])


def _ffn_kernel(*refs, weighted, precise):
    it = iter(refs)
    hb_ref, h_ref, p_ref = next(it), next(it), next(it)
    comb_ref = next(it) if weighted else None
    wg, wu, wd, wp, wpg = [_take_w(it, precise) for _ in range(5)]
    lg_ref, lb_ref, o_ref, ob_ref, acc_sc = next(it), next(it), next(it), next(it), next(it)
    j = pl.program_id(1)
    hb = h_ref[...] if precise else hb_ref[...]

    @pl.when(j == 0)
    def _():
        ple = _mm(p_ref[...], _ld(wp), precise) * _sigmoid(_mm(hb, _ld(wpg), precise))
        acc_sc[...] = ALPHA * h_ref[...] + ple

    gate = _mm(hb, _ld(wg), precise)
    up = _mm(hb, _ld(wu), precise)
    out = _mm((gate * _sigmoid(gate)) * up, _ld(wd), precise)
    if weighted:
        comb = comb_ref[...]
        lane = lax.broadcasted_iota(jnp.int32, comb.shape, 1)
        out = jnp.sum(jnp.where(lane == j, comb, 0.0), axis=-1, keepdims=True) * out
    acc_sc[...] += out

    @pl.when(j == pl.num_programs(1) - 1)
    def _():
        y = _ln_rows(acc_sc[...], lg_ref[...], lb_ref[...])
        o_ref[...] = y
        ob_ref[...] = y.astype(BF16)


def _ffn_call(hb, h, p, comb, lw, tm, precise):
    n = h.shape[0]
    weighted = comb is not None
    if weighted:
        n_e = N_EXPERTS
        wg_spec = pl.BlockSpec((None, D_MODEL, D_FF_EXPERT), lambda i, j: (j, 0, 0))
        wd_spec = pl.BlockSpec((None, D_FF_EXPERT, D_MODEL), lambda i, j: (j, 0, 0))
    else:
        n_e = D_FF // D_FF_EXPERT
        wg_spec = pl.BlockSpec((D_MODEL, D_FF_EXPERT), lambda i, j: (0, j))
        wd_spec = pl.BlockSpec((D_FF_EXPERT, D_MODEL), lambda i, j: (j, 0))

    def rows(w):
        return pl.BlockSpec((tm, w), lambda i, j: (i, 0))

    def const(shape):
        return pl.BlockSpec(shape, lambda i, j: (0, 0))

    wspecs, wargs = [], []
    for name, spec in (("w_gate", wg_spec), ("w_up", wg_spec), ("w_down", wd_spec),
                       ("w_ple", const((PLE_DIM, D_MODEL))), ("w_pleg", const((D_MODEL, D_MODEL)))):
        ws = _wargs(lw, name, precise)
        wargs += ws
        wspecs += [spec] * len(ws)
    return pl.pallas_call(
        functools.partial(_ffn_kernel, weighted=weighted, precise=precise), grid=(n // tm, n_e),
        in_specs=[rows(D_MODEL), rows(D_MODEL), rows(PLE_DIM)] + [rows(LANES)] * weighted + wspecs
        + [const((1, D_MODEL)), const((1, D_MODEL))],
        out_specs=[rows(D_MODEL), rows(D_MODEL)],
        out_shape=[jax.ShapeDtypeStruct((n, D_MODEL), F32), jax.ShapeDtypeStruct((n, D_MODEL), BF16)],
        scratch_shapes=[pltpu.VMEM((tm, D_MODEL), F32)],
        compiler_params=_cparams(("parallel", "arbitrary")), name="ffn")(
            hb, h, p, *([comb] * weighted), *wargs, lw["ln2_g"], lw["ln2_b"])


def _layer_weights(i, W):
    w_in = W["w_in"][i]
    sizes = [D_AV, D_AV, D_B, D_B, D_B, FOX_HEADS, D_C, CONV_DIM, SSM_HEADS, N_BRANCH * D_MODEL]
    o = [0] + [int(s) for s in np.cumsum(sizes)]
    seg = [w_in[:, o[k]:o[k + 1]] for k in range(len(sizes))]
    w_u, w_v, w_q, w_k, w_va, w_f, w_z, w_xbc, w_dt, w_gate = seg
    pad = jnp.zeros((D_MODEL, LANES - SSM_HEADS - FOX_HEADS), F32)

    def lane_row(v, lane0=0):
        return jnp.pad(v.astype(F32), (lane0, LANES - lane0 - v.shape[0])).reshape(1, LANES)

    lw = dict(
        fox_b=lane_row(W["fox_b_forget"][i]),
        gmlp_g=W["gmlp_ln_g"][i], gmlp_b=W["gmlp_ln_b"][i], gmlp_w=W["gmlp_w_spatial"][i], gmlp_bs=W["gmlp_b_spatial"][i],
        conv_w=W["ssm_conv_w"][i], conv_b=W["ssm_conv_b"][i].reshape(1, CONV_DIM),
        dtb_row=lane_row(W["ssm_dt_bias"][i], DT_LANE0), dtb_col=W["ssm_dt_bias"][i].reshape(SSM_HEADS, 1),
        alog_row=lane_row(W["ssm_a_log"][i], DT_LANE0), alog_col=W["ssm_a_log"][i].reshape(SSM_HEADS, 1),
        alog_e=jnp.repeat(W["ssm_a_log"][i], SSM_HD).reshape(1, D_C),
        dvec=jnp.repeat(W["ssm_d"][i], SSM_HD).reshape(1, D_C), norm_g=W["ssm_norm_g"][i].reshape(1, D_C),
        ln1_g=W["ln1_g"][i].reshape(1, D_MODEL), ln1_b=W["ln1_b"][i].reshape(1, D_MODEL),
        ln2_g=W["ln2_g"][i].reshape(1, D_MODEL), ln2_b=W["ln2_b"][i].reshape(1, D_MODEL),
    )
    j = i // 2
    mats = dict(
        w_main=jnp.concatenate([w_xbc, w_z, w_u, w_v], axis=1), w_bgate=w_gate, w_k=w_k, w_va=w_va,
        w_small=jnp.concatenate([w_f, w_dt, pad], axis=1),
        w_qt=jnp.transpose(w_q), w_vt=jnp.transpose(w_va),
        w_a=W["w_branch_a"][i], w_b=W["w_branch_b"][i], w_c=W["w_branch_c"][i], w_o=W["w_out"][i],
        w_ple=W["ple_w_proj"][i], w_pleg=W["ple_w_gate"][i],
    )
    if i % 2 == 0:
        mats.update(w_gate=W["ffn_w_gate"][j], w_up=W["ffn_w_up"][j], w_down=W["ffn_w_down"][j])
    else:
        mats.update(w_gate=W["moe_w_gate"][j], w_up=W["moe_w_up"][j], w_down=W["moe_w_down"][j],
                    wr=jnp.pad(W["moe_w_router"][j], ((0, 0), (0, LANES - N_EXPERTS))))
        lw["br"] = lane_row(W["moe_b_router"][j])
    for name, w in mats.items():
        c = w * np.float32(2.0**16 + 1.0)
        hi_f = c - (c - w)
        lw[name] = hi_f.astype(BF16)
        lw[name + "_lo"] = (w - hi_f).astype(BF16)
    return lw


def _trunk_layer(x, xb, p, i, lw, bsz, t, fox_cache, ssm0, conv0, tm, precise):
    n = bsz * t
    act = _act_dtype(precise)
    xin = x if precise else xb
    tmm = min(tm, t)
    tbig = min(n, PROJ_TM)
    proj = _matmul_call(xin, lw, "w_main", tbig, MAIN_TN, precise, "in_proj")
    gates_sig = _matmul_call(xin, lw, "w_bgate", tbig, GATE_TN, precise, "in_proj_gates", gate=True)
    k_new, v_new, small, logf = _kvs_call(xin, lw, tm, precise)

    w_eff, bias_eff = _gmlp_weights(lw["gmlp_w"], lw["gmlp_bs"], t)
    y_a, v_n = _gmlp_call(proj, lw["gmlp_g"], lw["gmlp_b"], w_eff, bias_eff, min(n, TOKEN_TM), precise)

    k_new = k_new.reshape(bsz, t, D_B)
    v_new = v_new.reshape(bsz, t, FOX_HEADS, FOX_HD)
    logf = logf.reshape(bsz, t, FOX_HEADS)
    q_t, v_t = _qvt_call(xin, lw, tmm, bsz, t, precise)
    if fox_cache is None:
        k_all, logf_all, q_off = k_new, logf, 0
    else:
        k_c, v_c, logf_c = fox_cache
        q_off = k_c.shape[1]
        k_all = jnp.concatenate([k_c.reshape(bsz, q_off, D_B), k_new], axis=1)
        v_t = jnp.concatenate([jnp.transpose(v_c.reshape(bsz, q_off, D_B), (0, 2, 1)).astype(act), v_t], axis=2)
        logf_all = jnp.concatenate([logf_c, logf], axis=1)
    t_k = k_all.shape[1]
    k_a = _keys_call(k_all, logf_all, FOX_TILE if t_k % FOX_TILE == 0 else t_k, precise)
    tq, tk = (FOX_TILE, FOX_TILE) if t % FOX_TILE == 0 else (t, t_k)
    y_bt = _fox_call(q_t, k_a, v_t, tq, tk, q_off, FOX_NH, precise)

    dt_t = jnp.transpose(small[:, DT_LANE0:DT_LANE0 + SSM_HEADS].reshape(bsz, t, SSM_HEADS), (0, 2, 1))
    y_c, h_fin, conv_new = _ssd_call(proj, small, dt_t, ssm0, conv0, lw, bsz, t, precise)

    x1, x1b = _merge_call(y_a, y_bt, y_c, gates_sig, x, lw, tmm, bsz, t, precise)
    if i % 2 == 0:
        x2, x2b = _ffn_call(x1b, x1, p, None, lw, tm, precise)
    else:
        comb, ids, wsel = _router_call(x1, lw, tm)
        if precise or n % MOE_TM or n < N_EXPERTS * MOE_ROWS:
            x2, x2b = _ffn_call(x1b, x1, p, comb, lw, tm, precise)
        else:
            src_tok, pos, tile_expert, n_used = _moe_plan(ids, n)
            y_sorted = _experts_call(x1, src_tok, tile_expert, n_used, lw)
            x2, x2b = _combine_call(x1b, x1, p, wsel, pos, y_sorted, lw)
    return x2, x2b, (k_new.reshape(bsz, t, FOX_HEADS, FOX_HD), v_new, logf, h_fin, conv_new,
                     v_n.reshape(bsz, t, D_AV))


def kernel(x_prompt, x_sample, p_prompt, p_sample, cache_fox_k, cache_fox_v, cache_fox_logf, state_ssm, state_conv, ln0_g, ln0_b, w_in, fox_b_forget, gmlp_ln_g, gmlp_ln_b, gmlp_w_spatial, gmlp_b_spatial, ssm_conv_w, ssm_conv_b, ssm_dt_bias, ssm_a_log, ssm_d, ssm_norm_g, w_branch_a, w_branch_b, w_branch_c, w_out, ln1_g, ln1_b, ln2_g, ln2_b, ffn_w_gate, ffn_w_up, ffn_w_down, moe_w_router, moe_b_router, moe_w_gate, moe_w_up, moe_w_down, ple_w_proj, ple_w_gate):
    W = dict(w_in=w_in, fox_b_forget=fox_b_forget, gmlp_ln_g=gmlp_ln_g, gmlp_ln_b=gmlp_ln_b,
             gmlp_w_spatial=gmlp_w_spatial, gmlp_b_spatial=gmlp_b_spatial, ssm_conv_w=ssm_conv_w,
             ssm_conv_b=ssm_conv_b, ssm_dt_bias=ssm_dt_bias, ssm_a_log=ssm_a_log, ssm_d=ssm_d,
             ssm_norm_g=ssm_norm_g, w_branch_a=w_branch_a, w_branch_b=w_branch_b,
             w_branch_c=w_branch_c, w_out=w_out, ln1_g=ln1_g, ln1_b=ln1_b, ln2_g=ln2_g, ln2_b=ln2_b,
             ffn_w_gate=ffn_w_gate, ffn_w_up=ffn_w_up, ffn_w_down=ffn_w_down,
             moe_w_router=moe_w_router, moe_b_router=moe_b_router, moe_w_gate=moe_w_gate, moe_w_up=moe_w_up,
             moe_w_down=moe_w_down, ple_w_proj=ple_w_proj, ple_w_gate=ple_w_gate)
    bp, tp, _ = x_prompt.shape
    bs, ts, _ = x_sample.shape
    tm_p = TOKEN_TM if (bp * tp) % TOKEN_TM == 0 else bp * tp
    tm_s = bs * ts
    xp, xpb = _layer_norm_call(x_prompt.reshape(bp * tp, D_MODEL), ln0_g, ln0_b, tm_p)
    xs, xsb = _layer_norm_call(x_sample.reshape(bs * ts, D_MODEL), ln0_g, ln0_b, tm_s)
    outs_p, outs_s = [], []
    for i in range(DEPTH):
        lw = _layer_weights(i, W)
        ssm0 = jnp.zeros((bp, SSM_HEADS, SSM_HD, SSM_N), F32)
        conv0 = jnp.zeros((bp, CONV_W - 1, CONV_DIM), F32)
        xp, xpb, st_p = _trunk_layer(xp, xpb, p_prompt[i].reshape(bp * tp, PLE_DIM), i, lw, bp, tp,
                                     None, ssm0, conv0, tm_p, False)
        xs, xsb, st_s = _trunk_layer(xs, xsb, p_sample[i].reshape(bs * ts, PLE_DIM), i, lw, bs, ts,
                                     (cache_fox_k[i], cache_fox_v[i], cache_fox_logf[i]),
                                     state_ssm[i], state_conv[i], tm_s, True)
        outs_p.append(st_p)
        outs_s.append(st_s)

    def stack(outs, k):
        return jnp.stack([o[k] for o in outs])

    return (xp.reshape(bp, tp, D_MODEL), xs.reshape(bs, ts, D_MODEL),
            stack(outs_p, 0), stack(outs_p, 1), stack(outs_p, 2), stack(outs_p, 3), stack(outs_p, 4),
            stack(outs_s, 0), stack(outs_s, 1), stack(outs_s, 2), stack(outs_s, 3), stack(outs_s, 4), stack(outs_s, 5))
```
